```python
import jax
import jax.numpy as jnp
from jax import lax
import numpy as np

D_MODEL = 4096
BATCH = 1
SEQ = 8192
DEPTH = 2

N_META = 16
N_EVEN = (DEPTH + 1) // 2
N_ODD = DEPTH // 2
A_HEADS = 16
A_KDIM = 128
A_VDIM = D_MODEL // 2 // A_HEADS
A_FDIM = A_HEADS * A_KDIM
A_WIDTH = A_HEADS * A_VDIM
HGRN_CHUNK = 64
HGRN_SUB = 16
B_HDIM = 64
B_QHEADS = D_MODEL // 2 // B_HDIM
B_KVHEADS = B_QHEADS // 8
B_GROUP = B_QHEADS // B_KVHEADS
B_WIDTH = B_QHEADS * B_HDIM
B_KVWIDTH = B_KVHEADS * B_HDIM
WINDOW = 128
ATTN_BLOCK = 128
EVEN_SPLITS = (A_FDIM, A_FDIM, A_WIDTH, A_WIDTH, B_WIDTH, B_KVWIDTH, B_KVWIDTH)
EVEN_IN = sum(EVEN_SPLITS)
EVEN_MIX = A_WIDTH + B_WIDTH
LRU_WIDTH = D_MODEL
LRU_BLOCKS = 16
LRU_BDIM = LRU_WIDTH // LRU_BLOCKS
LRU_CONV = 4
LRU_C = 8.0
D_FF = 256 * ((8 * D_MODEL // 3 + 255) // 256)
FFN_CONV = 3
NORM_EPS = 1e-6
NEG_INF = -1e30

kernel_name = 'hybrid_hgrn2_swa_rglru_block'


def rmsnorm(x, w):
    xf = x.astype(jnp.float32)
    y = xf * lax.rsqrt(jnp.mean(xf * xf, axis=-1, keepdims=True) + NORM_EPS)
    return (y * w.astype(jnp.float32)).astype(x.dtype)


def causal_dwconv(x, w, b):
    width = w.shape[0]
    seq_len = x.shape[1]
    xp = jnp.pad(x, ((0, 0), (width - 1, 0), (0, 0)))
    out = b
    for tap in range(width):
        out = out + xp[:, tap:tap + seq_len] * w[tap]
    return out


def _split(h, sizes):
    idx = np.cumsum(sizes)[:-1].tolist()
    return jnp.split(h, idx, axis=-1)


def hgrn2_heads(q_raw, f_raw, i_raw, g_raw, lb, gn_w):
    bsz, seq_len, _ = q_raw.shape
    f32 = jnp.float32
    pad = HGRN_CHUNK - N_META
    n_sub = HGRN_CHUNK // HGRN_SUB
    q = jax.nn.silu(q_raw.astype(f32))
    forget = lb + (1.0 - lb) * jax.nn.sigmoid(f_raw.astype(f32))
    k = 1.0 - forget
    g = jnp.log(forget)
    v = i_raw.astype(f32)

    def to_chunks(t, dh):
        t = jnp.pad(t, ((0, 0), (pad, 0), (0, 0)))
        n_c = t.shape[1] // HGRN_CHUNK
        return t.reshape(bsz, n_c, HGRN_CHUNK, A_HEADS, dh).transpose(0, 3, 1, 2, 4)

    q = to_chunks(q, A_KDIM)
    k = to_chunks(k, A_KDIM)
    g = to_chunks(g, A_KDIM)
    v = to_chunks(v, A_VDIM)
    b = jnp.cumsum(g, axis=-2)
    b_last = b[..., -1:, :]
    u = jnp.einsum('bhnck,bhncv->bhnkv', k * jnp.exp(b_last - b), v)
    decay = jnp.exp(b_last[..., 0, :])

    def chunk_step(state, inp):
        d, uc = inp
        return d[..., None] * state + uc, state

    s0 = jnp.zeros((bsz, A_HEADS, A_KDIM, A_VDIM), f32)
    _, s_start = lax.scan(chunk_step, s0, (jnp.moveaxis(decay, 2, 0), jnp.moveaxis(u, 2, 0)))
    s_start = jnp.moveaxis(s_start, 0, 2)
    o_inter = jnp.einsum('bhnck,bhnkv->bhncv', q * jnp.exp(b), s_start)

    sub_shape = b.shape[:3] + (n_sub, HGRN_SUB, A_KDIM)
    b_sub = b.reshape(sub_shape)
    ref = jnp.concatenate([jnp.zeros_like(b_sub[..., :1, 0, :]), b_sub[..., :-1, -1, :]], axis=-2)
    q_sc = q.reshape(sub_shape) * jnp.exp(b_sub - ref[..., None, :])
    pos = jnp.arange(HGRN_CHUNK)
    key_ok = pos[None, :] < (jnp.arange(n_sub)[:, None] + 1) * HGRN_SUB
    expo = jnp.where(key_ok[:, :, None], ref[..., :, None, :] - b[..., None, :, :], 0.0)
    k_sc = k[..., None, :, :] * jnp.exp(expo)
    att = jnp.einsum('bhnitk,bhnisk->bhnits', q_sc, k_sc)
    causal = pos.reshape(n_sub, HGRN_SUB)[:, :, None] >= pos[None, None, :]
    att = jnp.where(causal, att, 0.0)
    o_intra = jnp.einsum('bhnits,bhnsv->bhnitv', att, v).reshape(o_inter.shape)
    o = jnp.moveaxis(o_inter + o_intra, 1, 3)
    o = o.reshape(bsz, -1, A_HEADS, A_VDIM)[:, pad:]
    gate = jax.nn.silu(g_raw.astype(f32)).reshape(bsz, seq_len, A_HEADS, A_VDIM)
    o = rmsnorm(o, gn_w) * gate
    return o.reshape(bsz, seq_len, A_WIDTH).astype(q_raw.dtype)


def swa_sink_heads(q_raw, k_raw, v_raw, sinks):
    bsz, seq_len, _ = q_raw.shape
    f32 = jnp.float32
    pad = ATTN_BLOCK - N_META
    padw = ((0, 0), (pad, 0), (0, 0))
    qp = jnp.pad(q_raw, padw)
    kp = jnp.pad(k_raw, padw)
    vp = jnp.pad(v_raw, padw)
    n_b = qp.shape[1] // ATTN_BLOCK
    q = qp.reshape(bsz, n_b, ATTN_BLOCK, B_KVHEADS, B_GROUP, B_HDIM)
    k = kp.reshape(bsz, n_b, ATTN_BLOCK, B_KVHEADS, B_HDIM)
    v = vp.reshape(bsz, n_b, ATTN_BLOCK, B_KVHEADS, B_HDIM)

    def band(t):
        prev = jnp.pad(t, ((0, 0), (1, 0), (0, 0), (0, 0), (0, 0)))[:, :-1]
        return jnp.concatenate([prev, t], axis=2)

    k_band = band(k)
    v_band = band(v)
    k_meta = k_raw[:, :N_META].reshape(bsz, N_META, B_KVHEADS, B_HDIM)
    v_meta = v_raw[:, :N_META].reshape(bsz, N_META, B_KVHEADS, B_HDIM)
    scale = B_HDIM ** -0.5
    s_meta = jnp.einsum('bnqhgd,bmhd->bnhgqm', q, k_meta, preferred_element_type=f32) * scale
    s_band = jnp.einsum('bnqhgd,bnkhd->bnhgqk', q, k_band, preferred_element_type=f32) * scale
    qpos = jnp.arange(n_b)[:, None] * ATTN_BLOCK + jnp.arange(ATTN_BLOCK)[None, :]
    kpos = (jnp.arange(n_b)[:, None] - 1) * ATTN_BLOCK + jnp.arange(2 * ATTN_BLOCK)[None, :]
    qp3 = qpos[:, :, None]
    kp3 = kpos[:, None, :]
    band_ok = (kp3 <= qp3) & (qp3 - kp3 < WINDOW) & (kp3 >= ATTN_BLOCK)
    meta_ok = (qp3 >= ATTN_BLOCK) | (qp3 - pad >= jnp.arange(N_META)[None, None, :])
    sink = jnp.broadcast_to(sinks.astype(f32).reshape(1, 1, B_KVHEADS, B_GROUP, 1, 1), s_band.shape[:-1] + (1,))
    logits = jnp.concatenate([
        jnp.where(meta_ok[None, :, None, None], s_meta, NEG_INF),
        jnp.where(band_ok[None, :, None, None], s_band, NEG_INF),
        sink], axis=-1)
    p = jax.nn.softmax(logits, axis=-1).astype(v_raw.dtype)
    out = (jnp.einsum('bnhgqm,bmhd->bnqhgd', p[..., :N_META], v_meta)
           + jnp.einsum('bnhgqk,bnkhd->bnqhgd', p[..., N_META:N_META + 2 * ATTN_BLOCK], v_band))
    return out.reshape(bsz, n_b * ATTN_BLOCK, B_WIDTH)[:, pad:]


def even_mixer(xn, w_in, lb, gn_w, sinks, w_out):
    h = xn @ w_in
    q_a, f_a, i_a, g_a, q_b, k_b, v_b = _split(h, EVEN_SPLITS)
    o_a = hgrn2_heads(q_a, f_a, i_a, g_a, lb, gn_w)
    o_b = swa_sink_heads(q_b, k_b, v_b, sinks)
    return jnp.concatenate([o_a, o_b], axis=-1) @ w_out


def rglru(x, w_a, b_a, w_x, b_x, a_param):
    bsz, seq_len, _ = x.shape
    f32 = jnp.float32
    xf = x.astype(f32)
    xb = xf.reshape(bsz, seq_len, LRU_BLOCKS, LRU_BDIM)
    gate_r = jax.nn.sigmoid(jnp.einsum('blhi,hij->blhj', xb, w_a.astype(f32)).reshape(bsz, seq_len, LRU_WIDTH) + b_a.astype(f32))
    gate_i = jax.nn.sigmoid(jnp.einsum('blhi,hij->blhj', xb, w_x.astype(f32)).reshape(bsz, seq_len, LRU_WIDTH) + b_x.astype(f32))
    log_a = LRU_C * gate_r * jax.nn.log_sigmoid(a_param.astype(f32))
    a = jnp.exp(log_a)
    inp = jnp.sqrt(-jnp.expm1(2.0 * log_a)) * (gate_i * xf)

    def combine(left, right):
        a_l, b_l = left
        a_r, b_r = right
        return a_l * a_r, a_r * b_l + b_r

    _, h = lax.associative_scan(combine, (a, inp), axis=1)
    return h.astype(x.dtype)


def odd_mixer(xn, w_in, conv_w, conv_b, w_a, b_a, w_x, b_x, a_param, w_out):
    y_br, x_br = jnp.split(xn @ w_in, 2, axis=-1)
    y_br = jax.nn.gelu(y_br, approximate=True)
    x_br = causal_dwconv(x_br, conv_w, conv_b)
    return (rglru(x_br, w_a, b_a, w_x, b_x, a_param) * y_br) @ w_out


def conv_ffn(xn, w_gu, conv_w, conv_b, w_down):
    gate, up = jnp.split(xn @ w_gu, 2, axis=-1)
    gate = causal_dwconv(gate, conv_w, conv_b)
    return (jax.nn.gelu(gate, approximate=True) * up) @ w_down


def setup_inputs(seed: int = 0) -> dict:
    key = jax.random.key(seed)
    ks = jax.random.split(key, 24)
    f32 = jnp.float32
    nrm = jax.random.normal

    def dense(k, shape, fan_in):
        return nrm(k, shape, f32) * fan_in ** -0.5

    x = nrm(ks[0], (BATCH, SEQ, D_MODEL), f32)
    meta_tokens = nrm(ks[1], (N_META, D_MODEL), f32)
    norm_w = 1.0 + 0.02 * nrm(ks[2], (DEPTH, 4, D_MODEL), f32)
    w_in_even = dense(ks[3], (N_EVEN, D_MODEL, EVEN_IN), D_MODEL)
    lb_logits = 0.1 * nrm(ks[4], (N_EVEN + 1, A_FDIM), f32)
    hgrn_gn_w = 1.0 + 0.02 * nrm(ks[5], (N_EVEN, A_VDIM), f32)
    attn_sinks = nrm(ks[6], (N_EVEN, B_QHEADS), f32)
    w_out_even = dense(ks[7], (N_EVEN, EVEN_MIX, D_MODEL), EVEN_MIX)
    w_in_odd = dense(ks[8], (N_ODD, D_MODEL, 2 * LRU_WIDTH), D_MODEL)
    lru_conv_w = dense(ks[9], (N_ODD, LRU_CONV, LRU_WIDTH), LRU_CONV)
    lru_conv_b = 0.01 * nrm(ks[10], (N_ODD, LRU_WIDTH), f32)
    lru_wa = dense(ks[11], (N_ODD, LRU_BLOCKS, LRU_BDIM, LRU_BDIM), LRU_BDIM)
    lru_ba = 0.01 * nrm(ks[12], (N_ODD, LRU_WIDTH), f32)
    lru_wx = dense(ks[13], (N_ODD, LRU_BLOCKS, LRU_BDIM, LRU_BDIM), LRU_BDIM)
    lru_bx = 0.01 * nrm(ks[14], (N_ODD, LRU_WIDTH), f32)
    a0 = jax.random.uniform(ks[15], (N_ODD, LRU_WIDTH), f32, 0.9, 0.999) ** (1.0 / LRU_C)
    lru_a_param = jnp.log(a0) - jnp.log1p(-a0)
    w_out_odd = dense(ks[16], (N_ODD, LRU_WIDTH, D_MODEL), LRU_WIDTH)
    ffn_w_gu = dense(ks[17], (DEPTH, D_MODEL, 2 * D_FF), D_MODEL)
    ffn_conv_w = dense(ks[18], (DEPTH, FFN_CONV, D_FF), FFN_CONV)
    ffn_conv_b = 0.01 * nrm(ks[19], (DEPTH, D_FF), f32)
    ffn_w_down = dense(ks[20], (DEPTH, D_FF, D_MODEL), D_FF)
    return {
        'x': x, 'meta_tokens': meta_tokens, 'norm_w': norm_w,
        'w_in_even': w_in_even, 'lb_logits': lb_logits, 'hgrn_gn_w': hgrn_gn_w,
        'attn_sinks': attn_sinks, 'w_out_even': w_out_even,
        'w_in_odd': w_in_odd, 'lru_conv_w': lru_conv_w, 'lru_conv_b': lru_conv_b,
        'lru_wa': lru_wa, 'lru_ba': lru_ba, 'lru_wx': lru_wx, 'lru_bx': lru_bx,
        'lru_a_param': lru_a_param, 'w_out_odd': w_out_odd,
        'ffn_w_gu': ffn_w_gu, 'ffn_conv_w': ffn_conv_w, 'ffn_conv_b': ffn_conv_b,
        'ffn_w_down': ffn_w_down,
    }


def reference(x, meta_tokens, norm_w, w_in_even, lb_logits, hgrn_gn_w, attn_sinks, w_out_even,
              w_in_odd, lru_conv_w, lru_conv_b, lru_wa, lru_ba, lru_wx, lru_bx, lru_a_param, w_out_odd,
              ffn_w_gu, ffn_conv_w, ffn_conv_b, ffn_w_down):
    bsz = x.shape[0]
    meta = jnp.broadcast_to(meta_tokens.astype(x.dtype)[None], (bsz, N_META, D_MODEL))
    h = jnp.concatenate([meta, x], axis=1)
    lower_bounds = jnp.cumsum(jax.nn.softmax(lb_logits.astype(jnp.float32), axis=0), axis=0)
    for layer in range(DEPTH):
        j = layer // 2
        xn = rmsnorm(h, norm_w[layer, 0])
        if layer % 2 == 0:
            mix = even_mixer(xn, w_in_even[j], lower_bounds[j], hgrn_gn_w[j], attn_sinks[j], w_out_even[j])
        else:
            mix = odd_mixer(xn, w_in_odd[j], lru_conv_w[j], lru_conv_b[j], lru_wa[j], lru_ba[j],
                            lru_wx[j], lru_bx[j], lru_a_param[j], w_out_odd[j])
        h = h + rmsnorm(mix, norm_w[layer, 1])
        ff = conv_ffn(rmsnorm(h, norm_w[layer, 2]), ffn_w_gu[layer], ffn_conv_w[layer],
                      ffn_conv_b[layer], ffn_w_down[layer])
        h = h + rmsnorm(ff, norm_w[layer, 3])
    return h[:, N_META:]
```

```python
import functools

import jax
import jax.numpy as jnp
from jax import lax
from jax.experimental import pallas as pl
from jax.experimental.pallas import tpu as pltpu

F32 = jnp.float32
BF16 = jnp.bfloat16

D_MODEL = 4096
SEQ = 8192
DEPTH = 2
N_META = 16
A_HEADS = 16
A_KDIM = 128
A_VDIM = D_MODEL // 2 // A_HEADS
A_FDIM = A_HEADS * A_KDIM
A_WIDTH = A_HEADS * A_VDIM
HGRN_CHUNK = 64
HGRN_SUB = 16
B_HDIM = 64
B_QHEADS = D_MODEL // 2 // B_HDIM
B_KVHEADS = B_QHEADS // 8
B_GROUP = B_QHEADS // B_KVHEADS
B_WIDTH = B_QHEADS * B_HDIM
B_KVWIDTH = B_KVHEADS * B_HDIM
WINDOW = 128
ATTN_BLOCK = 128
EVEN_IN = 2 * A_FDIM + 2 * A_WIDTH + B_WIDTH + 2 * B_KVWIDTH
LRU_WIDTH = D_MODEL
LRU_BLOCKS = 16
LRU_BDIM = LRU_WIDTH // LRU_BLOCKS
LRU_CONV = 4
LRU_C = 8.0
D_FF = 256 * ((8 * D_MODEL // 3 + 255) // 256)
FFN_CONV = 3
NORM_EPS = 1e-6
NEG_INF = -1e30

PAD = ATTN_BLOCK - N_META
ROW0 = PAD + N_META
LP = ROW0 + SEQ
assert PAD % HGRN_CHUNK == HGRN_CHUNK - N_META and ROW0 == ATTN_BLOCK

V7X_LANES = 128
V7X_SUBLANES = 8
V7X_VMEM_LIMIT_CAP = 60 * 1024 * 1024
CARRY_ROWS = V7X_SUBLANES

TM_MM = 1040
TN_MM = 512
TM_DOWN = 520
TF_FFN = 256
TN_ODD = 256
TR_NORM = 320
TR_LRU = 208
CW_LRU = 1024
TR_HGRN = 640
SCAN_ROWS = 16


def _vmem_limit(nbytes):
    return int(min(V7X_VMEM_LIMIT_CAP, nbytes * 1.15 + (4 << 20)))


def _params(sem, nbytes):
    return pltpu.CompilerParams(dimension_semantics=sem, vmem_limit_bytes=_vmem_limit(nbytes))


def _rms(x, w):
    return x * lax.rsqrt(jnp.mean(x * x, axis=-1, keepdims=True) + NORM_EPS) * w


def _iota(shape, dim):
    return lax.broadcasted_iota(jnp.int32, shape, dim)


def _expm1(x):
    u = jnp.exp(x)
    um1 = u - 1.0
    return jnp.where(um1 == 0.0, x, jnp.where(um1 == -1.0, -1.0, um1 * x / jnp.log(u)))


def _dot(a, b):
    return jnp.dot(a, b, preferred_element_type=F32)


def _dot_nt(a, b):
    return lax.dot_general(a, b, (((1,), (1,)), ((), ())), preferred_element_type=F32)


def _dot_tn(a, b):
    return lax.dot_general(a, b, (((0,), (0,)), ((), ())), preferred_element_type=F32)


def _rms_cast_kernel(h_ref, w_ref, o_ref):
    o_ref[...] = _rms(h_ref[...], w_ref[...]).astype(o_ref.dtype)


def rms_cast(h, w):
    n, d = h.shape
    tr = TR_NORM
    return pl.pallas_call(
        _rms_cast_kernel,
        out_shape=jax.ShapeDtypeStruct((n, d), BF16),
        grid=(n // tr,),
        in_specs=[pl.BlockSpec((tr, d), lambda i: (i, 0)),
                  pl.BlockSpec((1, d), lambda i: (0, 0))],
        out_specs=pl.BlockSpec((tr, d), lambda i: (i, 0)),
        compiler_params=_params(("arbitrary",), 2 * tr * d * 6),
        name="rms_cast",
    )(h, w.reshape(1, d))


def _resid_norm_kernel(h_ref, y_ref, wp_ref, wn_ref, ho_ref, xo_ref):
    h = h_ref[...] + _rms(y_ref[...], wp_ref[...])
    ho_ref[...] = h
    xo_ref[...] = _rms(h, wn_ref[...]).astype(xo_ref.dtype)


def resid_norm(h, y, w_post, w_next):
    n, d = h.shape
    tr = TR_NORM
    row = pl.BlockSpec((tr, d), lambda i: (i, 0))
    vec = pl.BlockSpec((1, d), lambda i: (0, 0))
    return pl.pallas_call(
        _resid_norm_kernel,
        out_shape=(jax.ShapeDtypeStruct((n, d), F32), jax.ShapeDtypeStruct((n, d), BF16)),
        grid=(n // tr,),
        in_specs=[row, row, vec, vec],
        out_specs=(row, row),
        compiler_params=_params(("arbitrary",), 2 * tr * d * 14),
        name="resid_norm",
    )(h, y, w_post.reshape(1, d), w_next.reshape(1, d))


def _resid_final_kernel(h_ref, y_ref, wp_ref, o_ref):
    o_ref[...] = h_ref[...] + _rms(y_ref[...], wp_ref[...])


def resid_final(h, y, w_post):
    n, d = h.shape
    tr = ATTN_BLOCK
    off = ROW0 // tr
    src = pl.BlockSpec((tr, d), lambda i: (i + off, 0))
    return pl.pallas_call(
        _resid_final_kernel,
        out_shape=jax.ShapeDtypeStruct((n - ROW0, d), F32),
        grid=((n - ROW0) // tr,),
        in_specs=[src, src, pl.BlockSpec((1, d), lambda i: (0, 0))],
        out_specs=pl.BlockSpec((tr, d), lambda i: (i, 0)),
        compiler_params=_params(("arbitrary",), 2 * tr * d * 12),
        name="resid_final",
    )(h, y, w_post.reshape(1, d))


def _mm_cast_kernel(x_ref, w_ref, o_ref, wb_ref):
    @pl.when(pl.program_id(1) == 0)
    def _():
        wb_ref[...] = w_ref[...].astype(BF16)

    o_ref[...] = _dot(x_ref[...], wb_ref[...]).astype(o_ref.dtype)


def _mm_kernel(x_ref, w_ref, o_ref):
    o_ref[...] = _dot(x_ref[...], w_ref[...]).astype(o_ref.dtype)


def matmul(x, w, *, tm, tn, out_dtype=F32, name="matmul"):
    m, k = x.shape
    _, n = w.shape
    cast = w.dtype != BF16
    wbytes = k * tn * (4 if cast else 2)
    nbytes = 2 * tm * k * 2 + 2 * wbytes + (k * tn * 2 if cast else 0) + 2 * tm * tn * 4
    return pl.pallas_call(
        _mm_cast_kernel if cast else _mm_kernel,
        out_shape=jax.ShapeDtypeStruct((m, n), out_dtype),
        grid=(n // tn, m // tm),
        in_specs=[pl.BlockSpec((tm, k), lambda j, i: (i, 0)),
                  pl.BlockSpec((k, tn), lambda j, i: (0, j))],
        out_specs=pl.BlockSpec((tm, tn), lambda j, i: (i, j)),
        scratch_shapes=[pltpu.VMEM((k, tn), BF16)] if cast else [],
        compiler_params=_params(("arbitrary", "arbitrary"), nbytes),
        name=name,
    )(x, w)


def _ffn_up_kernel(x_ref, wg_ref, wu_ref, cw_ref, cb_ref, o_ref, wgb_ref, wub_ref, gbuf_ref, *, tm):
    @pl.when(pl.program_id(1) == 0)
    def _():
        wgb_ref[...] = wg_ref[...].astype(BF16)
        wub_ref[...] = wu_ref[...].astype(BF16)
        gbuf_ref[0:CARRY_ROWS, :] = jnp.zeros((CARRY_ROWS, gbuf_ref.shape[1]), F32)

    x = x_ref[...]
    gate = _dot(x, wgb_ref[...])
    up = _dot(x, wub_ref[...])
    gbuf_ref[CARRY_ROWS:CARRY_ROWS + tm, :] = gate
    conv = cb_ref[...]
    for tap in range(FFN_CONV - 1):
        start = CARRY_ROWS - (FFN_CONV - 1) + tap
        conv = conv + gbuf_ref[start:start + tm, :] * cw_ref[tap:tap + 1, :]
    conv = conv + gate * cw_ref[FFN_CONV - 1:FFN_CONV, :]
    o_ref[...] = (jax.nn.gelu(conv, approximate=True) * up).astype(o_ref.dtype)
    gbuf_ref[0:CARRY_ROWS, :] = gbuf_ref[tm:tm + CARRY_ROWS, :]


def ffn_up(xn, w_gu, conv_w, conv_b):
    m, k = xn.shape
    tm, tf = TM_MM, TF_FFN
    nj = D_FF // tf
    nbytes = 2 * tm * k * 2 + 4 * k * tf * 4 + 2 * k * tf * 2 + 2 * tm * tf * 2 + (tm + 8) * tf * 4
    return pl.pallas_call(
        functools.partial(_ffn_up_kernel, tm=tm),
        out_shape=jax.ShapeDtypeStruct((m, D_FF), BF16),
        grid=(nj, m // tm),
        in_specs=[pl.BlockSpec((tm, k), lambda j, i: (i, 0)),
                  pl.BlockSpec((k, tf), lambda j, i: (0, j)),
                  pl.BlockSpec((k, tf), lambda j, i: (0, j + nj)),
                  pl.BlockSpec((FFN_CONV, tf), lambda j, i: (0, j)),
                  pl.BlockSpec((1, tf), lambda j, i: (0, j))],
        out_specs=pl.BlockSpec((tm, tf), lambda j, i: (i, j)),
        scratch_shapes=[pltpu.VMEM((k, tf), BF16), pltpu.VMEM((k, tf), BF16),
                        pltpu.VMEM((tm + CARRY_ROWS, tf), F32)],
        compiler_params=_params(("arbitrary", "arbitrary"), nbytes),
        name="ffn_up",
    )(xn, w_gu, w_gu, conv_w, conv_b.reshape(1, D_FF))


def _in_odd_kernel(x_ref, wy_ref, wx_ref, cw_ref, cb_ref, y_ref, xc_ref, wyb_ref, wxb_ref, xbuf_ref, *, tm):
    @pl.when(pl.program_id(1) == 0)
    def _():
        wyb_ref[...] = wy_ref[...].astype(BF16)
        wxb_ref[...] = wx_ref[...].astype(BF16)
        xbuf_ref[0:CARRY_ROWS, :] = jnp.zeros((CARRY_ROWS, xbuf_ref.shape[1]), F32)

    x = x_ref[...]
    y_ref[...] = jax.nn.gelu(_dot(x, wyb_ref[...]), approximate=True)
    xb = _dot(x, wxb_ref[...])
    xbuf_ref[CARRY_ROWS:CARRY_ROWS + tm, :] = xb
    conv = cb_ref[...]
    for tap in range(LRU_CONV - 1):
        start = CARRY_ROWS - (LRU_CONV - 1) + tap
        conv = conv + xbuf_ref[start:start + tm, :] * cw_ref[tap:tap + 1, :]
    conv = conv + xb * cw_ref[LRU_CONV - 1:LRU_CONV, :]
    xc_ref[...] = conv
    xbuf_ref[0:CARRY_ROWS, :] = xbuf_ref[tm:tm + CARRY_ROWS, :]


def in_odd(xn, w_in, conv_w, conv_b):
    m, k = xn.shape
    tm, tn = TM_MM, TN_ODD
    nj = LRU_WIDTH // tn
    nbytes = 2 * tm * k * 2 + 4 * k * tn * 4 + 2 * k * tn * 2 + 4 * tm * tn * 4 + (tm + 8) * tn * 4
    out = pl.BlockSpec((tm, tn), lambda j, i: (i, j))
    return pl.pallas_call(
        functools.partial(_in_odd_kernel, tm=tm),
        out_shape=(jax.ShapeDtypeStruct((m, LRU_WIDTH), F32), jax.ShapeDtypeStruct((m, LRU_WIDTH), F32)),
        grid=(nj, m // tm),
        in_specs=[pl.BlockSpec((tm, k), lambda j, i: (i, 0)),
                  pl.BlockSpec((k, tn), lambda j, i: (0, j)),
                  pl.BlockSpec((k, tn), lambda j, i: (0, j + nj)),
                  pl.BlockSpec((LRU_CONV, tn), lambda j, i: (0, j)),
                  pl.BlockSpec((1, tn), lambda j, i: (0, j))],
        out_specs=(out, out),
        scratch_shapes=[pltpu.VMEM((k, tn), BF16), pltpu.VMEM((k, tn), BF16),
                        pltpu.VMEM((tm + CARRY_ROWS, tn), F32)],
        compiler_params=_params(("arbitrary", "arbitrary"), nbytes),
        name="in_odd",
    )(xn, w_in, w_in, conv_w, conv_b.reshape(1, LRU_WIDTH))


def _lru_kernel(xc_ref, y_ref, wa_ref, wx_ref, ba_ref, bx_ref, ap_ref, o_ref, a_ref, b_ref, hc_ref, *, tr):
    step = pl.program_id(0)

    @pl.when(step == 0)
    def _():
        hc_ref[...] = jnp.zeros_like(hc_ref)

    valid = (step * tr + _iota((tr, 1), 0)) >= PAD
    for blk in range(LRU_BLOCKS):
        sl = slice(blk * LRU_BDIM, (blk + 1) * LRU_BDIM)
        x = xc_ref[:, sl]
        xb = x.astype(BF16)
        gate_r = jax.nn.sigmoid(_dot(xb, wa_ref[blk]) + ba_ref[:, sl])
        gate_i = jax.nn.sigmoid(_dot(xb, wx_ref[blk]) + bx_ref[:, sl])
        log_a = LRU_C * gate_r * jax.nn.log_sigmoid(ap_ref[:, sl])
        a_ref[:, sl] = jnp.exp(log_a)
        inp = jnp.sqrt(-_expm1(2.0 * log_a)) * (gate_i * x)
        b_ref[:, sl] = jnp.where(valid, inp, 0.0)

    ridx = _iota((SCAN_ROWS, CW_LRU), 0)
    for c in range(LRU_WIDTH // CW_LRU):
        cs = slice(c * CW_LRU, (c + 1) * CW_LRU)

        def body(g, carry, cs=cs):
            r0 = pl.multiple_of(g * SCAN_ROWS, SCAN_ROWS)
            a = a_ref[pl.ds(r0, SCAN_ROWS), cs]
            b = b_ref[pl.ds(r0, SCAN_ROWS), cs]
            shift = 1
            while shift < SCAN_ROWS:
                ok = ridx >= shift
                b = jnp.where(ok, a * pltpu.roll(b, shift, 0) + b, b)
                a = jnp.where(ok, a * pltpu.roll(a, shift, 0), a)
                shift *= 2
            h = a * carry + b
            o_ref[pl.ds(r0, SCAN_ROWS), cs] = (h * y_ref[pl.ds(r0, SCAN_ROWS), cs]).astype(o_ref.dtype)
            return h[SCAN_ROWS - 1:SCAN_ROWS, :]

        hc_ref[0:1, cs] = lax.fori_loop(0, tr // SCAN_ROWS, body, hc_ref[0:1, cs])


def rglru(xc, y, w_a, b_a, w_x, b_x, a_param):
    m, d = xc.shape
    tr = TR_LRU
    row = pl.BlockSpec((tr, d), lambda i: (i, 0))
    vec = pl.BlockSpec((1, d), lambda i: (0, 0))
    wspec = pl.BlockSpec((LRU_BLOCKS, LRU_BDIM, LRU_BDIM), lambda i: (0, 0, 0))
    nbytes = 4 * tr * d * 4 + 2 * tr * d * 2 + 2 * tr * d * 4 + 4 * LRU_BLOCKS * LRU_BDIM * LRU_BDIM * 2
    return pl.pallas_call(
        functools.partial(_lru_kernel, tr=tr),
        out_shape=jax.ShapeDtypeStruct((m, d), BF16),
        grid=(m // tr,),
        in_specs=[row, row, wspec, wspec, vec, vec, vec],
        out_specs=row,
        scratch_shapes=[pltpu.VMEM((tr, d), F32), pltpu.VMEM((tr, d), F32),
                        pltpu.VMEM((V7X_SUBLANES, d), F32)],
        compiler_params=_params(("arbitrary",), nbytes),
        name="rglru",
    )(xc, y, w_a.astype(BF16), w_x.astype(BF16), b_a.reshape(1, d), b_x.reshape(1, d),
      a_param.reshape(1, d))


def _hgrn_kernel(q_ref, f_ref, i_ref, g_ref, lbl_ref, gnw_ref, o_ref, st_ref, *, tr, layer_j):
    step = pl.program_id(1)

    @pl.when(step == 0)
    def _():
        st_ref[...] = jnp.zeros_like(st_ref)

    logits = lbl_ref[...]
    e = jnp.exp(logits - jnp.max(logits, axis=0, keepdims=True))
    lb = jnp.sum(e[0:layer_j + 1], axis=0, keepdims=True) / jnp.sum(e, axis=0, keepdims=True)

    valid = (step * tr + _iota((tr, 1), 0)) >= PAD
    q = jax.nn.silu(q_ref[...])
    forget = lb + (1.0 - lb) * jax.nn.sigmoid(f_ref[...])
    k = jnp.where(valid, 1.0 - forget, 0.0)
    g = jnp.where(valid, jnp.log(forget), 0.0)
    v = i_ref[...]

    pos = _iota((tr, A_KDIM), 0) & (HGRN_CHUNK - 1)
    b = g
    shift = 1
    while shift < HGRN_CHUNK:
        b = b + jnp.where(pos >= shift, pltpu.roll(b, shift, 0), 0.0)
        shift *= 2

    n_sub = HGRN_CHUNK // HGRN_SUB
    cpos = _iota((HGRN_CHUNK, A_KDIM), 0)
    causal = _iota((HGRN_CHUNK, HGRN_CHUNK), 0) >= _iota((HGRN_CHUNK, HGRN_CHUNK), 1)
    gnw = gnw_ref[...]
    st = st_ref[...]
    for c in range(tr // HGRN_CHUNK):
        sl = slice(c * HGRN_CHUNK, (c + 1) * HGRN_CHUNK)
        bc, qc, kc, vc = b[sl], q[sl], k[sl], v[sl]
        vcb = vc.astype(BF16)
        b_last = bc[HGRN_CHUNK - 1:HGRN_CHUNK]
        o_inter = _dot_nt((qc * jnp.exp(bc)).astype(BF16), st.astype(BF16))
        u_t = _dot_tn(vcb, (kc * jnp.exp(b_last - bc)).astype(BF16))
        att_rows = []
        for i in range(n_sub):
            ss = slice(i * HGRN_SUB, (i + 1) * HGRN_SUB)
            ref = jnp.zeros((1, A_KDIM), F32) if i == 0 else bc[i * HGRN_SUB - 1:i * HGRN_SUB]
            q_sc = (qc[ss] * jnp.exp(bc[ss] - ref)).astype(BF16)
            expo = jnp.where(cpos < (i + 1) * HGRN_SUB, ref - bc, 0.0)
            k_sc = (kc * jnp.exp(expo)).astype(BF16)
            att_rows.append(_dot_nt(q_sc, k_sc))
        att = jnp.where(causal, jnp.concatenate(att_rows, axis=0), 0.0)
        o = o_inter + _dot(att.astype(BF16), vcb)
        gate = jax.nn.silu(g_ref[sl, :])
        o_ref[sl, :] = (_rms(o, gnw) * gate).astype(o_ref.dtype)
        st = st * jnp.exp(b_last) + u_t
    st_ref[...] = st


def hgrn2(hproj, lb_logits, gn_w, layer_j):
    m = hproj.shape[0]
    tr = TR_HGRN
    nrow = lb_logits.shape[0]

    def col(off):
        return pl.BlockSpec((tr, A_KDIM), lambda h, t, off=off: (t, h + off))

    nbytes = 2 * 4 * tr * A_KDIM * 4 + 2 * tr * A_VDIM * 2 + 12 * tr * A_KDIM * 4
    return pl.pallas_call(
        functools.partial(_hgrn_kernel, tr=tr, layer_j=layer_j),
        out_shape=jax.ShapeDtypeStruct((m, A_WIDTH), BF16),
        grid=(A_HEADS, m // tr),
        in_specs=[col(0), col(A_HEADS), col(2 * A_HEADS), col(3 * A_HEADS),
                  pl.BlockSpec((nrow, A_KDIM), lambda h, t: (0, h)),
                  pl.BlockSpec((1, A_VDIM), lambda h, t: (0, 0))],
        out_specs=pl.BlockSpec((tr, A_VDIM), lambda h, t: (t, h)),
        scratch_shapes=[pltpu.VMEM((A_VDIM, A_KDIM), F32)],
        compiler_params=_params(("arbitrary", "arbitrary"), nbytes),
        name="hgrn2",
    )(hproj, hproj, hproj, hproj, lb_logits, gn_w.reshape(1, A_VDIM))


PAIRS = B_GROUP // 2
QROWS = PAIRS * ATTN_BLOCK


def _swa_kernel(sink_ref, q_ref, kc_ref, kp_ref, km_ref, vc_ref, vp_ref, vm_ref, o_ref):
    n = pl.program_id(0)
    scale = B_HDIM ** -0.5
    lane_lo = _iota((1, V7X_LANES), 1) < B_HDIM
    tq = _iota((QROWS, ATTN_BLOCK), 0) & (ATTN_BLOCK - 1)
    sk = _iota((QROWS, ATTN_BLOCK), 1)
    cur_ok = sk <= tq + jnp.where(n >= 1, 0, -2 * ATTN_BLOCK)
    prev_ok = sk > tq + jnp.where(n >= 2, 0, 2 * ATTN_BLOCK)
    tq_m = _iota((QROWS, N_META), 0) & (ATTN_BLOCK - 1)
    meta_ok = tq_m - PAD + jnp.where(n >= 1, ATTN_BLOCK, 0) >= _iota((QROWS, N_META), 1)
    pair_of_row = _iota((QROWS, 1), 0) >> (ATTN_BLOCK.bit_length() - 1)

    def split(x, natural_lo):
        rolled = pltpu.roll(x, B_HDIM, 1)
        if natural_lo:
            lo, hi = jnp.where(lane_lo, x, 0.0), jnp.where(lane_lo, 0.0, rolled)
        else:
            lo, hi = jnp.where(lane_lo, rolled, 0.0), jnp.where(lane_lo, 0.0, x)
        return lo.astype(BF16), hi.astype(BF16)

    for h in range(B_KVHEADS):
        tile = slice((h // 2) * V7X_LANES, (h // 2 + 1) * V7X_LANES)
        nat = h % 2 == 0
        kc, kp, km = split(kc_ref[:, tile], nat), split(kp_ref[:, tile], nat), split(km_ref[PAD:, tile], nat)
        vc, vp, vm = split(vc_ref[:, tile], nat), split(vp_ref[:, tile], nat), split(vm_ref[PAD:, tile], nat)
        qs = jnp.concatenate(
            [q_ref[:, (h * PAIRS + p) * V7X_LANES:(h * PAIRS + p + 1) * V7X_LANES] for p in range(PAIRS)],
            axis=0).astype(BF16)
        acc = jnp.zeros((QROWS, V7X_LANES), F32)
        for par in range(2):
            sink = jnp.zeros((QROWS, 1), F32)
            for p in range(PAIRS):
                sink = jnp.where(pair_of_row == p, sink_ref[h * B_GROUP + 2 * p + par], sink)
            l_c = jnp.where(cur_ok, _dot_nt(qs, kc[par]) * scale, NEG_INF)
            l_p = jnp.where(prev_ok, _dot_nt(qs, kp[par]) * scale, NEG_INF)
            l_m = jnp.where(meta_ok, _dot_nt(qs, km[par]) * scale, NEG_INF)
            mx = jnp.maximum(jnp.maximum(jnp.max(l_c, axis=-1, keepdims=True),
                                         jnp.max(l_p, axis=-1, keepdims=True)),
                             jnp.maximum(jnp.max(l_m, axis=-1, keepdims=True), sink))
            e_c, e_p, e_m = jnp.exp(l_c - mx), jnp.exp(l_p - mx), jnp.exp(l_m - mx)
            den = (jnp.sum(e_c, axis=-1, keepdims=True) + jnp.sum(e_p, axis=-1, keepdims=True)
                   + jnp.sum(e_m, axis=-1, keepdims=True) + jnp.exp(sink - mx))
            pv = (_dot(e_c.astype(BF16), vc[par]) + _dot(e_p.astype(BF16), vp[par])
                  + _dot(e_m.astype(BF16), vm[par]))
            acc = acc + pv / den
        for p in range(PAIRS):
            o_ref[:, (h * PAIRS + p) * V7X_LANES:(h * PAIRS + p + 1) * V7X_LANES] = (
                acc[p * ATTN_BLOCK:(p + 1) * ATTN_BLOCK].astype(o_ref.dtype))


def swa(hproj, sinks):
    m = hproj.shape[0]
    blk = ATTN_BLOCK
    q_col = (2 * A_FDIM + 2 * A_WIDTH) // B_WIDTH
    k_col = (2 * A_FDIM + 2 * A_WIDTH + B_WIDTH) // B_KVWIDTH
    v_col = k_col + 1
    qspec = pl.BlockSpec((blk, B_WIDTH), lambda n: (n, q_col))

    def kv(col, which):
        if which == "cur":
            return pl.BlockSpec((blk, B_KVWIDTH), lambda n: (n, col))
        if which == "prev":
            return pl.BlockSpec((blk, B_KVWIDTH), lambda n: (jnp.maximum(n - 1, 0), col))
        return pl.BlockSpec((blk, B_KVWIDTH), lambda n: (0, col))

    nbytes = 2 * blk * B_WIDTH * 4 + 12 * blk * B_KVWIDTH * 4 + 2 * blk * B_WIDTH * 2 + 40 * QROWS * 128 * 4
    return pl.pallas_call(
        _swa_kernel,
        out_shape=jax.ShapeDtypeStruct((m, B_WIDTH), BF16),
        grid=(m // blk,),
        in_specs=[pl.BlockSpec(memory_space=pltpu.SMEM), qspec,
                  kv(k_col, "cur"), kv(k_col, "prev"), kv(k_col, "meta"),
                  kv(v_col, "cur"), kv(v_col, "prev"), kv(v_col, "meta")],
        out_specs=pl.BlockSpec((blk, B_WIDTH), lambda n: (n, 0)),
        compiler_params=_params(("arbitrary",), nbytes),
        name="swa",
    )(sinks, hproj, hproj, hproj, hproj, hproj, hproj, hproj)


def kernel(x, meta_tokens, norm_w, w_in_even, lb_logits, hgrn_gn_w, attn_sinks, w_out_even,
           w_in_odd, lru_conv_w, lru_conv_b, lru_wa, lru_ba, lru_wx, lru_bx, lru_a_param, w_out_odd,
           ffn_w_gu, ffn_conv_w, ffn_conv_b, ffn_w_down):
    assert x.shape == (1, SEQ, D_MODEL) and norm_w.shape[0] == DEPTH
    h = jnp.concatenate([jnp.zeros((PAD, D_MODEL), F32), meta_tokens.astype(F32), x[0].astype(F32)], axis=0)
    xn = rms_cast(h, norm_w[0, 0])
    out = None
    for layer in range(DEPTH):
        j = layer // 2
        if layer % 2 == 0:
            hproj = matmul(xn, w_in_even[j], tm=TM_MM, tn=TN_MM, name="in_even")
            o_a = hgrn2(hproj, lb_logits, hgrn_gn_w[j], j)
            o_b = swa(hproj, attn_sinks[j])
            mix = matmul(jnp.concatenate([o_a, o_b], axis=1), w_out_even[j], tm=TM_MM, tn=TN_MM,
                         name="out_even")
        else:
            y_br, x_br = in_odd(xn, w_in_odd[j], lru_conv_w[j], lru_conv_b[j])
            rec = rglru(x_br, y_br, lru_wa[j], lru_ba[j], lru_wx[j], lru_bx[j], lru_a_param[j])
            mix = matmul(rec, w_out_odd[j], tm=TM_MM, tn=TN_MM, name="out_odd")
        h, xn = resid_norm(h, mix, norm_w[layer, 1], norm_w[layer, 2])
        act = ffn_up(xn, ffn_w_gu[layer], ffn_conv_w[layer], ffn_conv_b[layer])
        ff = matmul(act, ffn_w_down[layer].astype(BF16), tm=TM_DOWN, tn=TN_MM, name="ffn_down")
        if layer + 1 < DEPTH:
            h, xn = resid_norm(h, ff, norm_w[layer, 3], norm_w[layer + 1, 0])
        else:
            out = resid_final(h, ff, norm_w[layer, 3])
    return out[None]
```

```python
import functools

import jax
import jax.numpy as jnp
from jax import lax
from jax.experimental import pallas as pl
from jax.experimental.pallas import tpu as pltpu

F32 = jnp.float32
BF16 = jnp.bfloat16

D_MODEL = 4096
SEQ = 8192
DEPTH = 2
N_META = 16
A_HEADS = 16
A_KDIM = 128
A_VDIM = D_MODEL // 2 // A_HEADS
A_FDIM = A_HEADS * A_KDIM
A_WIDTH = A_HEADS * A_VDIM
HGRN_CHUNK = 64
HGRN_SUB = 16
B_HDIM = 64
B_QHEADS = D_MODEL // 2 // B_HDIM
B_KVHEADS = B_QHEADS // 8
B_GROUP = B_QHEADS // B_KVHEADS
B_WIDTH = B_QHEADS * B_HDIM
B_KVWIDTH = B_KVHEADS * B_HDIM
WINDOW = 128
ATTN_BLOCK = 128
EVEN_IN = 2 * A_FDIM + 2 * A_WIDTH + B_WIDTH + 2 * B_KVWIDTH
LRU_WIDTH = D_MODEL
LRU_BLOCKS = 16
LRU_BDIM = LRU_WIDTH // LRU_BLOCKS
LRU_CONV = 4
LRU_C = 8.0
D_FF = 256 * ((8 * D_MODEL // 3 + 255) // 256)
FFN_CONV = 3
NORM_EPS = 1e-6
NEG_INF = -1e30

PAD = ATTN_BLOCK - N_META
ROW0 = PAD + N_META
LP = ROW0 + SEQ
assert PAD % HGRN_CHUNK == HGRN_CHUNK - N_META and ROW0 == ATTN_BLOCK

V7X_LANES = 128
V7X_SUBLANES = 8
V7X_VMEM_LIMIT_CAP = 60 * 1024 * 1024
CARRY_ROWS = V7X_SUBLANES

TM_MM = 1040
TN_MM = 512
TM_DOWN = 520
TF_FFN = 256
TN_ODD = 256
EPI_ROWS = 80
MXU_GROUPS = 7
TR_NORM = 320
TR_LRU = 208
CW_LRU = 1024
TR_HGRN = 640
SCAN_ROWS = 16


def _vmem_limit(nbytes):
    return int(min(V7X_VMEM_LIMIT_CAP, nbytes * 1.15 + (4 << 20)))


def _params(sem, nbytes):
    return pltpu.CompilerParams(dimension_semantics=sem, vmem_limit_bytes=_vmem_limit(nbytes))


def _rms(x, w):
    return x * lax.rsqrt(jnp.mean(x * x, axis=-1, keepdims=True) + NORM_EPS) * w


def _iota(shape, dim):
    return lax.broadcasted_iota(jnp.int32, shape, dim)


def _expm1(x):
    u = jnp.exp(x)
    um1 = u - 1.0
    return jnp.where(um1 == 0.0, x, jnp.where(um1 == -1.0, -1.0, um1 * x / jnp.log(u)))


def _dot(a, b):
    return jnp.dot(a, b, preferred_element_type=F32)


def _dot_nt(a, b):
    return lax.dot_general(a, b, (((1,), (1,)), ((), ())), preferred_element_type=F32)


def _dot_tn(a, b):
    return lax.dot_general(a, b, (((0,), (0,)), ((), ())), preferred_element_type=F32)


def _stream_rows(n, x_ref, meta_ref):
    first = jnp.concatenate([jnp.zeros((PAD, D_MODEL), F32), meta_ref[...].astype(F32)], axis=0)
    return jnp.where(n == 0, first, x_ref[...].astype(F32))


def _stream_specs():
    blk = ATTN_BLOCK
    return [pl.BlockSpec((blk, D_MODEL), lambda n: (jnp.maximum(n - 1, 0), 0)),
            pl.BlockSpec((N_META, D_MODEL), lambda n: (0, 0))]


def _rms_cast_stream_kernel(x_ref, meta_ref, w_ref, o_ref):
    h = _stream_rows(pl.program_id(0), x_ref, meta_ref)
    o_ref[...] = _rms(h, w_ref[...]).astype(o_ref.dtype)


def rms_cast_stream(x2d, meta, w):
    blk, d = ATTN_BLOCK, D_MODEL
    return pl.pallas_call(
        _rms_cast_stream_kernel,
        out_shape=jax.ShapeDtypeStruct((LP, d), BF16),
        grid=(LP // blk,),
        in_specs=_stream_specs() + [pl.BlockSpec((1, d), lambda n: (0, 0))],
        out_specs=pl.BlockSpec((blk, d), lambda n: (n, 0)),
        compiler_params=_params(("arbitrary",), 2 * blk * d * 6 + 4 * blk * d * 4),
        name="rms_cast_stream",
    )(x2d, meta, w.reshape(1, d))


def _resid_norm_stream_kernel(x_ref, meta_ref, y_ref, wp_ref, wn_ref, ho_ref, xo_ref):
    h = _stream_rows(pl.program_id(0), x_ref, meta_ref) + _rms(y_ref[...], wp_ref[...])
    ho_ref[...] = h
    xo_ref[...] = _rms(h, wn_ref[...]).astype(xo_ref.dtype)


def resid_norm_stream(x2d, meta, y, w_post, w_next):
    blk, d = ATTN_BLOCK, D_MODEL
    row = pl.BlockSpec((blk, d), lambda n: (n, 0))
    vec = pl.BlockSpec((1, d), lambda n: (0, 0))
    return pl.pallas_call(
        _resid_norm_stream_kernel,
        out_shape=(jax.ShapeDtypeStruct((LP, d), F32), jax.ShapeDtypeStruct((LP, d), BF16)),
        grid=(LP // blk,),
        in_specs=_stream_specs() + [row, vec, vec],
        out_specs=(row, row),
        compiler_params=_params(("arbitrary",), 2 * blk * d * 14 + 4 * blk * d * 4),
        name="resid_norm_stream",
    )(x2d, meta, y, w_post.reshape(1, d), w_next.reshape(1, d))


def _resid_norm_kernel(h_ref, y_ref, wp_ref, wn_ref, ho_ref, xo_ref):
    h = h_ref[...] + _rms(y_ref[...], wp_ref[...])
    ho_ref[...] = h
    xo_ref[...] = _rms(h, wn_ref[...]).astype(xo_ref.dtype)


def resid_norm(h, y, w_post, w_next):
    n, d = h.shape
    tr = TR_NORM
    row = pl.BlockSpec((tr, d), lambda i: (i, 0))
    vec = pl.BlockSpec((1, d), lambda i: (0, 0))
    return pl.pallas_call(
        _resid_norm_kernel,
        out_shape=(jax.ShapeDtypeStruct((n, d), F32), jax.ShapeDtypeStruct((n, d), BF16)),
        grid=(n // tr,),
        in_specs=[row, row, vec, vec],
        out_specs=(row, row),
        compiler_params=_params(("arbitrary",), 2 * tr * d * 14),
        name="resid_norm",
    )(h, y, w_post.reshape(1, d), w_next.reshape(1, d))


def _resid_final_kernel(h_ref, y_ref, wp_ref, o_ref):
    o_ref[...] = h_ref[...] + _rms(y_ref[...], wp_ref[...])


def resid_final(h, y, w_post):
    n, d = h.shape
    tr = ATTN_BLOCK
    off = ROW0 // tr
    src = pl.BlockSpec((tr, d), lambda i: (i + off, 0))
    return pl.pallas_call(
        _resid_final_kernel,
        out_shape=jax.ShapeDtypeStruct((n - ROW0, d), F32),
        grid=((n - ROW0) // tr,),
        in_specs=[src, src, pl.BlockSpec((1, d), lambda i: (0, 0))],
        out_specs=pl.BlockSpec((tr, d), lambda i: (i, 0)),
        compiler_params=_params(("arbitrary",), 2 * tr * d * 12),
        name="resid_final",
    )(h, y, w_post.reshape(1, d))


def _weight_spec(w, layer, k, tn, col_of):
    if w.ndim == 3:
        return pl.BlockSpec((None, k, tn), lambda *g: (layer, 0, col_of(*g)))
    return pl.BlockSpec((k, tn), lambda *g: (0, col_of(*g)))


def _mm_kernel(*refs, n_lhs, cast):
    x_refs, w_ref, o_ref = refs[:n_lhs], refs[n_lhs], refs[n_lhs + 1]
    if cast:
        wb_ref = refs[n_lhs + 2]

        @pl.when(pl.program_id(1) == 0)
        def _():
            wb_ref[...] = w_ref[...].astype(BF16)

        w_ref = wb_ref
    acc, k0 = None, 0
    for x_ref in x_refs:
        kk = x_ref.shape[1]
        part = _dot(x_ref[...], w_ref[k0:k0 + kk, :])
        acc = part if acc is None else acc + part
        k0 += kk
    o_ref[...] = acc.astype(o_ref.dtype)


def matmul(xs, w, *, layer=0, tm, tn, out_dtype=F32, name="matmul"):
    m = xs[0].shape[0]
    k, n = w.shape[-2:]
    assert sum(x.shape[1] for x in xs) == k
    cast = w.dtype != BF16
    wbytes = k * tn * (4 if cast else 2)
    nbytes = 2 * tm * k * 2 + 2 * wbytes + (k * tn * 2 if cast else 0) + 2 * tm * tn * 4
    return pl.pallas_call(
        functools.partial(_mm_kernel, n_lhs=len(xs), cast=cast),
        out_shape=jax.ShapeDtypeStruct((m, n), out_dtype),
        grid=(n // tn, m // tm),
        in_specs=[pl.BlockSpec((tm, x.shape[1]), lambda j, i: (i, 0)) for x in xs]
        + [_weight_spec(w, layer, k, tn, lambda j, i: j)],
        out_specs=pl.BlockSpec((tm, tn), lambda j, i: (i, j)),
        scratch_shapes=[pltpu.VMEM((k, tn), BF16)] if cast else [],
        compiler_params=_params(("arbitrary", "arbitrary"), nbytes),
        name=name,
    )(*xs, w)


def _tile_maps(ni, nj):
    nt = ni * nj

    def cur_row(t):
        return lax.rem(jnp.minimum(t, nt - 1), ni)

    def cur_col(t):
        return lax.div(jnp.minimum(t, nt - 1), ni)

    def prev_row(t):
        return lax.rem(jnp.maximum(t - 1, 0), ni)

    def prev_col(t):
        return lax.div(jnp.maximum(t - 1, 0), ni)

    return nt, cur_row, cur_col, prev_row, prev_col


def _row_groups(tm):
    n_chunks = tm // EPI_ROWS
    bounds = [EPI_ROWS * (n_chunks * g // MXU_GROUPS) for g in range(MXU_GROUPS + 1)]
    return [(lo, hi - lo) for lo, hi in zip(bounds[:-1], bounds[1:])]


def _causal_conv(buf_ref, cw_ref, cb_ref, width, r0, rows):
    win = buf_ref[r0:r0 + CARRY_ROWS + rows, :]
    conv = cb_ref[...]
    for tap in range(width):
        back = width - 1 - tap
        shifted = pltpu.roll(win, back, 0) if back else win
        conv = conv + shifted[CARRY_ROWS:, :] * cw_ref[tap:tap + 1, :]
    return conv


def _ffn_up_kernel(x_ref, wg_ref, wu_ref, cw_ref, cb_ref, wd_ref, o_ref, wdb_ref,
                   wgb_ref, wub_ref, g0_ref, g1_ref, u0_ref, u1_ref, *, tm, ni, nt):
    t = pl.program_id(0)
    i = lax.rem(t, ni)

    @pl.when(t == 0)
    def _():
        g1_ref[...] = jnp.zeros_like(g1_ref)
        u1_ref[...] = jnp.zeros_like(u1_ref)

    @pl.when(jnp.logical_and(i == 0, t < nt))
    def _():
        wgb_ref[...] = wg_ref[...].astype(BF16)
        wub_ref[...] = wu_ref[...].astype(BF16)
        wdb_ref[...] = wd_ref[...].astype(BF16)

    def step(g_cur, u_cur, g_prev, u_prev):
        g_cur[0:CARRY_ROWS, :] = jnp.where(i == 0, 0.0, g_prev[tm:tm + CARRY_ROWS, :])
        for m0, mr in _row_groups(tm):
            for r0 in range(m0, m0 + mr, EPI_ROWS):
                conv = _causal_conv(g_prev, cw_ref, cb_ref, FFN_CONV, r0, EPI_ROWS)
                o_ref[r0:r0 + EPI_ROWS, :] = (jax.nn.gelu(conv, approximate=True)
                                              * u_prev[r0:r0 + EPI_ROWS, :]).astype(o_ref.dtype)
            x = x_ref[m0:m0 + mr, :]
            g_cur[CARRY_ROWS + m0:CARRY_ROWS + m0 + mr, :] = _dot(x, wgb_ref[...])
            u_cur[m0:m0 + mr, :] = _dot(x, wub_ref[...])

    @pl.when(lax.rem(t, 2) == 0)
    def _():
        step(g0_ref, u0_ref, g1_ref, u1_ref)

    @pl.when(lax.rem(t, 2) == 1)
    def _():
        step(g1_ref, u1_ref, g0_ref, u0_ref)


def ffn_up(xn, w_gu, conv_w, conv_b, w_down, layer):
    m, k = xn.shape
    tm, tf = TM_MM, TF_FFN
    ni, nj = m // tm, D_FF // tf
    nt, cur_row, cur_col, prev_row, prev_col = _tile_maps(ni, nj)
    nbytes = (2 * tm * k * 2 + 4 * k * tf * 4 + 2 * k * tf * 2 + 2 * tm * tf * 2
              + 4 * (tm + CARRY_ROWS) * tf * 4 + 2 * tf * D_MODEL * 6)
    return pl.pallas_call(
        functools.partial(_ffn_up_kernel, tm=tm, ni=ni, nt=nt),
        out_shape=(jax.ShapeDtypeStruct((m, D_FF), BF16), jax.ShapeDtypeStruct((D_FF, D_MODEL), BF16)),
        grid=(nt + 1,),
        in_specs=[pl.BlockSpec((tm, k), lambda t: (cur_row(t), 0)),
                  pl.BlockSpec((None, k, tf), lambda t: (layer, 0, cur_col(t))),
                  pl.BlockSpec((None, k, tf), lambda t: (layer, 0, cur_col(t) + nj)),
                  pl.BlockSpec((None, FFN_CONV, tf), lambda t: (layer, 0, prev_col(t))),
                  pl.BlockSpec((None, 1, tf), lambda t: (layer, 0, prev_col(t))),
                  pl.BlockSpec((None, tf, D_MODEL), lambda t: (layer, cur_col(t), 0))],
        out_specs=(pl.BlockSpec((tm, tf), lambda t: (prev_row(t), prev_col(t))),
                   pl.BlockSpec((tf, D_MODEL), lambda t: (cur_col(t), 0))),
        scratch_shapes=[pltpu.VMEM((k, tf), BF16), pltpu.VMEM((k, tf), BF16),
                        pltpu.VMEM((tm + CARRY_ROWS, tf), F32), pltpu.VMEM((tm + CARRY_ROWS, tf), F32),
                        pltpu.VMEM((tm, tf), F32), pltpu.VMEM((tm, tf), F32)],
        compiler_params=_params(("arbitrary",), nbytes),
        name="ffn_up",
    )(xn, w_gu, w_gu, conv_w, conv_b.reshape(conv_b.shape[0], 1, D_FF), w_down)


def _in_odd_kernel(x_ref, wy_ref, wx_ref, cw_ref, cb_ref, y_ref, xc_ref,
                   wyb_ref, wxb_ref, b0_ref, b1_ref, r0_ref, r1_ref, *, tm, ni, nt):
    t = pl.program_id(0)
    i = lax.rem(t, ni)

    @pl.when(t == 0)
    def _():
        b1_ref[...] = jnp.zeros_like(b1_ref)
        r1_ref[...] = jnp.zeros_like(r1_ref)

    @pl.when(jnp.logical_and(i == 0, t < nt))
    def _():
        wyb_ref[...] = wy_ref[...].astype(BF16)
        wxb_ref[...] = wx_ref[...].astype(BF16)

    def step(b_cur, r_cur, b_prev, r_prev):
        b_cur[0:CARRY_ROWS, :] = jnp.where(i == 0, 0.0, b_prev[tm:tm + CARRY_ROWS, :])
        for m0, mr in _row_groups(tm):
            for r0 in range(m0, m0 + mr, EPI_ROWS):
                y_ref[r0:r0 + EPI_ROWS, :] = jax.nn.gelu(r_prev[r0:r0 + EPI_ROWS, :], approximate=True)
                xc_ref[r0:r0 + EPI_ROWS, :] = _causal_conv(b_prev, cw_ref, cb_ref, LRU_CONV, r0, EPI_ROWS)
            x = x_ref[m0:m0 + mr, :]
            r_cur[m0:m0 + mr, :] = _dot(x, wyb_ref[...])
            b_cur[CARRY_ROWS + m0:CARRY_ROWS + m0 + mr, :] = _dot(x, wxb_ref[...])

    @pl.when(lax.rem(t, 2) == 0)
    def _():
        step(b0_ref, r0_ref, b1_ref, r1_ref)

    @pl.when(lax.rem(t, 2) == 1)
    def _():
        step(b1_ref, r1_ref, b0_ref, r0_ref)


def in_odd(xn, w_in, conv_w, conv_b, layer):
    m, k = xn.shape
    tm, tn = TM_MM, TN_ODD
    ni, nj = m // tm, LRU_WIDTH // tn
    nt, cur_row, cur_col, prev_row, prev_col = _tile_maps(ni, nj)
    nbytes = (2 * tm * k * 2 + 4 * k * tn * 4 + 2 * k * tn * 2 + 4 * tm * tn * 4
              + 4 * (tm + CARRY_ROWS) * tn * 4)
    out = pl.BlockSpec((tm, tn), lambda t: (prev_row(t), prev_col(t)))
    return pl.pallas_call(
        functools.partial(_in_odd_kernel, tm=tm, ni=ni, nt=nt),
        out_shape=(jax.ShapeDtypeStruct((m, LRU_WIDTH), F32), jax.ShapeDtypeStruct((m, LRU_WIDTH), F32)),
        grid=(nt + 1,),
        in_specs=[pl.BlockSpec((tm, k), lambda t: (cur_row(t), 0)),
                  pl.BlockSpec((None, k, tn), lambda t: (layer, 0, cur_col(t))),
                  pl.BlockSpec((None, k, tn), lambda t: (layer, 0, cur_col(t) + nj)),
                  pl.BlockSpec((None, LRU_CONV, tn), lambda t: (layer, 0, prev_col(t))),
                  pl.BlockSpec((None, 1, tn), lambda t: (layer, 0, prev_col(t)))],
        out_specs=(out, out),
        scratch_shapes=[pltpu.VMEM((k, tn), BF16), pltpu.VMEM((k, tn), BF16),
                        pltpu.VMEM((tm + CARRY_ROWS, tn), F32), pltpu.VMEM((tm + CARRY_ROWS, tn), F32),
                        pltpu.VMEM((tm, tn), F32), pltpu.VMEM((tm, tn), F32)],
        compiler_params=_params(("arbitrary",), nbytes),
        name="in_odd",
    )(xn, w_in, w_in, conv_w, conv_b.reshape(conv_b.shape[0], 1, LRU_WIDTH))


def _lru_kernel(xc_ref, y_ref, wa_ref, wx_ref, ba_ref, bx_ref, ap_ref, o_ref, a_ref, b_ref, hc_ref, *, tr):
    step = pl.program_id(0)

    @pl.when(step == 0)
    def _():
        hc_ref[...] = jnp.zeros_like(hc_ref)

    valid = (step * tr + _iota((tr, 1), 0)) >= PAD
    for blk in range(LRU_BLOCKS):
        sl = slice(blk * LRU_BDIM, (blk + 1) * LRU_BDIM)
        x = xc_ref[:, sl]
        xb = x.astype(BF16)
        gate_r = jax.nn.sigmoid(_dot(xb, wa_ref[blk]) + ba_ref[:, sl])
        gate_i = jax.nn.sigmoid(_dot(xb, wx_ref[blk]) + bx_ref[:, sl])
        log_a = LRU_C * gate_r * jax.nn.log_sigmoid(ap_ref[:, sl])
        a_ref[:, sl] = jnp.exp(log_a)
        inp = jnp.sqrt(-_expm1(2.0 * log_a)) * (gate_i * x)
        b_ref[:, sl] = jnp.where(valid, inp, 0.0)

    ridx = _iota((SCAN_ROWS, CW_LRU), 0)
    for c in range(LRU_WIDTH // CW_LRU):
        cs = slice(c * CW_LRU, (c + 1) * CW_LRU)

        def body(g, carry, cs=cs):
            r0 = pl.multiple_of(g * SCAN_ROWS, SCAN_ROWS)
            a = a_ref[pl.ds(r0, SCAN_ROWS), cs]
            b = b_ref[pl.ds(r0, SCAN_ROWS), cs]
            shift = 1
            while shift < SCAN_ROWS:
                ok = ridx >= shift
                b = jnp.where(ok, a * pltpu.roll(b, shift, 0) + b, b)
                a = jnp.where(ok, a * pltpu.roll(a, shift, 0), a)
                shift *= 2
            h = a * carry + b
            o_ref[pl.ds(r0, SCAN_ROWS), cs] = (h * y_ref[pl.ds(r0, SCAN_ROWS), cs]).astype(o_ref.dtype)
            return h[SCAN_ROWS - 1:SCAN_ROWS, :]

        hc_ref[0:1, cs] = lax.fori_loop(0, tr // SCAN_ROWS, body, hc_ref[0:1, cs])


def rglru(xc, y, w_a, b_a, w_x, b_x, a_param):
    m, d = xc.shape
    tr = TR_LRU
    row = pl.BlockSpec((tr, d), lambda i: (i, 0))
    vec = pl.BlockSpec((1, d), lambda i: (0, 0))
    wspec = pl.BlockSpec((LRU_BLOCKS, LRU_BDIM, LRU_BDIM), lambda i: (0, 0, 0))
    nbytes = 4 * tr * d * 4 + 2 * tr * d * 2 + 2 * tr * d * 4 + 4 * LRU_BLOCKS * LRU_BDIM * LRU_BDIM * 2
    return pl.pallas_call(
        functools.partial(_lru_kernel, tr=tr),
        out_shape=jax.ShapeDtypeStruct((m, d), BF16),
        grid=(m // tr,),
        in_specs=[row, row, wspec, wspec, vec, vec, vec],
        out_specs=row,
        scratch_shapes=[pltpu.VMEM((tr, d), F32), pltpu.VMEM((tr, d), F32),
                        pltpu.VMEM((V7X_SUBLANES, d), F32)],
        compiler_params=_params(("arbitrary",), nbytes),
        name="rglru",
    )(xc, y, w_a.astype(BF16), w_x.astype(BF16), b_a.reshape(1, d), b_x.reshape(1, d),
      a_param.reshape(1, d))


def _hgrn_kernel(q_ref, f_ref, i_ref, g_ref, lbl_ref, gnw_ref, o_ref, st_ref, *, tr, layer_j):
    step = pl.program_id(1)

    @pl.when(step == 0)
    def _():
        st_ref[...] = jnp.zeros_like(st_ref)

    logits = lbl_ref[...]
    e = jnp.exp(logits - jnp.max(logits, axis=0, keepdims=True))
    lb = jnp.sum(e[0:layer_j + 1], axis=0, keepdims=True) / jnp.sum(e, axis=0, keepdims=True)

    valid = (step * tr + _iota((tr, 1), 0)) >= PAD
    q = jax.nn.silu(q_ref[...])
    forget = lb + (1.0 - lb) * jax.nn.sigmoid(f_ref[...])
    k = jnp.where(valid, 1.0 - forget, 0.0)
    g = jnp.where(valid, jnp.log(forget), 0.0)
    v = i_ref[...]

    pos = _iota((tr, A_KDIM), 0) & (HGRN_CHUNK - 1)
    b = g
    shift = 1
    while shift < HGRN_CHUNK:
        b = b + jnp.where(pos >= shift, pltpu.roll(b, shift, 0), 0.0)
        shift *= 2

    n_sub = HGRN_CHUNK // HGRN_SUB
    cpos = _iota((HGRN_CHUNK, A_KDIM), 0)
    causal = _iota((HGRN_CHUNK, HGRN_CHUNK), 0) >= _iota((HGRN_CHUNK, HGRN_CHUNK), 1)
    gnw = gnw_ref[...]
    st = st_ref[...]
    for c in range(tr // HGRN_CHUNK):
        sl = slice(c * HGRN_CHUNK, (c + 1) * HGRN_CHUNK)
        bc, qc, kc, vc = b[sl], q[sl], k[sl], v[sl]
        vcb = vc.astype(BF16)
        b_last = bc[HGRN_CHUNK - 1:HGRN_CHUNK]
        o_inter = _dot_nt((qc * jnp.exp(bc)).astype(BF16), st.astype(BF16))
        u_t = _dot_tn(vcb, (kc * jnp.exp(b_last - bc)).astype(BF16))
        att_rows = []
        for i in range(n_sub):
            ss = slice(i * HGRN_SUB, (i + 1) * HGRN_SUB)
            ref = jnp.zeros((1, A_KDIM), F32) if i == 0 else bc[i * HGRN_SUB - 1:i * HGRN_SUB]
            q_sc = (qc[ss] * jnp.exp(bc[ss] - ref)).astype(BF16)
            expo = jnp.where(cpos < (i + 1) * HGRN_SUB, ref - bc, 0.0)
            k_sc = (kc * jnp.exp(expo)).astype(BF16)
            att_rows.append(_dot_nt(q_sc, k_sc))
        att = jnp.where(causal, jnp.concatenate(att_rows, axis=0), 0.0)
        o = o_inter + _dot(att.astype(BF16), vcb)
        gate = jax.nn.silu(g_ref[sl, :])
        o_ref[sl, :] = (_rms(o, gnw) * gate).astype(o_ref.dtype)
        st = st * jnp.exp(b_last) + u_t
    st_ref[...] = st


def hgrn2(hproj, lb_logits, gn_w, layer_j):
    m = hproj.shape[0]
    tr = TR_HGRN
    nrow = lb_logits.shape[0]

    def col(off):
        return pl.BlockSpec((tr, A_KDIM), lambda h, t, off=off: (t, h + off))

    nbytes = 2 * 4 * tr * A_KDIM * 4 + 2 * tr * A_VDIM * 2 + 12 * tr * A_KDIM * 4
    return pl.pallas_call(
        functools.partial(_hgrn_kernel, tr=tr, layer_j=layer_j),
        out_shape=jax.ShapeDtypeStruct((m, A_WIDTH), BF16),
        grid=(A_HEADS, m // tr),
        in_specs=[col(0), col(A_HEADS), col(2 * A_HEADS), col(3 * A_HEADS),
                  pl.BlockSpec((nrow, A_KDIM), lambda h, t: (0, h)),
                  pl.BlockSpec((1, A_VDIM), lambda h, t: (0, 0))],
        out_specs=pl.BlockSpec((tr, A_VDIM), lambda h, t: (t, h)),
        scratch_shapes=[pltpu.VMEM((A_VDIM, A_KDIM), F32)],
        compiler_params=_params(("arbitrary", "arbitrary"), nbytes),
        name="hgrn2",
    )(hproj, hproj, hproj, hproj, lb_logits, gn_w.reshape(1, A_VDIM))


PAIRS = B_GROUP // 2
QROWS = PAIRS * ATTN_BLOCK


def _swa_kernel(sink_ref, q_ref, kc_ref, kp_ref, km_ref, vc_ref, vp_ref, vm_ref, o_ref):
    n = pl.program_id(0)
    scale = B_HDIM ** -0.5
    lane_lo = _iota((1, V7X_LANES), 1) < B_HDIM
    tq = _iota((QROWS, ATTN_BLOCK), 0) & (ATTN_BLOCK - 1)
    sk = _iota((QROWS, ATTN_BLOCK), 1)
    cur_ok = sk <= tq + jnp.where(n >= 1, 0, -2 * ATTN_BLOCK)
    prev_ok = sk > tq + jnp.where(n >= 2, 0, 2 * ATTN_BLOCK)
    tq_m = _iota((QROWS, N_META), 0) & (ATTN_BLOCK - 1)
    meta_ok = tq_m - PAD + jnp.where(n >= 1, ATTN_BLOCK, 0) >= _iota((QROWS, N_META), 1)
    pair_of_row = _iota((QROWS, 1), 0) >> (ATTN_BLOCK.bit_length() - 1)

    def split(x, natural_lo):
        rolled = pltpu.roll(x, B_HDIM, 1)
        if natural_lo:
            lo, hi = jnp.where(lane_lo, x, 0.0), jnp.where(lane_lo, 0.0, rolled)
        else:
            lo, hi = jnp.where(lane_lo, rolled, 0.0), jnp.where(lane_lo, 0.0, x)
        return lo.astype(BF16), hi.astype(BF16)

    for h in range(B_KVHEADS):
        tile = slice((h // 2) * V7X_LANES, (h // 2 + 1) * V7X_LANES)
        nat = h % 2 == 0
        kc, kp, km = split(kc_ref[:, tile], nat), split(kp_ref[:, tile], nat), split(km_ref[PAD:, tile], nat)
        vc, vp, vm = split(vc_ref[:, tile], nat), split(vp_ref[:, tile], nat), split(vm_ref[PAD:, tile], nat)
        qs = jnp.concatenate(
            [q_ref[:, (h * PAIRS + p) * V7X_LANES:(h * PAIRS + p + 1) * V7X_LANES] for p in range(PAIRS)],
            axis=0).astype(BF16)
        acc = jnp.zeros((QROWS, V7X_LANES), F32)
        for par in range(2):
            sink = jnp.zeros((QROWS, 1), F32)
            for p in range(PAIRS):
                sink = jnp.where(pair_of_row == p, sink_ref[h * B_GROUP + 2 * p + par], sink)
            l_c = jnp.where(cur_ok, _dot_nt(qs, kc[par]) * scale, NEG_INF)
            l_p = jnp.where(prev_ok, _dot_nt(qs, kp[par]) * scale, NEG_INF)
            l_m = jnp.where(meta_ok, _dot_nt(qs, km[par]) * scale, NEG_INF)
            mx = jnp.maximum(jnp.maximum(jnp.max(l_c, axis=-1, keepdims=True),
                                         jnp.max(l_p, axis=-1, keepdims=True)),
                             jnp.maximum(jnp.max(l_m, axis=-1, keepdims=True), sink))
            e_c, e_p, e_m = jnp.exp(l_c - mx), jnp.exp(l_p - mx), jnp.exp(l_m - mx)
            den = (jnp.sum(e_c, axis=-1, keepdims=True) + jnp.sum(e_p, axis=-1, keepdims=True)
                   + jnp.sum(e_m, axis=-1, keepdims=True) + jnp.exp(sink - mx))
            pv = (_dot(e_c.astype(BF16), vc[par]) + _dot(e_p.astype(BF16), vp[par])
                  + _dot(e_m.astype(BF16), vm[par]))
            acc = acc + pv / den
        for p in range(PAIRS):
            o_ref[:, (h * PAIRS + p) * V7X_LANES:(h * PAIRS + p + 1) * V7X_LANES] = (
                acc[p * ATTN_BLOCK:(p + 1) * ATTN_BLOCK].astype(o_ref.dtype))


def swa(hproj, sinks):
    m = hproj.shape[0]
    blk = ATTN_BLOCK
    q_col = (2 * A_FDIM + 2 * A_WIDTH) // B_WIDTH
    k_col = (2 * A_FDIM + 2 * A_WIDTH + B_WIDTH) // B_KVWIDTH
    v_col = k_col + 1
    qspec = pl.BlockSpec((blk, B_WIDTH), lambda n: (n, q_col))

    def kv(col, which):
        if which == "cur":
            return pl.BlockSpec((blk, B_KVWIDTH), lambda n: (n, col))
        if which == "prev":
            return pl.BlockSpec((blk, B_KVWIDTH), lambda n: (jnp.maximum(n - 1, 0), col))
        return pl.BlockSpec((blk, B_KVWIDTH), lambda n: (0, col))

    nbytes = 2 * blk * B_WIDTH * 4 + 12 * blk * B_KVWIDTH * 4 + 2 * blk * B_WIDTH * 2 + 40 * QROWS * 128 * 4
    return pl.pallas_call(
        _swa_kernel,
        out_shape=jax.ShapeDtypeStruct((m, B_WIDTH), BF16),
        grid=(m // blk,),
        in_specs=[pl.BlockSpec(memory_space=pltpu.SMEM), qspec,
                  kv(k_col, "cur"), kv(k_col, "prev"), kv(k_col, "meta"),
                  kv(v_col, "cur"), kv(v_col, "prev"), kv(v_col, "meta")],
        out_specs=pl.BlockSpec((blk, B_WIDTH), lambda n: (n, 0)),
        compiler_params=_params(("arbitrary",), nbytes),
        name="swa",
    )(sinks, hproj, hproj, hproj, hproj, hproj, hproj, hproj)


def kernel(x, meta_tokens, norm_w, w_in_even, lb_logits, hgrn_gn_w, attn_sinks, w_out_even,
           w_in_odd, lru_conv_w, lru_conv_b, lru_wa, lru_ba, lru_wx, lru_bx, lru_a_param, w_out_odd,
           ffn_w_gu, ffn_conv_w, ffn_conv_b, ffn_w_down):
    assert x.shape == (1, SEQ, D_MODEL) and norm_w.shape[0] == DEPTH
    x2d = x[0]
    xn = rms_cast_stream(x2d, meta_tokens, norm_w[0, 0])
    h, out = None, None
    for layer in range(DEPTH):
        j = layer // 2
        if layer % 2 == 0:
            hproj = matmul([xn], w_in_even, layer=j, tm=TM_MM, tn=TN_MM, name="in_even")
            o_a = hgrn2(hproj, lb_logits, hgrn_gn_w[j], j)
            o_b = swa(hproj, attn_sinks[j])
            mix = matmul([o_a, o_b], w_out_even, layer=j, tm=TM_MM, tn=TN_MM, name="out_even")
        else:
            y_br, x_br = in_odd(xn, w_in_odd, lru_conv_w, lru_conv_b, j)
            rec = rglru(x_br, y_br, lru_wa[j], lru_ba[j], lru_wx[j], lru_bx[j], lru_a_param[j])
            mix = matmul([rec], w_out_odd, layer=j, tm=TM_MM, tn=TN_MM, name="out_odd")
        if h is None:
            h, xn = resid_norm_stream(x2d, meta_tokens, mix, norm_w[layer, 1], norm_w[layer, 2])
        else:
            h, xn = resid_norm(h, mix, norm_w[layer, 1], norm_w[layer, 2])
        act, w_down_bf16 = ffn_up(xn, ffn_w_gu, ffn_conv_w, ffn_conv_b, ffn_w_down, layer)
        ff = matmul([act], w_down_bf16, tm=TM_DOWN, tn=TN_MM, name="ffn_down")
        if layer + 1 < DEPTH:
            h, xn = resid_norm(h, ff, norm_w[layer, 3], norm_w[layer + 1, 0])
        else:
            out = resid_final(h, ff, norm_w[layer, 3])
    return out[None]
```

```python
import functools

import jax
import jax.numpy as jnp
from jax import lax
from jax.experimental import pallas as pl
from jax.experimental.pallas import tpu as pltpu

F32 = jnp.float32
BF16 = jnp.bfloat16

D_MODEL = 4096
SEQ = 8192
DEPTH = 2
N_META = 16
A_HEADS = 16
A_KDIM = 128
A_VDIM = D_MODEL // 2 // A_HEADS
A_FDIM = A_HEADS * A_KDIM
A_WIDTH = A_HEADS * A_VDIM
HGRN_CHUNK = 64
HGRN_SUB = 16
B_HDIM = 64
B_QHEADS = D_MODEL // 2 // B_HDIM
B_KVHEADS = B_QHEADS // 8
B_GROUP = B_QHEADS // B_KVHEADS
B_WIDTH = B_QHEADS * B_HDIM
B_KVWIDTH = B_KVHEADS * B_HDIM
WINDOW = 128
ATTN_BLOCK = 128
EVEN_IN = 2 * A_FDIM + 2 * A_WIDTH + B_WIDTH + 2 * B_KVWIDTH
LRU_WIDTH = D_MODEL
LRU_BLOCKS = 16
LRU_BDIM = LRU_WIDTH // LRU_BLOCKS
LRU_CONV = 4
LRU_C = 8.0
D_FF = 256 * ((8 * D_MODEL // 3 + 255) // 256)
FFN_CONV = 3
NORM_EPS = 1e-6
NEG_INF = -1e30

PAD = ATTN_BLOCK - N_META
ROW0 = PAD + N_META
LP = ROW0 + SEQ
assert PAD % HGRN_CHUNK == HGRN_CHUNK - N_META and ROW0 == ATTN_BLOCK

V7X_LANES = 128
V7X_SUBLANES = 8
V7X_VMEM_LIMIT_CAP = 60 * 1024 * 1024
CARRY_ROWS = V7X_SUBLANES

TM_MM = 1040
TM_IN_EVEN = 520
TN_IN_EVEN = 768
TM_OUT = 520
TN_OUT = 1024
TM_DOWN = 520
TN_DOWN = 512
TF_FFN = 256
TN_ODD = 256
EPI_ROWS = 80
K_CHUNK = 256
TR_NORM = 320
TR_LRU = 208
CW_LRU = 1024
TR_HGRN = 640
SCAN_ROWS = 16


def _vmem_limit(nbytes):
    return int(min(V7X_VMEM_LIMIT_CAP, nbytes * 1.15 + (4 << 20)))


def _params(sem, nbytes):
    return pltpu.CompilerParams(dimension_semantics=sem, vmem_limit_bytes=_vmem_limit(nbytes))


def _rms(x, w):
    return x * lax.rsqrt(jnp.mean(x * x, axis=-1, keepdims=True) + NORM_EPS) * w


def _iota(shape, dim):
    return lax.broadcasted_iota(jnp.int32, shape, dim)


def _expm1(x):
    u = jnp.exp(x)
    um1 = u - 1.0
    return jnp.where(um1 == 0.0, x, jnp.where(um1 == -1.0, -1.0, um1 * x / jnp.log(u)))


def _dot(a, b):
    return jnp.dot(a, b, preferred_element_type=F32)


def _dot_nt(a, b):
    return lax.dot_general(a, b, (((1,), (1,)), ((), ())), preferred_element_type=F32)


def _dot_tn(a, b):
    return lax.dot_general(a, b, (((0,), (0,)), ((), ())), preferred_element_type=F32)


def _stream_rows(n, x_ref, meta_ref):
    first = jnp.concatenate([jnp.zeros((PAD, D_MODEL), F32), meta_ref[...].astype(F32)], axis=0)
    return jnp.where(n == 0, first, x_ref[...].astype(F32))


def _stream_specs():
    blk = ATTN_BLOCK
    return [pl.BlockSpec((blk, D_MODEL), lambda n: (jnp.maximum(n - 1, 0), 0)),
            pl.BlockSpec((N_META, D_MODEL), lambda n: (0, 0))]


def _rms_cast_stream_kernel(x_ref, meta_ref, w_ref, o_ref):
    h = _stream_rows(pl.program_id(0), x_ref, meta_ref)
    o_ref[...] = _rms(h, w_ref[...]).astype(o_ref.dtype)


def rms_cast_stream(x2d, meta, w):
    blk, d = ATTN_BLOCK, D_MODEL
    return pl.pallas_call(
        _rms_cast_stream_kernel,
        out_shape=jax.ShapeDtypeStruct((LP, d), BF16),
        grid=(LP // blk,),
        in_specs=_stream_specs() + [pl.BlockSpec((1, d), lambda n: (0, 0))],
        out_specs=pl.BlockSpec((blk, d), lambda n: (n, 0)),
        compiler_params=_params(("arbitrary",), 2 * blk * d * 6 + 4 * blk * d * 4),
        name="rms_cast_stream",
    )(x2d, meta, w.reshape(1, d))


def _resid_norm_stream_kernel(x_ref, meta_ref, y_ref, wp_ref, wn_ref, ho_ref, xo_ref):
    h = _stream_rows(pl.program_id(0), x_ref, meta_ref) + _rms(y_ref[...], wp_ref[...])
    ho_ref[...] = h
    xo_ref[...] = _rms(h, wn_ref[...]).astype(xo_ref.dtype)


def resid_norm_stream(x2d, meta, y, w_post, w_next):
    blk, d = ATTN_BLOCK, D_MODEL
    row = pl.BlockSpec((blk, d), lambda n: (n, 0))
    vec = pl.BlockSpec((1, d), lambda n: (0, 0))
    return pl.pallas_call(
        _resid_norm_stream_kernel,
        out_shape=(jax.ShapeDtypeStruct((LP, d), F32), jax.ShapeDtypeStruct((LP, d), BF16)),
        grid=(LP // blk,),
        in_specs=_stream_specs() + [row, vec, vec],
        out_specs=(row, row),
        compiler_params=_params(("arbitrary",), 2 * blk * d * 14 + 4 * blk * d * 4),
        name="resid_norm_stream",
    )(x2d, meta, y, w_post.reshape(1, d), w_next.reshape(1, d))


def _resid_norm_kernel(h_ref, y_ref, wp_ref, wn_ref, ho_ref, xo_ref):
    h = h_ref[...] + _rms(y_ref[...], wp_ref[...])
    ho_ref[...] = h
    xo_ref[...] = _rms(h, wn_ref[...]).astype(xo_ref.dtype)


def resid_norm(h, y, w_post, w_next):
    n, d = h.shape
    tr = TR_NORM
    row = pl.BlockSpec((tr, d), lambda i: (i, 0))
    vec = pl.BlockSpec((1, d), lambda i: (0, 0))
    return pl.pallas_call(
        _resid_norm_kernel,
        out_shape=(jax.ShapeDtypeStruct((n, d), F32), jax.ShapeDtypeStruct((n, d), BF16)),
        grid=(n // tr,),
        in_specs=[row, row, vec, vec],
        out_specs=(row, row),
        compiler_params=_params(("arbitrary",), 2 * tr * d * 14),
        name="resid_norm",
    )(h, y, w_post.reshape(1, d), w_next.reshape(1, d))


def _resid_final_kernel(h_ref, y_ref, wp_ref, o_ref):
    o_ref[...] = h_ref[...] + _rms(y_ref[...], wp_ref[...])


def resid_final(h, y, w_post):
    n, d = h.shape
    tr = ATTN_BLOCK
    off = ROW0 // tr
    src = pl.BlockSpec((tr, d), lambda i: (i + off, 0))
    return pl.pallas_call(
        _resid_final_kernel,
        out_shape=jax.ShapeDtypeStruct((n - ROW0, d), F32),
        grid=((n - ROW0) // tr,),
        in_specs=[src, src, pl.BlockSpec((1, d), lambda i: (0, 0))],
        out_specs=pl.BlockSpec((tr, d), lambda i: (i, 0)),
        compiler_params=_params(("arbitrary",), 2 * tr * d * 12),
        name="resid_final",
    )(h, y, w_post.reshape(1, d))


def _weight_spec(w, layer, k, tn, col_of):
    if w.ndim == 3:
        return pl.BlockSpec((None, k, tn), lambda *g: (layer, 0, col_of(*g)))
    return pl.BlockSpec((k, tn), lambda *g: (0, col_of(*g)))


def _mm_kernel(*refs, n_lhs, cast):
    x_refs, w_ref, o_ref = refs[:n_lhs], refs[n_lhs], refs[n_lhs + 1]
    if cast:
        wb_ref = refs[n_lhs + 2]

        @pl.when(pl.program_id(1) == 0)
        def _():
            wb_ref[...] = w_ref[...].astype(BF16)

        w_ref = wb_ref
    acc, k0 = None, 0
    for x_ref in x_refs:
        kk = x_ref.shape[1]
        part = _dot(x_ref[...], w_ref[k0:k0 + kk, :])
        acc = part if acc is None else acc + part
        k0 += kk
    o_ref[...] = acc.astype(o_ref.dtype)


def matmul(xs, w, *, layer=0, tm, tn, out_dtype=F32, name="matmul"):
    m = xs[0].shape[0]
    k, n = w.shape[-2:]
    assert sum(x.shape[1] for x in xs) == k
    cast = w.dtype != BF16
    wbytes = k * tn * (4 if cast else 2)
    nbytes = 2 * tm * k * 2 + 2 * wbytes + (k * tn * 2 if cast else 0) + 2 * tm * tn * 4
    return pl.pallas_call(
        functools.partial(_mm_kernel, n_lhs=len(xs), cast=cast),
        out_shape=jax.ShapeDtypeStruct((m, n), out_dtype),
        grid=(n // tn, m // tm),
        in_specs=[pl.BlockSpec((tm, x.shape[1]), lambda j, i: (i, 0)) for x in xs]
        + [_weight_spec(w, layer, k, tn, lambda j, i: j)],
        out_specs=pl.BlockSpec((tm, tn), lambda j, i: (i, j)),
        scratch_shapes=[pltpu.VMEM((k, tn), BF16)] if cast else [],
        compiler_params=_params(("arbitrary", "arbitrary"), nbytes),
        name=name,
    )(*xs, w)


def _tile_maps(ni, nj):
    nt = ni * nj

    def cur_row(t):
        return lax.rem(jnp.minimum(t, nt - 1), ni)

    def cur_col(t):
        return lax.div(jnp.minimum(t, nt - 1), ni)

    def prev_row(t):
        return lax.rem(jnp.maximum(t - 1, 0), ni)

    def prev_col(t):
        return lax.div(jnp.maximum(t - 1, 0), ni)

    return nt, cur_row, cur_col, prev_row, prev_col


def _interleaved_projections(x_ref, w_refs, dst_stores, epilogue_chunk, tm):
    nk = x_ref.shape[1] // K_CHUNK
    chunk_rows = list(range(0, tm, EPI_ROWS))
    slots, slot, done = len(w_refs) * nk, 0, 0
    for w_ref, store in zip(w_refs, dst_stores):
        acc = None
        for kk in range(nk):
            ks = slice(kk * K_CHUNK, (kk + 1) * K_CHUNK)
            part = _dot(x_ref[:, ks], w_ref[ks, :])
            acc = part if acc is None else acc + part
            slot += 1
            while done < len(chunk_rows) * slot // slots:
                epilogue_chunk(chunk_rows[done])
                done += 1
        store(acc)


def _causal_conv(buf_ref, cw_ref, cb_ref, width, r0, rows):
    win = buf_ref[r0:r0 + CARRY_ROWS + rows, :]
    conv = cb_ref[...]
    for tap in range(width):
        back = width - 1 - tap
        shifted = pltpu.roll(win, back, 0) if back else win
        conv = conv + shifted[CARRY_ROWS:, :] * cw_ref[tap:tap + 1, :]
    return conv


def _ffn_up_kernel(x_ref, wg_ref, wu_ref, cw_ref, cb_ref, wd_ref, o_ref, wdb_ref,
                   wgb_ref, wub_ref, g0_ref, g1_ref, u0_ref, u1_ref, *, tm, ni, nt):
    t = pl.program_id(0)
    i = lax.rem(t, ni)

    @pl.when(t == 0)
    def _():
        g1_ref[...] = jnp.zeros_like(g1_ref)
        u1_ref[...] = jnp.zeros_like(u1_ref)

    @pl.when(jnp.logical_and(i == 0, t < nt))
    def _():
        wgb_ref[...] = wg_ref[...].astype(BF16)
        wub_ref[...] = wu_ref[...].astype(BF16)
        wdb_ref[...] = wd_ref[...].astype(BF16)

    def step(g_cur, u_cur, g_prev, u_prev):
        g_cur[0:CARRY_ROWS, :] = jnp.where(i == 0, 0.0, g_prev[tm:tm + CARRY_ROWS, :])

        def epilogue_chunk(r0):
            conv = _causal_conv(g_prev, cw_ref, cb_ref, FFN_CONV, r0, EPI_ROWS)
            o_ref[r0:r0 + EPI_ROWS, :] = (jax.nn.gelu(conv, approximate=True)
                                          * u_prev[r0:r0 + EPI_ROWS, :]).astype(o_ref.dtype)

        def store_gate(acc):
            g_cur[CARRY_ROWS:CARRY_ROWS + tm, :] = acc

        def store_up(acc):
            u_cur[...] = acc

        _interleaved_projections(x_ref, (wgb_ref, wub_ref), (store_gate, store_up), epilogue_chunk, tm)

    @pl.when(lax.rem(t, 2) == 0)
    def _():
        step(g0_ref, u0_ref, g1_ref, u1_ref)

    @pl.when(lax.rem(t, 2) == 1)
    def _():
        step(g1_ref, u1_ref, g0_ref, u0_ref)


def ffn_up(xn, w_gu, conv_w, conv_b, w_down, layer):
    m, k = xn.shape
    tm, tf = TM_MM, TF_FFN
    ni, nj = m // tm, D_FF // tf
    nt, cur_row, cur_col, prev_row, prev_col = _tile_maps(ni, nj)
    nbytes = (2 * tm * k * 2 + 4 * k * tf * 4 + 2 * k * tf * 2 + 2 * tm * tf * 2
              + 4 * (tm + CARRY_ROWS) * tf * 4 + 2 * tf * D_MODEL * 6)
    return pl.pallas_call(
        functools.partial(_ffn_up_kernel, tm=tm, ni=ni, nt=nt),
        out_shape=(jax.ShapeDtypeStruct((m, D_FF), BF16), jax.ShapeDtypeStruct((D_FF, D_MODEL), BF16)),
        grid=(nt + 1,),
        in_specs=[pl.BlockSpec((tm, k), lambda t: (cur_row(t), 0)),
                  pl.BlockSpec((None, k, tf), lambda t: (layer, 0, cur_col(t))),
                  pl.BlockSpec((None, k, tf), lambda t: (layer, 0, cur_col(t) + nj)),
                  pl.BlockSpec((None, FFN_CONV, tf), lambda t: (layer, 0, prev_col(t))),
                  pl.BlockSpec((None, 1, tf), lambda t: (layer, 0, prev_col(t))),
                  pl.BlockSpec((None, tf, D_MODEL), lambda t: (layer, cur_col(t), 0))],
        out_specs=(pl.BlockSpec((tm, tf), lambda t: (prev_row(t), prev_col(t))),
                   pl.BlockSpec((tf, D_MODEL), lambda t: (cur_col(t), 0))),
        scratch_shapes=[pltpu.VMEM((k, tf), BF16), pltpu.VMEM((k, tf), BF16),
                        pltpu.VMEM((tm + CARRY_ROWS, tf), F32), pltpu.VMEM((tm + CARRY_ROWS, tf), F32),
                        pltpu.VMEM((tm, tf), F32), pltpu.VMEM((tm, tf), F32)],
        compiler_params=_params(("arbitrary",), nbytes),
        name="ffn_up",
    )(xn, w_gu, w_gu, conv_w, conv_b.reshape(conv_b.shape[0], 1, D_FF), w_down)


def _in_odd_kernel(x_ref, wy_ref, wx_ref, cw_ref, cb_ref, y_ref, xc_ref,
                   wyb_ref, wxb_ref, b0_ref, b1_ref, r0_ref, r1_ref, *, tm, ni, nt):
    t = pl.program_id(0)
    i = lax.rem(t, ni)

    @pl.when(t == 0)
    def _():
        b1_ref[...] = jnp.zeros_like(b1_ref)
        r1_ref[...] = jnp.zeros_like(r1_ref)

    @pl.when(jnp.logical_and(i == 0, t < nt))
    def _():
        wyb_ref[...] = wy_ref[...].astype(BF16)
        wxb_ref[...] = wx_ref[...].astype(BF16)

    def step(b_cur, r_cur, b_prev, r_prev):
        b_cur[0:CARRY_ROWS, :] = jnp.where(i == 0, 0.0, b_prev[tm:tm + CARRY_ROWS, :])

        def epilogue_chunk(r0):
            y_ref[r0:r0 + EPI_ROWS, :] = jax.nn.gelu(r_prev[r0:r0 + EPI_ROWS, :], approximate=True)
            xc_ref[r0:r0 + EPI_ROWS, :] = _causal_conv(b_prev, cw_ref, cb_ref, LRU_CONV, r0, EPI_ROWS)

        def store_y(acc):
            r_cur[...] = acc

        def store_x(acc):
            b_cur[CARRY_ROWS:CARRY_ROWS + tm, :] = acc

        _interleaved_projections(x_ref, (wyb_ref, wxb_ref), (store_y, store_x), epilogue_chunk, tm)

    @pl.when(lax.rem(t, 2) == 0)
    def _():
        step(b0_ref, r0_ref, b1_ref, r1_ref)

    @pl.when(lax.rem(t, 2) == 1)
    def _():
        step(b1_ref, r1_ref, b0_ref, r0_ref)


def in_odd(xn, w_in, conv_w, conv_b, layer):
    m, k = xn.shape
    tm, tn = TM_MM, TN_ODD
    ni, nj = m // tm, LRU_WIDTH // tn
    nt, cur_row, cur_col, prev_row, prev_col = _tile_maps(ni, nj)
    nbytes = (2 * tm * k * 2 + 4 * k * tn * 4 + 2 * k * tn * 2 + 4 * tm * tn * 4
              + 4 * (tm + CARRY_ROWS) * tn * 4)
    out = pl.BlockSpec((tm, tn), lambda t: (prev_row(t), prev_col(t)))
    return pl.pallas_call(
        functools.partial(_in_odd_kernel, tm=tm, ni=ni, nt=nt),
        out_shape=(jax.ShapeDtypeStruct((m, LRU_WIDTH), F32), jax.ShapeDtypeStruct((m, LRU_WIDTH), F32)),
        grid=(nt + 1,),
        in_specs=[pl.BlockSpec((tm, k), lambda t: (cur_row(t), 0)),
                  pl.BlockSpec((None, k, tn), lambda t: (layer, 0, cur_col(t))),
                  pl.BlockSpec((None, k, tn), lambda t: (layer, 0, cur_col(t) + nj)),
                  pl.BlockSpec((None, LRU_CONV, tn), lambda t: (layer, 0, prev_col(t))),
                  pl.BlockSpec((None, 1, tn), lambda t: (layer, 0, prev_col(t)))],
        out_specs=(out, out),
        scratch_shapes=[pltpu.VMEM((k, tn), BF16), pltpu.VMEM((k, tn), BF16),
                        pltpu.VMEM((tm + CARRY_ROWS, tn), F32), pltpu.VMEM((tm + CARRY_ROWS, tn), F32),
                        pltpu.VMEM((tm, tn), F32), pltpu.VMEM((tm, tn), F32)],
        compiler_params=_params(("arbitrary",), nbytes),
        name="in_odd",
    )(xn, w_in, w_in, conv_w, conv_b.reshape(conv_b.shape[0], 1, LRU_WIDTH))


def _lru_kernel(xc_ref, y_ref, wa_ref, wx_ref, ba_ref, bx_ref, ap_ref, o_ref, a_ref, b_ref, hc_ref, *, tr):
    step = pl.program_id(0)

    @pl.when(step == 0)
    def _():
        hc_ref[...] = jnp.zeros_like(hc_ref)

    valid = (step * tr + _iota((tr, 1), 0)) >= PAD
    for blk in range(LRU_BLOCKS):
        sl = slice(blk * LRU_BDIM, (blk + 1) * LRU_BDIM)
        x = xc_ref[:, sl]
        xb = x.astype(BF16)
        gate_r = jax.nn.sigmoid(_dot(xb, wa_ref[blk]) + ba_ref[:, sl])
        gate_i = jax.nn.sigmoid(_dot(xb, wx_ref[blk]) + bx_ref[:, sl])
        log_a = LRU_C * gate_r * jax.nn.log_sigmoid(ap_ref[:, sl])
        a_ref[:, sl] = jnp.exp(log_a)
        inp = jnp.sqrt(-_expm1(2.0 * log_a)) * (gate_i * x)
        b_ref[:, sl] = jnp.where(valid, inp, 0.0)

    ridx = _iota((SCAN_ROWS, CW_LRU), 0)
    for c in range(LRU_WIDTH // CW_LRU):
        cs = slice(c * CW_LRU, (c + 1) * CW_LRU)

        def body(g, carry, cs=cs):
            r0 = pl.multiple_of(g * SCAN_ROWS, SCAN_ROWS)
            a = a_ref[pl.ds(r0, SCAN_ROWS), cs]
            b = b_ref[pl.ds(r0, SCAN_ROWS), cs]
            shift = 1
            while shift < SCAN_ROWS:
                ok = ridx >= shift
                b = jnp.where(ok, a * pltpu.roll(b, shift, 0) + b, b)
                a = jnp.where(ok, a * pltpu.roll(a, shift, 0), a)
                shift *= 2
            h = a * carry + b
            o_ref[pl.ds(r0, SCAN_ROWS), cs] = (h * y_ref[pl.ds(r0, SCAN_ROWS), cs]).astype(o_ref.dtype)
            return h[SCAN_ROWS - 1:SCAN_ROWS, :]

        hc_ref[0:1, cs] = lax.fori_loop(0, tr // SCAN_ROWS, body, hc_ref[0:1, cs])


def rglru(xc, y, w_a, b_a, w_x, b_x, a_param):
    m, d = xc.shape
    tr = TR_LRU
    row = pl.BlockSpec((tr, d), lambda i: (i, 0))
    vec = pl.BlockSpec((1, d), lambda i: (0, 0))
    wspec = pl.BlockSpec((LRU_BLOCKS, LRU_BDIM, LRU_BDIM), lambda i: (0, 0, 0))
    nbytes = 4 * tr * d * 4 + 2 * tr * d * 2 + 2 * tr * d * 4 + 4 * LRU_BLOCKS * LRU_BDIM * LRU_BDIM * 2
    return pl.pallas_call(
        functools.partial(_lru_kernel, tr=tr),
        out_shape=jax.ShapeDtypeStruct((m, d), BF16),
        grid=(m // tr,),
        in_specs=[row, row, wspec, wspec, vec, vec, vec],
        out_specs=row,
        scratch_shapes=[pltpu.VMEM((tr, d), F32), pltpu.VMEM((tr, d), F32),
                        pltpu.VMEM((V7X_SUBLANES, d), F32)],
        compiler_params=_params(("arbitrary",), nbytes),
        name="rglru",
    )(xc, y, w_a.astype(BF16), w_x.astype(BF16), b_a.reshape(1, d), b_x.reshape(1, d),
      a_param.reshape(1, d))


def _hgrn_kernel(q_ref, f_ref, i_ref, g_ref, lbl_ref, gnw_ref, o_ref, st_ref, *, tr, layer_j):
    step = pl.program_id(1)

    @pl.when(step == 0)
    def _():
        st_ref[...] = jnp.zeros_like(st_ref)

    logits = lbl_ref[...]
    e = jnp.exp(logits - jnp.max(logits, axis=0, keepdims=True))
    lb = jnp.sum(e[0:layer_j + 1], axis=0, keepdims=True) / jnp.sum(e, axis=0, keepdims=True)

    valid = (step * tr + _iota((tr, 1), 0)) >= PAD
    q = jax.nn.silu(q_ref[...])
    forget = lb + (1.0 - lb) * jax.nn.sigmoid(f_ref[...])
    k = jnp.where(valid, 1.0 - forget, 0.0)
    g = jnp.where(valid, jnp.log(forget), 0.0)
    v = i_ref[...]

    pos = _iota((tr, A_KDIM), 0) & (HGRN_CHUNK - 1)
    b = g
    shift = 1
    while shift < HGRN_CHUNK:
        b = b + jnp.where(pos >= shift, pltpu.roll(b, shift, 0), 0.0)
        shift *= 2

    n_sub = HGRN_CHUNK // HGRN_SUB
    cpos = _iota((HGRN_CHUNK, A_KDIM), 0)
    causal = _iota((HGRN_CHUNK, HGRN_CHUNK), 0) >= _iota((HGRN_CHUNK, HGRN_CHUNK), 1)
    gnw = gnw_ref[...]
    st = st_ref[...]
    for c in range(tr // HGRN_CHUNK):
        sl = slice(c * HGRN_CHUNK, (c + 1) * HGRN_CHUNK)
        bc, qc, kc, vc = b[sl], q[sl], k[sl], v[sl]
        vcb = vc.astype(BF16)
        b_last = bc[HGRN_CHUNK - 1:HGRN_CHUNK]
        o_inter = _dot_nt((qc * jnp.exp(bc)).astype(BF16), st.astype(BF16))
        u_t = _dot_tn(vcb, (kc * jnp.exp(b_last - bc)).astype(BF16))
        att_rows = []
        for i in range(n_sub):
            ss = slice(i * HGRN_SUB, (i + 1) * HGRN_SUB)
            ref = jnp.zeros((1, A_KDIM), F32) if i == 0 else bc[i * HGRN_SUB - 1:i * HGRN_SUB]
            q_sc = (qc[ss] * jnp.exp(bc[ss] - ref)).astype(BF16)
            expo = jnp.where(cpos < (i + 1) * HGRN_SUB, ref - bc, 0.0)
            k_sc = (kc * jnp.exp(expo)).astype(BF16)
            att_rows.append(_dot_nt(q_sc, k_sc))
        att = jnp.where(causal, jnp.concatenate(att_rows, axis=0), 0.0)
        o = o_inter + _dot(att.astype(BF16), vcb)
        gate = jax.nn.silu(g_ref[sl, :])
        o_ref[sl, :] = (_rms(o, gnw) * gate).astype(o_ref.dtype)
        st = st * jnp.exp(b_last) + u_t
    st_ref[...] = st


def hgrn2(hproj, lb_logits, gn_w, layer_j):
    m = hproj.shape[0]
    tr = TR_HGRN
    nrow = lb_logits.shape[0]

    def col(off):
        return pl.BlockSpec((tr, A_KDIM), lambda h, t, off=off: (t, h + off))

    nbytes = 2 * 4 * tr * A_KDIM * 4 + 2 * tr * A_VDIM * 2 + 12 * tr * A_KDIM * 4
    return pl.pallas_call(
        functools.partial(_hgrn_kernel, tr=tr, layer_j=layer_j),
        out_shape=jax.ShapeDtypeStruct((m, A_WIDTH), BF16),
        grid=(A_HEADS, m // tr),
        in_specs=[col(0), col(A_HEADS), col(2 * A_HEADS), col(3 * A_HEADS),
                  pl.BlockSpec((nrow, A_KDIM), lambda h, t: (0, h)),
                  pl.BlockSpec((1, A_VDIM), lambda h, t: (0, 0))],
        out_specs=pl.BlockSpec((tr, A_VDIM), lambda h, t: (t, h)),
        scratch_shapes=[pltpu.VMEM((A_VDIM, A_KDIM), F32)],
        compiler_params=_params(("arbitrary", "arbitrary"), nbytes),
        name="hgrn2",
    )(hproj, hproj, hproj, hproj, lb_logits, gn_w.reshape(1, A_VDIM))


PAIRS = B_GROUP // 2
QROWS = PAIRS * ATTN_BLOCK


def _swa_kernel(sink_ref, q_ref, kc_ref, kp_ref, km_ref, vc_ref, vp_ref, vm_ref, o_ref):
    n = pl.program_id(0)
    scale = B_HDIM ** -0.5
    lane_lo = _iota((1, V7X_LANES), 1) < B_HDIM
    tq = _iota((QROWS, ATTN_BLOCK), 0) & (ATTN_BLOCK - 1)
    sk = _iota((QROWS, ATTN_BLOCK), 1)
    cur_ok = sk <= tq + jnp.where(n >= 1, 0, -2 * ATTN_BLOCK)
    prev_ok = sk > tq + jnp.where(n >= 2, 0, 2 * ATTN_BLOCK)
    tq_m = _iota((QROWS, N_META), 0) & (ATTN_BLOCK - 1)
    meta_ok = tq_m - PAD + jnp.where(n >= 1, ATTN_BLOCK, 0) >= _iota((QROWS, N_META), 1)
    pair_of_row = _iota((QROWS, 1), 0) >> (ATTN_BLOCK.bit_length() - 1)

    def split(x, natural_lo):
        rolled = pltpu.roll(x, B_HDIM, 1)
        if natural_lo:
            lo, hi = jnp.where(lane_lo, x, 0.0), jnp.where(lane_lo, 0.0, rolled)
        else:
            lo, hi = jnp.where(lane_lo, rolled, 0.0), jnp.where(lane_lo, 0.0, x)
        return lo.astype(BF16), hi.astype(BF16)

    for h in range(B_KVHEADS):
        tile = slice((h // 2) * V7X_LANES, (h // 2 + 1) * V7X_LANES)
        nat = h % 2 == 0
        kc, kp, km = split(kc_ref[:, tile], nat), split(kp_ref[:, tile], nat), split(km_ref[PAD:, tile], nat)
        vc, vp, vm = split(vc_ref[:, tile], nat), split(vp_ref[:, tile], nat), split(vm_ref[PAD:, tile], nat)
        qs = jnp.concatenate(
            [q_ref[:, (h * PAIRS + p) * V7X_LANES:(h * PAIRS + p + 1) * V7X_LANES] for p in range(PAIRS)],
            axis=0).astype(BF16)
        acc = jnp.zeros((QROWS, V7X_LANES), F32)
        for par in range(2):
            sink = jnp.zeros((QROWS, 1), F32)
            for p in range(PAIRS):
                sink = jnp.where(pair_of_row == p, sink_ref[h * B_GROUP + 2 * p + par], sink)
            l_c = jnp.where(cur_ok, _dot_nt(qs, kc[par]) * scale, NEG_INF)
            l_p = jnp.where(prev_ok, _dot_nt(qs, kp[par]) * scale, NEG_INF)
            l_m = jnp.where(meta_ok, _dot_nt(qs, km[par]) * scale, NEG_INF)
            mx = jnp.maximum(jnp.maximum(jnp.max(l_c, axis=-1, keepdims=True),
                                         jnp.max(l_p, axis=-1, keepdims=True)),
                             jnp.maximum(jnp.max(l_m, axis=-1, keepdims=True), sink))
            e_c, e_p, e_m = jnp.exp(l_c - mx), jnp.exp(l_p - mx), jnp.exp(l_m - mx)
            den = (jnp.sum(e_c, axis=-1, keepdims=True) + jnp.sum(e_p, axis=-1, keepdims=True)
                   + jnp.sum(e_m, axis=-1, keepdims=True) + jnp.exp(sink - mx))
            pv = (_dot(e_c.astype(BF16), vc[par]) + _dot(e_p.astype(BF16), vp[par])
                  + _dot(e_m.astype(BF16), vm[par]))
            acc = acc + pv / den
        for p in range(PAIRS):
            o_ref[:, (h * PAIRS + p) * V7X_LANES:(h * PAIRS + p + 1) * V7X_LANES] = (
                acc[p * ATTN_BLOCK:(p + 1) * ATTN_BLOCK].astype(o_ref.dtype))


def swa(hproj, sinks):
    m = hproj.shape[0]
    blk = ATTN_BLOCK
    q_col = (2 * A_FDIM + 2 * A_WIDTH) // B_WIDTH
    k_col = (2 * A_FDIM + 2 * A_WIDTH + B_WIDTH) // B_KVWIDTH
    v_col = k_col + 1
    qspec = pl.BlockSpec((blk, B_WIDTH), lambda n: (n, q_col))

    def kv(col, which):
        if which == "cur":
            return pl.BlockSpec((blk, B_KVWIDTH), lambda n: (n, col))
        if which == "prev":
            return pl.BlockSpec((blk, B_KVWIDTH), lambda n: (jnp.maximum(n - 1, 0), col))
        return pl.BlockSpec((blk, B_KVWIDTH), lambda n: (0, col))

    nbytes = 2 * blk * B_WIDTH * 4 + 12 * blk * B_KVWIDTH * 4 + 2 * blk * B_WIDTH * 2 + 40 * QROWS * 128 * 4
    return pl.pallas_call(
        _swa_kernel,
        out_shape=jax.ShapeDtypeStruct((m, B_WIDTH), BF16),
        grid=(m // blk,),
        in_specs=[pl.BlockSpec(memory_space=pltpu.SMEM), qspec,
                  kv(k_col, "cur"), kv(k_col, "prev"), kv(k_col, "meta"),
                  kv(v_col, "cur"), kv(v_col, "prev"), kv(v_col, "meta")],
        out_specs=pl.BlockSpec((blk, B_WIDTH), lambda n: (n, 0)),
        compiler_params=_params(("arbitrary",), nbytes),
        name="swa",
    )(sinks, hproj, hproj, hproj, hproj, hproj, hproj, hproj)


def kernel(x, meta_tokens, norm_w, w_in_even, lb_logits, hgrn_gn_w, attn_sinks, w_out_even,
           w_in_odd, lru_conv_w, lru_conv_b, lru_wa, lru_ba, lru_wx, lru_bx, lru_a_param, w_out_odd,
           ffn_w_gu, ffn_conv_w, ffn_conv_b, ffn_w_down):
    assert x.shape == (1, SEQ, D_MODEL) and norm_w.shape[0] == DEPTH
    x2d = x[0]
    xn = rms_cast_stream(x2d, meta_tokens, norm_w[0, 0])
    h, out = None, None
    for layer in range(DEPTH):
        j = layer // 2
        if layer % 2 == 0:
            hproj = matmul([xn], w_in_even, layer=j, tm=TM_IN_EVEN, tn=TN_IN_EVEN, name="in_even")
            o_a = hgrn2(hproj, lb_logits, hgrn_gn_w[j], j)
            o_b = swa(hproj, attn_sinks[j])
            mix = matmul([o_a, o_b], w_out_even, layer=j, tm=TM_OUT, tn=TN_OUT, name="out_even")
        else:
            y_br, x_br = in_odd(xn, w_in_odd, lru_conv_w, lru_conv_b, j)
            rec = rglru(x_br, y_br, lru_wa[j], lru_ba[j], lru_wx[j], lru_bx[j], lru_a_param[j])
            mix = matmul([rec], w_out_odd, layer=j, tm=TM_OUT, tn=TN_OUT, name="out_odd")
        if h is None:
            h, xn = resid_norm_stream(x2d, meta_tokens, mix, norm_w[layer, 1], norm_w[layer, 2])
        else:
            h, xn = resid_norm(h, mix, norm_w[layer, 1], norm_w[layer, 2])
        act, w_down_bf16 = ffn_up(xn, ffn_w_gu, ffn_conv_w, ffn_conv_b, ffn_w_down, layer)
        ff = matmul([act], w_down_bf16, tm=TM_DOWN, tn=TN_DOWN, name="ffn_down")
        if layer + 1 < DEPTH:
            h, xn = resid_norm(h, ff, norm_w[layer, 3], norm_w[layer + 1, 0])
        else:
            out = resid_final(h, ff, norm_w[layer, 3])
    return out[None]
```

```python
import functools

import jax
import jax.numpy as jnp
from jax import lax
from jax.experimental import pallas as pl
from jax.experimental.pallas import tpu as pltpu

F32 = jnp.float32
BF16 = jnp.bfloat16

D_MODEL = 4096
SEQ = 8192
DEPTH = 2
N_META = 16
A_HEADS = 16
A_KDIM = 128
A_VDIM = D_MODEL // 2 // A_HEADS
A_FDIM = A_HEADS * A_KDIM
A_WIDTH = A_HEADS * A_VDIM
HGRN_CHUNK = 64
HGRN_SUB = 16
B_HDIM = 64
B_QHEADS = D_MODEL // 2 // B_HDIM
B_KVHEADS = B_QHEADS // 8
B_GROUP = B_QHEADS // B_KVHEADS
B_WIDTH = B_QHEADS * B_HDIM
B_KVWIDTH = B_KVHEADS * B_HDIM
WINDOW = 128
ATTN_BLOCK = 128
EVEN_IN = 2 * A_FDIM + 2 * A_WIDTH + B_WIDTH + 2 * B_KVWIDTH
LRU_WIDTH = D_MODEL
LRU_BLOCKS = 16
LRU_BDIM = LRU_WIDTH // LRU_BLOCKS
LRU_CONV = 4
LRU_C = 8.0
D_FF = 256 * ((8 * D_MODEL // 3 + 255) // 256)
FFN_CONV = 3
NORM_EPS = 1e-6
NEG_INF = -1e30

PAD = ATTN_BLOCK - N_META
ROW0 = PAD + N_META
LP = ROW0 + SEQ
assert PAD % HGRN_CHUNK == HGRN_CHUNK - N_META and ROW0 == ATTN_BLOCK

V7X_LANES = 128
V7X_SUBLANES = 8
V7X_VMEM_LIMIT_CAP = 60 * 1024 * 1024
CARRY_ROWS = V7X_SUBLANES

TM_MM = 1040
TM_IN_EVEN = 1040
TN_IN_EVEN = 512
TM_OUT = 520
TN_OUT = 1024
TM_DOWN = 520
TN_DOWN = 512
TF_FFN = 256
TN_ODD = 256
EPI_ROWS = 80
K_CHUNK = 256
TR_NORM = 320
TR_LRU = 208
CW_LRU = 1024
TR_HGRN = 640
SCAN_ROWS = 16


def _vmem_limit(nbytes):
    return int(min(V7X_VMEM_LIMIT_CAP, nbytes * 1.15 + (4 << 20)))


def _params(sem, nbytes):
    return pltpu.CompilerParams(dimension_semantics=sem, vmem_limit_bytes=_vmem_limit(nbytes))


def _rms(x, w):
    return x * lax.rsqrt(jnp.mean(x * x, axis=-1, keepdims=True) + NORM_EPS) * w


def _iota(shape, dim):
    return lax.broadcasted_iota(jnp.int32, shape, dim)


def _dot(a, b):
    return jnp.dot(a, b, preferred_element_type=F32)


def _dot_nt(a, b):
    return lax.dot_general(a, b, (((1,), (1,)), ((), ())), preferred_element_type=F32)


def _dot_tn(a, b):
    return lax.dot_general(a, b, (((0,), (0,)), ((), ())), preferred_element_type=F32)


def _stream_rows(n, x_ref, meta_ref):
    first = jnp.concatenate([jnp.zeros((PAD, D_MODEL), F32), meta_ref[...].astype(F32)], axis=0)
    return jnp.where(n == 0, first, x_ref[...].astype(F32))


def _stream_specs():
    blk = ATTN_BLOCK
    return [pl.BlockSpec((blk, D_MODEL), lambda n: (jnp.maximum(n - 1, 0), 0)),
            pl.BlockSpec((N_META, D_MODEL), lambda n: (0, 0))]


def _rms_cast_stream_kernel(x_ref, meta_ref, w_ref, o_ref):
    h = _stream_rows(pl.program_id(0), x_ref, meta_ref)
    o_ref[...] = _rms(h, w_ref[...]).astype(o_ref.dtype)


def rms_cast_stream(x2d, meta, w):
    blk, d = ATTN_BLOCK, D_MODEL
    return pl.pallas_call(
        _rms_cast_stream_kernel,
        out_shape=jax.ShapeDtypeStruct((LP, d), BF16),
        grid=(LP // blk,),
        in_specs=_stream_specs() + [pl.BlockSpec((1, d), lambda n: (0, 0))],
        out_specs=pl.BlockSpec((blk, d), lambda n: (n, 0)),
        compiler_params=_params(("arbitrary",), 2 * blk * d * 6 + 4 * blk * d * 4),
        name="rms_cast_stream",
    )(x2d, meta, w.reshape(1, d))


def _resid_norm_stream_kernel(x_ref, meta_ref, y_ref, wp_ref, wn_ref, ho_ref, xo_ref):
    h = _stream_rows(pl.program_id(0), x_ref, meta_ref) + _rms(y_ref[...], wp_ref[...])
    ho_ref[...] = h
    xo_ref[...] = _rms(h, wn_ref[...]).astype(xo_ref.dtype)


def resid_norm_stream(x2d, meta, y, w_post, w_next):
    blk, d = ATTN_BLOCK, D_MODEL
    row = pl.BlockSpec((blk, d), lambda n: (n, 0))
    vec = pl.BlockSpec((1, d), lambda n: (0, 0))
    return pl.pallas_call(
        _resid_norm_stream_kernel,
        out_shape=(jax.ShapeDtypeStruct((LP, d), F32), jax.ShapeDtypeStruct((LP, d), BF16)),
        grid=(LP // blk,),
        in_specs=_stream_specs() + [row, vec, vec],
        out_specs=(row, row),
        compiler_params=_params(("arbitrary",), 2 * blk * d * 14 + 4 * blk * d * 4),
        name="resid_norm_stream",
    )(x2d, meta, y, w_post.reshape(1, d), w_next.reshape(1, d))


def _resid_norm_kernel(h_ref, y_ref, wp_ref, wn_ref, ho_ref, xo_ref):
    h = h_ref[...] + _rms(y_ref[...], wp_ref[...])
    ho_ref[...] = h
    xo_ref[...] = _rms(h, wn_ref[...]).astype(xo_ref.dtype)


def resid_norm(h, y, w_post, w_next):
    n, d = h.shape
    tr = TR_NORM
    row = pl.BlockSpec((tr, d), lambda i: (i, 0))
    vec = pl.BlockSpec((1, d), lambda i: (0, 0))
    return pl.pallas_call(
        _resid_norm_kernel,
        out_shape=(jax.ShapeDtypeStruct((n, d), F32), jax.ShapeDtypeStruct((n, d), BF16)),
        grid=(n // tr,),
        in_specs=[row, row, vec, vec],
        out_specs=(row, row),
        compiler_params=_params(("arbitrary",), 2 * tr * d * 14),
        name="resid_norm",
    )(h, y, w_post.reshape(1, d), w_next.reshape(1, d))


def _resid_final_kernel(h_ref, y_ref, wp_ref, o_ref):
    o_ref[...] = h_ref[...] + _rms(y_ref[...], wp_ref[...])


def resid_final(h, y, w_post):
    n, d = h.shape
    tr = ATTN_BLOCK
    off = ROW0 // tr
    src = pl.BlockSpec((tr, d), lambda i: (i + off, 0))
    return pl.pallas_call(
        _resid_final_kernel,
        out_shape=jax.ShapeDtypeStruct((n - ROW0, d), F32),
        grid=((n - ROW0) // tr,),
        in_specs=[src, src, pl.BlockSpec((1, d), lambda i: (0, 0))],
        out_specs=pl.BlockSpec((tr, d), lambda i: (i, 0)),
        compiler_params=_params(("arbitrary",), 2 * tr * d * 12),
        name="resid_final",
    )(h, y, w_post.reshape(1, d))


def _weight_spec(w, layer, k, tn, col_of):
    if w.ndim == 3:
        return pl.BlockSpec((None, k, tn), lambda *g: (layer, 0, col_of(*g)))
    return pl.BlockSpec((k, tn), lambda *g: (0, col_of(*g)))


def _mm_kernel(*refs, n_lhs, cast):
    x_refs, w_ref, o_ref = refs[:n_lhs], refs[n_lhs], refs[n_lhs + 1]
    if cast:
        wb_ref = refs[n_lhs + 2]

        @pl.when(pl.program_id(1) == 0)
        def _():
            wb_ref[...] = w_ref[...].astype(BF16)

        w_ref = wb_ref
    acc, k0 = None, 0
    for x_ref in x_refs:
        kk = x_ref.shape[1]
        part = _dot(x_ref[...], w_ref[k0:k0 + kk, :])
        acc = part if acc is None else acc + part
        k0 += kk
    o_ref[...] = acc.astype(o_ref.dtype)


def matmul(xs, w, *, layer=0, tm, tn, out_dtype=F32, name="matmul"):
    m = xs[0].shape[0]
    k, n = w.shape[-2:]
    assert sum(x.shape[1] for x in xs) == k
    cast = w.dtype != BF16
    wbytes = k * tn * (4 if cast else 2)
    nbytes = 2 * tm * k * 2 + 2 * wbytes + (k * tn * 2 if cast else 0) + 2 * tm * tn * 4
    return pl.pallas_call(
        functools.partial(_mm_kernel, n_lhs=len(xs), cast=cast),
        out_shape=jax.ShapeDtypeStruct((m, n), out_dtype),
        grid=(n // tn, m // tm),
        in_specs=[pl.BlockSpec((tm, x.shape[1]), lambda j, i: (i, 0)) for x in xs]
        + [_weight_spec(w, layer, k, tn, lambda j, i: j)],
        out_specs=pl.BlockSpec((tm, tn), lambda j, i: (i, j)),
        scratch_shapes=[pltpu.VMEM((k, tn), BF16)] if cast else [],
        compiler_params=_params(("arbitrary", "arbitrary"), nbytes),
        name=name,
    )(*xs, w)


def _tile_maps(ni, nj):
    nt = ni * nj

    def cur_row(t):
        return lax.rem(jnp.minimum(t, nt - 1), ni)

    def cur_col(t):
        return lax.div(jnp.minimum(t, nt - 1), ni)

    def prev_row(t):
        return lax.rem(jnp.maximum(t - 1, 0), ni)

    def prev_col(t):
        return lax.div(jnp.maximum(t - 1, 0), ni)

    return nt, cur_row, cur_col, prev_row, prev_col


def _interleaved_projections(x_ref, w_refs, dst_stores, epilogue_chunk, tm):
    nk = x_ref.shape[1] // K_CHUNK
    chunk_rows = list(range(0, tm, EPI_ROWS))
    slots, slot, done = len(w_refs) * nk, 0, 0
    for w_ref, store in zip(w_refs, dst_stores):
        acc = None
        for kk in range(nk):
            ks = slice(kk * K_CHUNK, (kk + 1) * K_CHUNK)
            part = _dot(x_ref[:, ks], w_ref[ks, :])
            acc = part if acc is None else acc + part
            slot += 1
            while done < len(chunk_rows) * slot // slots:
                epilogue_chunk(chunk_rows[done])
                done += 1
        store(acc)


def _causal_conv(buf_ref, cw_ref, cb_ref, width, r0, rows):
    win = buf_ref[r0:r0 + CARRY_ROWS + rows, :]
    conv = cb_ref[...]
    for tap in range(width):
        back = width - 1 - tap
        shifted = pltpu.roll(win, back, 0) if back else win
        conv = conv + shifted[CARRY_ROWS:, :] * cw_ref[tap:tap + 1, :]
    return conv


def _ffn_up_kernel(x_ref, wg_ref, wu_ref, cw_ref, cb_ref, wd_ref, o_ref, wdb_ref,
                   wgb_ref, wub_ref, g0_ref, g1_ref, u0_ref, u1_ref, *, tm, ni, nt):
    t = pl.program_id(0)
    i = lax.rem(t, ni)

    @pl.when(t == 0)
    def _():
        g1_ref[...] = jnp.zeros_like(g1_ref)
        u1_ref[...] = jnp.zeros_like(u1_ref)

    @pl.when(jnp.logical_and(i == 0, t < nt))
    def _():
        wgb_ref[...] = wg_ref[...].astype(BF16)
        wub_ref[...] = wu_ref[...].astype(BF16)
        wdb_ref[...] = wd_ref[...].astype(BF16)

    def step(g_cur, u_cur, g_prev, u_prev):
        g_cur[0:CARRY_ROWS, :] = jnp.where(i == 0, 0.0, g_prev[tm:tm + CARRY_ROWS, :])

        def epilogue_chunk(r0):
            conv = _causal_conv(g_prev, cw_ref, cb_ref, FFN_CONV, r0, EPI_ROWS)
            o_ref[r0:r0 + EPI_ROWS, :] = (jax.nn.gelu(conv, approximate=True)
                                          * u_prev[r0:r0 + EPI_ROWS, :]).astype(o_ref.dtype)

        def store_gate(acc):
            g_cur[CARRY_ROWS:CARRY_ROWS + tm, :] = acc

        def store_up(acc):
            u_cur[...] = acc

        _interleaved_projections(x_ref, (wgb_ref, wub_ref), (store_gate, store_up), epilogue_chunk, tm)

    @pl.when(lax.rem(t, 2) == 0)
    def _():
        step(g0_ref, u0_ref, g1_ref, u1_ref)

    @pl.when(lax.rem(t, 2) == 1)
    def _():
        step(g1_ref, u1_ref, g0_ref, u0_ref)


def ffn_up(xn, w_gu, conv_w, conv_b, w_down, layer):
    m, k = xn.shape
    tm, tf = TM_MM, TF_FFN
    ni, nj = m // tm, D_FF // tf
    nt, cur_row, cur_col, prev_row, prev_col = _tile_maps(ni, nj)
    nbytes = (2 * tm * k * 2 + 4 * k * tf * 4 + 2 * k * tf * 2 + 2 * tm * tf * 2
              + 4 * (tm + CARRY_ROWS) * tf * 4 + 2 * tf * D_MODEL * 6)
    return pl.pallas_call(
        functools.partial(_ffn_up_kernel, tm=tm, ni=ni, nt=nt),
        out_shape=(jax.ShapeDtypeStruct((m, D_FF), BF16), jax.ShapeDtypeStruct((D_FF, D_MODEL), BF16)),
        grid=(nt + 1,),
        in_specs=[pl.BlockSpec((tm, k), lambda t: (cur_row(t), 0)),
                  pl.BlockSpec((None, k, tf), lambda t: (layer, 0, cur_col(t))),
                  pl.BlockSpec((None, k, tf), lambda t: (layer, 0, cur_col(t) + nj)),
                  pl.BlockSpec((None, FFN_CONV, tf), lambda t: (layer, 0, prev_col(t))),
                  pl.BlockSpec((None, 1, tf), lambda t: (layer, 0, prev_col(t))),
                  pl.BlockSpec((None, tf, D_MODEL), lambda t: (layer, cur_col(t), 0))],
        out_specs=(pl.BlockSpec((tm, tf), lambda t: (prev_row(t), prev_col(t))),
                   pl.BlockSpec((tf, D_MODEL), lambda t: (cur_col(t), 0))),
        scratch_shapes=[pltpu.VMEM((k, tf), BF16), pltpu.VMEM((k, tf), BF16),
                        pltpu.VMEM((tm + CARRY_ROWS, tf), F32), pltpu.VMEM((tm + CARRY_ROWS, tf), F32),
                        pltpu.VMEM((tm, tf), F32), pltpu.VMEM((tm, tf), F32)],
        compiler_params=_params(("arbitrary",), nbytes),
        name="ffn_up",
    )(xn, w_gu, w_gu, conv_w, conv_b.reshape(conv_b.shape[0], 1, D_FF), w_down)


def _in_odd_kernel(x_ref, wy_ref, wx_ref, cw_ref, cb_ref, y_ref, xc_ref,
                   wyb_ref, wxb_ref, b0_ref, b1_ref, r0_ref, r1_ref, *, tm, ni, nt):
    t = pl.program_id(0)
    i = lax.rem(t, ni)

    @pl.when(t == 0)
    def _():
        b1_ref[...] = jnp.zeros_like(b1_ref)
        r1_ref[...] = jnp.zeros_like(r1_ref)

    @pl.when(jnp.logical_and(i == 0, t < nt))
    def _():
        wyb_ref[...] = wy_ref[...].astype(BF16)
        wxb_ref[...] = wx_ref[...].astype(BF16)

    def step(b_cur, r_cur, b_prev, r_prev):
        b_cur[0:CARRY_ROWS, :] = jnp.where(i == 0, 0.0, b_prev[tm:tm + CARRY_ROWS, :])

        def epilogue_chunk(r0):
            y_ref[r0:r0 + EPI_ROWS, :] = jax.nn.gelu(r_prev[r0:r0 + EPI_ROWS, :], approximate=True)
            xc_ref[r0:r0 + EPI_ROWS, :] = _causal_conv(b_prev, cw_ref, cb_ref, LRU_CONV, r0, EPI_ROWS)

        def store_y(acc):
            r_cur[...] = acc

        def store_x(acc):
            b_cur[CARRY_ROWS:CARRY_ROWS + tm, :] = acc

        _interleaved_projections(x_ref, (wyb_ref, wxb_ref), (store_y, store_x), epilogue_chunk, tm)

    @pl.when(lax.rem(t, 2) == 0)
    def _():
        step(b0_ref, r0_ref, b1_ref, r1_ref)

    @pl.when(lax.rem(t, 2) == 1)
    def _():
        step(b1_ref, r1_ref, b0_ref, r0_ref)


def in_odd(xn, w_in, conv_w, conv_b, layer):
    m, k = xn.shape
    tm, tn = TM_MM, TN_ODD
    ni, nj = m // tm, LRU_WIDTH // tn
    nt, cur_row, cur_col, prev_row, prev_col = _tile_maps(ni, nj)
    nbytes = (2 * tm * k * 2 + 4 * k * tn * 4 + 2 * k * tn * 2 + 4 * tm * tn * 4
              + 4 * (tm + CARRY_ROWS) * tn * 4)
    out = pl.BlockSpec((tm, tn), lambda t: (prev_row(t), prev_col(t)))
    return pl.pallas_call(
        functools.partial(_in_odd_kernel, tm=tm, ni=ni, nt=nt),
        out_shape=(jax.ShapeDtypeStruct((m, LRU_WIDTH), F32), jax.ShapeDtypeStruct((m, LRU_WIDTH), F32)),
        grid=(nt + 1,),
        in_specs=[pl.BlockSpec((tm, k), lambda t: (cur_row(t), 0)),
                  pl.BlockSpec((None, k, tn), lambda t: (layer, 0, cur_col(t))),
                  pl.BlockSpec((None, k, tn), lambda t: (layer, 0, cur_col(t) + nj)),
                  pl.BlockSpec((None, LRU_CONV, tn), lambda t: (layer, 0, prev_col(t))),
                  pl.BlockSpec((None, 1, tn), lambda t: (layer, 0, prev_col(t)))],
        out_specs=(out, out),
        scratch_shapes=[pltpu.VMEM((k, tn), BF16), pltpu.VMEM((k, tn), BF16),
                        pltpu.VMEM((tm + CARRY_ROWS, tn), F32), pltpu.VMEM((tm + CARRY_ROWS, tn), F32),
                        pltpu.VMEM((tm, tn), F32), pltpu.VMEM((tm, tn), F32)],
        compiler_params=_params(("arbitrary",), nbytes),
        name="in_odd",
    )(xn, w_in, w_in, conv_w, conv_b.reshape(conv_b.shape[0], 1, LRU_WIDTH))


def _lru_kernel(xc_ref, y_ref, wa_ref, wx_ref, ba_ref, bx_ref, ap_ref, o_ref, a_ref, b_ref, hc_ref, *, tr):
    step = pl.program_id(0)

    @pl.when(step == 0)
    def _():
        hc_ref[...] = jnp.zeros_like(hc_ref)

    valid = (step * tr + _iota((tr, 1), 0)) >= PAD
    for blk in range(LRU_BLOCKS):
        sl = slice(blk * LRU_BDIM, (blk + 1) * LRU_BDIM)
        x = xc_ref[:, sl]
        xb = x.astype(BF16)
        gate_r = 0.5 * jnp.tanh(0.5 * (_dot(xb, wa_ref[blk]) + ba_ref[:, sl])) + 0.5
        gate_i = 0.5 * jnp.tanh(0.5 * (_dot(xb, wx_ref[blk]) + bx_ref[:, sl])) + 0.5
        log_a = gate_r * (LRU_C * jax.nn.log_sigmoid(ap_ref[:, sl]))
        th = jnp.tanh(0.5 * log_a)
        em = 2.0 * th / (1.0 - th)
        a_ref[:, sl] = 1.0 + em
        inp = jnp.sqrt(-em * (2.0 + em)) * (gate_i * x)
        b_ref[:, sl] = jnp.where(valid, inp, 0.0)

    half = SCAN_ROWS // 2
    ridx = _iota((half, CW_LRU), 0)

    def local_scan(a, b):
        shift = 1
        while shift < half:
            ok = ridx >= shift
            b = jnp.where(ok, a * pltpu.roll(b, shift, 0) + b, b)
            a = jnp.where(ok, a * pltpu.roll(a, shift, 0), a)
            shift *= 2
        return a, b

    for c in range(LRU_WIDTH // CW_LRU):
        cs = slice(c * CW_LRU, (c + 1) * CW_LRU)

        def body(g, carry, cs=cs):
            r0 = pl.multiple_of(g * SCAN_ROWS, SCAN_ROWS)
            a = a_ref[pl.ds(r0, SCAN_ROWS), cs]
            b = b_ref[pl.ds(r0, SCAN_ROWS), cs]
            a_top, b_top = local_scan(a[:half], b[:half])
            a_bot, b_bot = local_scan(a[half:], b[half:])
            h_top = a_top * carry + b_top
            h_bot = a_bot * h_top[half - 1:half, :] + b_bot
            h = jnp.concatenate([h_top, h_bot], axis=0)
            o_ref[pl.ds(r0, SCAN_ROWS), cs] = (h * y_ref[pl.ds(r0, SCAN_ROWS), cs]).astype(o_ref.dtype)
            return h_bot[half - 1:half, :]

        hc_ref[0:1, cs] = lax.fori_loop(0, tr // SCAN_ROWS, body, hc_ref[0:1, cs])


def rglru(xc, y, w_a, b_a, w_x, b_x, a_param):
    m, d = xc.shape
    tr = TR_LRU
    row = pl.BlockSpec((tr, d), lambda i: (i, 0))
    vec = pl.BlockSpec((1, d), lambda i: (0, 0))
    wspec = pl.BlockSpec((LRU_BLOCKS, LRU_BDIM, LRU_BDIM), lambda i: (0, 0, 0))
    nbytes = 4 * tr * d * 4 + 2 * tr * d * 2 + 2 * tr * d * 4 + 4 * LRU_BLOCKS * LRU_BDIM * LRU_BDIM * 2
    return pl.pallas_call(
        functools.partial(_lru_kernel, tr=tr),
        out_shape=jax.ShapeDtypeStruct((m, d), BF16),
        grid=(m // tr,),
        in_specs=[row, row, wspec, wspec, vec, vec, vec],
        out_specs=row,
        scratch_shapes=[pltpu.VMEM((tr, d), F32), pltpu.VMEM((tr, d), F32),
                        pltpu.VMEM((V7X_SUBLANES, d), F32)],
        compiler_params=_params(("arbitrary",), nbytes),
        name="rglru",
    )(xc, y, w_a.astype(BF16), w_x.astype(BF16), b_a.reshape(1, d), b_x.reshape(1, d),
      a_param.reshape(1, d))


def _hgrn_kernel(q_ref, f_ref, i_ref, g_ref, lbl_ref, gnw_ref, o_ref, st_ref, *, tr, layer_j):
    step = pl.program_id(1)

    @pl.when(step == 0)
    def _():
        st_ref[...] = jnp.zeros_like(st_ref)

    logits = lbl_ref[...]
    e = jnp.exp(logits - jnp.max(logits, axis=0, keepdims=True))
    lb = jnp.sum(e[0:layer_j + 1], axis=0, keepdims=True) / jnp.sum(e, axis=0, keepdims=True)

    valid = (step * tr + _iota((tr, 1), 0)) >= PAD
    q = jax.nn.silu(q_ref[...])
    forget = lb + (1.0 - lb) * jax.nn.sigmoid(f_ref[...])
    k = jnp.where(valid, 1.0 - forget, 0.0)
    g = jnp.where(valid, jnp.log(forget), 0.0)
    v = i_ref[...]

    pos = _iota((tr, A_KDIM), 0) & (HGRN_CHUNK - 1)
    b = g
    shift = 1
    while shift < HGRN_CHUNK:
        b = b + jnp.where(pos >= shift, pltpu.roll(b, shift, 0), 0.0)
        shift *= 2

    n_sub = HGRN_CHUNK // HGRN_SUB
    cpos = _iota((HGRN_CHUNK, A_KDIM), 0)
    causal = _iota((HGRN_CHUNK, HGRN_CHUNK), 0) >= _iota((HGRN_CHUNK, HGRN_CHUNK), 1)
    gnw = gnw_ref[...]
    chunks = range(tr // HGRN_CHUNK)
    sl = [slice(c * HGRN_CHUNK, (c + 1) * HGRN_CHUNK) for c in chunks]
    bc = [b[s] for s in sl]
    qc = [q[s] for s in sl]
    kc = [k[s] for s in sl]
    vcb = [v[s].astype(BF16) for s in sl]
    b_last = [x[HGRN_CHUNK - 1:HGRN_CHUNK] for x in bc]
    u_t = [_dot_tn(vcb[c], (kc[c] * jnp.exp(b_last[c] - bc[c])).astype(BF16)) for c in chunks]
    att = []
    for c in chunks:
        rows = []
        for i in range(n_sub):
            ss = slice(i * HGRN_SUB, (i + 1) * HGRN_SUB)
            ref = jnp.zeros((1, A_KDIM), F32) if i == 0 else bc[c][i * HGRN_SUB - 1:i * HGRN_SUB]
            q_sc = (qc[c][ss] * jnp.exp(bc[c][ss] - ref)).astype(BF16)
            expo = jnp.where(cpos < (i + 1) * HGRN_SUB, ref - bc[c], 0.0)
            k_sc = (kc[c] * jnp.exp(expo)).astype(BF16)
            rows.append(_dot_nt(q_sc, k_sc))
        att.append(jnp.where(causal, jnp.concatenate(rows, axis=0), 0.0).astype(BF16))
    o_intra = [_dot(att[c], vcb[c]) for c in chunks]
    states = []
    st = st_ref[...]
    for c in chunks:
        states.append(st.astype(BF16))
        st = st * jnp.exp(b_last[c]) + u_t[c]
    st_ref[...] = st
    o_inter = [_dot_nt((qc[c] * jnp.exp(bc[c])).astype(BF16), states[c]) for c in chunks]
    for c in chunks:
        gate = jax.nn.silu(g_ref[sl[c], :])
        o_ref[sl[c], :] = (_rms(o_inter[c] + o_intra[c], gnw) * gate).astype(o_ref.dtype)


def hgrn2(hproj, lb_logits, gn_w, layer_j):
    m = hproj.shape[0]
    tr = TR_HGRN
    nrow = lb_logits.shape[0]

    def col(off):
        return pl.BlockSpec((tr, A_KDIM), lambda h, t, off=off: (t, h + off))

    nbytes = 2 * 4 * tr * A_KDIM * 4 + 2 * tr * A_VDIM * 2 + 12 * tr * A_KDIM * 4
    return pl.pallas_call(
        functools.partial(_hgrn_kernel, tr=tr, layer_j=layer_j),
        out_shape=jax.ShapeDtypeStruct((m, A_WIDTH), BF16),
        grid=(A_HEADS, m // tr),
        in_specs=[col(0), col(A_HEADS), col(2 * A_HEADS), col(3 * A_HEADS),
                  pl.BlockSpec((nrow, A_KDIM), lambda h, t: (0, h)),
                  pl.BlockSpec((1, A_VDIM), lambda h, t: (0, 0))],
        out_specs=pl.BlockSpec((tr, A_VDIM), lambda h, t: (t, h)),
        scratch_shapes=[pltpu.VMEM((A_VDIM, A_KDIM), F32)],
        compiler_params=_params(("arbitrary", "arbitrary"), nbytes),
        name="hgrn2",
    )(hproj, hproj, hproj, hproj, lb_logits, gn_w.reshape(1, A_VDIM))


PAIRS = B_GROUP // 2
QROWS = PAIRS * ATTN_BLOCK


def _swa_kernel(sink_ref, q_ref, kc_ref, kp_ref, km_ref, vc_ref, vp_ref, vm_ref, o_ref):
    n = pl.program_id(0)
    scale = B_HDIM ** -0.5
    lane_lo = _iota((1, V7X_LANES), 1) < B_HDIM
    tq = _iota((QROWS, ATTN_BLOCK), 0) & (ATTN_BLOCK - 1)
    sk = _iota((QROWS, ATTN_BLOCK), 1)
    masks = (sk > tq + jnp.where(n >= 2, 0, 2 * ATTN_BLOCK),
             sk <= tq + jnp.where(n >= 1, 0, -2 * ATTN_BLOCK),
             jnp.logical_and(sk >= PAD, sk <= tq + jnp.where(n >= 1, ATTN_BLOCK, 0)))

    def split(x, natural_lo, fill):
        rolled = pltpu.roll(x, B_HDIM, 1)
        lo_src, hi_src = (x, rolled) if natural_lo else (rolled, x)
        return (jnp.where(lane_lo, lo_src, fill).astype(BF16), jnp.where(lane_lo, fill, hi_src).astype(BF16))

    chains = [(h, par) for h in range(B_KVHEADS) for par in range(2)]
    keys, vals, qs = {}, {}, {}
    for h in range(B_KVHEADS):
        tile = slice((h // 2) * V7X_LANES, (h // 2 + 1) * V7X_LANES)
        nat = h % 2 == 0
        keys[h] = [split(r[:, tile], nat, 0.0) for r in (kp_ref, kc_ref, km_ref)]
        vals[h] = [split(r[:, tile], nat, 1.0) for r in (vp_ref, vc_ref, vm_ref)]
        qs[h] = (jnp.concatenate(
            [q_ref[:, (h * PAIRS + p) * V7X_LANES:(h * PAIRS + p + 1) * V7X_LANES] for p in range(PAIRS)],
            axis=0) * scale).astype(BF16)
    logits = {c: [jnp.where(m, _dot_nt(qs[c[0]], kk[c[1]]), NEG_INF) for m, kk in zip(masks, keys[c[0]])]
              for c in chains}
    sinks = {(h, par): jnp.concatenate(
        [jnp.full((ATTN_BLOCK, 1), sink_ref[h * B_GROUP + 2 * p + par], F32) for p in range(PAIRS)], axis=0)
        for h, par in chains}
    mx = {c: jnp.maximum(jnp.max(jnp.maximum(jnp.maximum(logits[c][0], logits[c][1]), logits[c][2]),
                                 axis=-1, keepdims=True), sinks[c]) for c in chains}
    pv = {}
    for c in chains:
        for lg, vv in zip(logits[c], vals[c[0]]):
            part = _dot(jnp.exp(lg - mx[c]).astype(BF16), vv[c[1]])
            pv[c] = part if c not in pv else pv[c] + part
    out = {}
    for c in chains:
        den = pltpu.roll(pv[c], B_HDIM, 1) + jnp.exp(sinks[c] - mx[c])
        own_half = lane_lo if c[1] == 0 else jnp.logical_not(lane_lo)
        part = jnp.where(own_half, pv[c] / den, 0.0)
        out[c[0]] = part if c[0] not in out else out[c[0]] + part
    for h in range(B_KVHEADS):
        for p in range(PAIRS):
            o_ref[:, (h * PAIRS + p) * V7X_LANES:(h * PAIRS + p + 1) * V7X_LANES] = (
                out[h][p * ATTN_BLOCK:(p + 1) * ATTN_BLOCK].astype(o_ref.dtype))


def swa(hproj, sinks):
    m = hproj.shape[0]
    blk = ATTN_BLOCK
    q_col = (2 * A_FDIM + 2 * A_WIDTH) // B_WIDTH
    k_col = (2 * A_FDIM + 2 * A_WIDTH + B_WIDTH) // B_KVWIDTH
    v_col = k_col + 1
    qspec = pl.BlockSpec((blk, B_WIDTH), lambda n: (n, q_col))

    def kv(col, which):
        if which == "cur":
            return pl.BlockSpec((blk, B_KVWIDTH), lambda n: (n, col))
        if which == "prev":
            return pl.BlockSpec((blk, B_KVWIDTH), lambda n: (jnp.maximum(n - 1, 0), col))
        return pl.BlockSpec((blk, B_KVWIDTH), lambda n: (0, col))

    nbytes = 2 * blk * B_WIDTH * 4 + 12 * blk * B_KVWIDTH * 4 + 2 * blk * B_WIDTH * 2 + 40 * QROWS * 128 * 4
    return pl.pallas_call(
        _swa_kernel,
        out_shape=jax.ShapeDtypeStruct((m, B_WIDTH), BF16),
        grid=(m // blk,),
        in_specs=[pl.BlockSpec(memory_space=pltpu.SMEM), qspec,
                  kv(k_col, "cur"), kv(k_col, "prev"), kv(k_col, "meta"),
                  kv(v_col, "cur"), kv(v_col, "prev"), kv(v_col, "meta")],
        out_specs=pl.BlockSpec((blk, B_WIDTH), lambda n: (n, 0)),
        compiler_params=_params(("arbitrary",), nbytes),
        name="swa",
    )(sinks, hproj, hproj, hproj, hproj, hproj, hproj, hproj)


def kernel(x, meta_tokens, norm_w, w_in_even, lb_logits, hgrn_gn_w, attn_sinks, w_out_even,
           w_in_odd, lru_conv_w, lru_conv_b, lru_wa, lru_ba, lru_wx, lru_bx, lru_a_param, w_out_odd,
           ffn_w_gu, ffn_conv_w, ffn_conv_b, ffn_w_down):
    assert x.shape == (1, SEQ, D_MODEL) and norm_w.shape[0] == DEPTH
    x2d = x[0]
    xn = rms_cast_stream(x2d, meta_tokens, norm_w[0, 0])
    h, out = None, None
    for layer in range(DEPTH):
        j = layer // 2
        if layer % 2 == 0:
            hproj = matmul([xn], w_in_even, layer=j, tm=TM_IN_EVEN, tn=TN_IN_EVEN, name="in_even")
            o_a = hgrn2(hproj, lb_logits, hgrn_gn_w[j], j)
            o_b = swa(hproj, attn_sinks[j])
            mix = matmul([o_a, o_b], w_out_even, layer=j, tm=TM_OUT, tn=TN_OUT, name="out_even")
        else:
            y_br, x_br = in_odd(xn, w_in_odd, lru_conv_w, lru_conv_b, j)
            rec = rglru(x_br, y_br, lru_wa[j], lru_ba[j], lru_wx[j], lru_bx[j], lru_a_param[j])
            mix = matmul([rec], w_out_odd, layer=j, tm=TM_OUT, tn=TN_OUT, name="out_odd")
        if h is None:
            h, xn = resid_norm_stream(x2d, meta_tokens, mix, norm_w[layer, 1], norm_w[layer, 2])
        else:
            h, xn = resid_norm(h, mix, norm_w[layer, 1], norm_w[layer, 2])
        act, w_down_bf16 = ffn_up(xn, ffn_w_gu, ffn_conv_w, ffn_conv_b, ffn_w_down, layer)
        ff = matmul([act], w_down_bf16, tm=TM_DOWN, tn=TN_DOWN, name="ffn_down")
        if layer + 1 < DEPTH:
            h, xn = resid_norm(h, ff, norm_w[layer, 3], norm_w[layer + 1, 0])
        else:
            out = resid_final(h, ff, norm_w[layer, 3])
    return out[None]
```

```python
import functools

import jax
import jax.numpy as jnp
from jax import lax
from jax.experimental import pallas as pl
from jax.experimental.pallas import tpu as pltpu

F32 = jnp.float32
BF16 = jnp.bfloat16

D_MODEL = 4096
SEQ = 8192
DEPTH = 2
N_META = 16
A_HEADS = 16
A_KDIM = 128
A_VDIM = D_MODEL // 2 // A_HEADS
A_FDIM = A_HEADS * A_KDIM
A_WIDTH = A_HEADS * A_VDIM
HGRN_CHUNK = 64
HGRN_SUB = 16
B_HDIM = 64
B_QHEADS = D_MODEL // 2 // B_HDIM
B_KVHEADS = B_QHEADS // 8
B_GROUP = B_QHEADS // B_KVHEADS
B_WIDTH = B_QHEADS * B_HDIM
B_KVWIDTH = B_KVHEADS * B_HDIM
WINDOW = 128
ATTN_BLOCK = 128
EVEN_IN = 2 * A_FDIM + 2 * A_WIDTH + B_WIDTH + 2 * B_KVWIDTH
LRU_WIDTH = D_MODEL
LRU_BLOCKS = 16
LRU_BDIM = LRU_WIDTH // LRU_BLOCKS
LRU_CONV = 4
LRU_C = 8.0
D_FF = 256 * ((8 * D_MODEL // 3 + 255) // 256)
FFN_CONV = 3
NORM_EPS = 1e-6
NEG_INF = -1e30

PAD = ATTN_BLOCK - N_META
ROW0 = PAD + N_META
LP = ROW0 + SEQ
assert PAD % HGRN_CHUNK == HGRN_CHUNK - N_META and ROW0 == ATTN_BLOCK

V7X_LANES = 128
V7X_SUBLANES = 8
V7X_VMEM_LIMIT_CAP = 60 * 1024 * 1024
CARRY_ROWS = V7X_SUBLANES

TM_MM = 1040
TM_IN_EVEN = 1040
TN_IN_EVEN = 512
TM_OUT = 520
TN_OUT = 1024
TM_DOWN = 520
TN_DOWN = 512
TF_FFN = 256
TN_ODD = 256
EPI_ROWS = 80
K_CHUNK = 256
TR_NORM = 320
TR_LRU = 208
CW_LRU = 1024
TR_HGRN = 640
SCAN_ROWS = 16


def _vmem_limit(nbytes):
    return int(min(V7X_VMEM_LIMIT_CAP, nbytes * 1.15 + (4 << 20)))


def _params(sem, nbytes):
    return pltpu.CompilerParams(dimension_semantics=sem, vmem_limit_bytes=_vmem_limit(nbytes))


def _rms(x, w):
    return x * lax.rsqrt(jnp.mean(x * x, axis=-1, keepdims=True) + NORM_EPS) * w


def _iota(shape, dim):
    return lax.broadcasted_iota(jnp.int32, shape, dim)


def _dot(a, b):
    return jnp.dot(a, b, preferred_element_type=F32)


def _dot_nt(a, b):
    return lax.dot_general(a, b, (((1,), (1,)), ((), ())), preferred_element_type=F32)


def _dot_tn(a, b):
    return lax.dot_general(a, b, (((0,), (0,)), ((), ())), preferred_element_type=F32)


def _stream_rows(n, x_ref, meta_ref):
    first = jnp.concatenate([jnp.zeros((PAD, D_MODEL), F32), meta_ref[...].astype(F32)], axis=0)
    return jnp.where(n == 0, first, x_ref[...].astype(F32))


def _stream_specs():
    blk = ATTN_BLOCK
    return [pl.BlockSpec((blk, D_MODEL), lambda n: (jnp.maximum(n - 1, 0), 0)),
            pl.BlockSpec((N_META, D_MODEL), lambda n: (0, 0))]


def _rms_cast_stream_kernel(x_ref, meta_ref, w_ref, o_ref):
    h = _stream_rows(pl.program_id(0), x_ref, meta_ref)
    o_ref[...] = _rms(h, w_ref[...]).astype(o_ref.dtype)


def rms_cast_stream(x2d, meta, w):
    blk, d = ATTN_BLOCK, D_MODEL
    return pl.pallas_call(
        _rms_cast_stream_kernel,
        out_shape=jax.ShapeDtypeStruct((LP, d), BF16),
        grid=(LP // blk,),
        in_specs=_stream_specs() + [pl.BlockSpec((1, d), lambda n: (0, 0))],
        out_specs=pl.BlockSpec((blk, d), lambda n: (n, 0)),
        compiler_params=_params(("arbitrary",), 2 * blk * d * 6 + 4 * blk * d * 4),
        name="rms_cast_stream",
    )(x2d, meta, w.reshape(1, d))


def _resid_norm_stream_kernel(x_ref, meta_ref, y_ref, wp_ref, wn_ref, ho_ref, xo_ref):
    h = _stream_rows(pl.program_id(0), x_ref, meta_ref) + _rms(y_ref[...], wp_ref[...])
    ho_ref[...] = h
    xo_ref[...] = _rms(h, wn_ref[...]).astype(xo_ref.dtype)


def resid_norm_stream(x2d, meta, y, w_post, w_next):
    blk, d = ATTN_BLOCK, D_MODEL
    row = pl.BlockSpec((blk, d), lambda n: (n, 0))
    vec = pl.BlockSpec((1, d), lambda n: (0, 0))
    return pl.pallas_call(
        _resid_norm_stream_kernel,
        out_shape=(jax.ShapeDtypeStruct((LP, d), F32), jax.ShapeDtypeStruct((LP, d), BF16)),
        grid=(LP // blk,),
        in_specs=_stream_specs() + [row, vec, vec],
        out_specs=(row, row),
        compiler_params=_params(("arbitrary",), 2 * blk * d * 14 + 4 * blk * d * 4),
        name="resid_norm_stream",
    )(x2d, meta, y, w_post.reshape(1, d), w_next.reshape(1, d))


def _resid_norm_kernel(h_ref, y_ref, wp_ref, wn_ref, ho_ref, xo_ref):
    h = h_ref[...] + _rms(y_ref[...], wp_ref[...])
    ho_ref[...] = h
    xo_ref[...] = _rms(h, wn_ref[...]).astype(xo_ref.dtype)


def resid_norm(h, y, w_post, w_next):
    n, d = h.shape
    tr = TR_NORM
    row = pl.BlockSpec((tr, d), lambda i: (i, 0))
    vec = pl.BlockSpec((1, d), lambda i: (0, 0))
    return pl.pallas_call(
        _resid_norm_kernel,
        out_shape=(jax.ShapeDtypeStruct((n, d), F32), jax.ShapeDtypeStruct((n, d), BF16)),
        grid=(n // tr,),
        in_specs=[row, row, vec, vec],
        out_specs=(row, row),
        compiler_params=_params(("arbitrary",), 2 * tr * d * 14),
        name="resid_norm",
    )(h, y, w_post.reshape(1, d), w_next.reshape(1, d))


def _resid_final_kernel(h_ref, y_ref, wp_ref, o_ref):
    o_ref[...] = h_ref[...] + _rms(y_ref[...], wp_ref[...])


def resid_final(h, y, w_post):
    n, d = h.shape
    tr = ATTN_BLOCK
    off = ROW0 // tr
    src = pl.BlockSpec((tr, d), lambda i: (i + off, 0))
    return pl.pallas_call(
        _resid_final_kernel,
        out_shape=jax.ShapeDtypeStruct((n - ROW0, d), F32),
        grid=((n - ROW0) // tr,),
        in_specs=[src, src, pl.BlockSpec((1, d), lambda i: (0, 0))],
        out_specs=pl.BlockSpec((tr, d), lambda i: (i, 0)),
        compiler_params=_params(("arbitrary",), 2 * tr * d * 12),
        name="resid_final",
    )(h, y, w_post.reshape(1, d))


def _weight_spec(w, layer, k, tn, col_of):
    if w.ndim == 3:
        return pl.BlockSpec((None, k, tn), lambda *g: (layer, 0, col_of(*g)))
    return pl.BlockSpec((k, tn), lambda *g: (0, col_of(*g)))


def _mm_kernel(*refs, n_lhs, cast):
    x_refs, w_ref, o_ref = refs[:n_lhs], refs[n_lhs], refs[n_lhs + 1]
    if cast:
        wb_ref = refs[n_lhs + 2]

        @pl.when(pl.program_id(1) == 0)
        def _():
            wb_ref[...] = w_ref[...].astype(BF16)

        w_ref = wb_ref
    acc, k0 = None, 0
    for x_ref in x_refs:
        kk = x_ref.shape[1]
        part = _dot(x_ref[...], w_ref[k0:k0 + kk, :])
        acc = part if acc is None else acc + part
        k0 += kk
    o_ref[...] = acc.astype(o_ref.dtype)


def matmul(xs, w, *, layer=0, tm, tn, out_dtype=F32, name="matmul"):
    m = xs[0].shape[0]
    k, n = w.shape[-2:]
    assert sum(x.shape[1] for x in xs) == k
    cast = w.dtype != BF16
    wbytes = k * tn * (4 if cast else 2)
    nbytes = 2 * tm * k * 2 + 2 * wbytes + (k * tn * 2 if cast else 0) + 2 * tm * tn * 4
    return pl.pallas_call(
        functools.partial(_mm_kernel, n_lhs=len(xs), cast=cast),
        out_shape=jax.ShapeDtypeStruct((m, n), out_dtype),
        grid=(n // tn, m // tm),
        in_specs=[pl.BlockSpec((tm, x.shape[1]), lambda j, i: (i, 0)) for x in xs]
        + [_weight_spec(w, layer, k, tn, lambda j, i: j)],
        out_specs=pl.BlockSpec((tm, tn), lambda j, i: (i, j)),
        scratch_shapes=[pltpu.VMEM((k, tn), BF16)] if cast else [],
        compiler_params=_params(("arbitrary", "arbitrary"), nbytes),
        name=name,
    )(*xs, w)


def _tile_maps(ni, nj):
    nt = ni * nj

    def cur_row(t):
        return lax.rem(jnp.minimum(t, nt - 1), ni)

    def cur_col(t):
        return lax.div(jnp.minimum(t, nt - 1), ni)

    def prev_row(t):
        return lax.rem(jnp.maximum(t - 1, 0), ni)

    def prev_col(t):
        return lax.div(jnp.maximum(t - 1, 0), ni)

    return nt, cur_row, cur_col, prev_row, prev_col


def _interleaved_projections(x_ref, w_refs, dst_stores, epilogue_chunk, tm):
    nk = x_ref.shape[1] // K_CHUNK
    chunk_rows = list(range(0, tm, EPI_ROWS))
    slots, slot, done = len(w_refs) * nk, 0, 0
    for w_ref, store in zip(w_refs, dst_stores):
        acc = None
        for kk in range(nk):
            ks = slice(kk * K_CHUNK, (kk + 1) * K_CHUNK)
            part = _dot(x_ref[:, ks], w_ref[ks, :])
            acc = part if acc is None else acc + part
            slot += 1
            while done < len(chunk_rows) * slot // slots:
                epilogue_chunk(chunk_rows[done])
                done += 1
        store(acc)


def _causal_conv(buf_ref, cw_ref, cb_ref, width, r0, rows):
    win = buf_ref[r0:r0 + CARRY_ROWS + rows, :]
    conv = cb_ref[...]
    for tap in range(width):
        back = width - 1 - tap
        shifted = pltpu.roll(win, back, 0) if back else win
        conv = conv + shifted[CARRY_ROWS:, :] * cw_ref[tap:tap + 1, :]
    return conv


def _ffn_up_kernel(x_ref, wg_ref, wu_ref, cw_ref, cb_ref, wd_ref, o_ref, wdb_ref,
                   wgb_ref, wub_ref, g0_ref, g1_ref, u0_ref, u1_ref, *, tm, ni, nt):
    t = pl.program_id(0)
    i = lax.rem(t, ni)

    @pl.when(t == 0)
    def _():
        g1_ref[...] = jnp.zeros_like(g1_ref)
        u1_ref[...] = jnp.zeros_like(u1_ref)

    @pl.when(jnp.logical_and(i == 0, t < nt))
    def _():
        wgb_ref[...] = wg_ref[...].astype(BF16)
        wub_ref[...] = wu_ref[...].astype(BF16)

    wdb_ref[...] = wd_ref[...].astype(BF16)

    def step(g_cur, u_cur, g_prev, u_prev):
        g_cur[0:CARRY_ROWS, :] = jnp.where(i == 0, 0.0, g_prev[tm:tm + CARRY_ROWS, :])

        def epilogue_chunk(r0):
            conv = _causal_conv(g_prev, cw_ref, cb_ref, FFN_CONV, r0, EPI_ROWS)
            o_ref[r0:r0 + EPI_ROWS, :] = (jax.nn.gelu(conv, approximate=True)
                                          * u_prev[r0:r0 + EPI_ROWS, :]).astype(o_ref.dtype)

        def store_gate(acc):
            g_cur[CARRY_ROWS:CARRY_ROWS + tm, :] = acc

        def store_up(acc):
            u_cur[...] = acc

        _interleaved_projections(x_ref, (wgb_ref, wub_ref), (store_gate, store_up), epilogue_chunk, tm)

    @pl.when(lax.rem(t, 2) == 0)
    def _():
        step(g0_ref, u0_ref, g1_ref, u1_ref)

    @pl.when(lax.rem(t, 2) == 1)
    def _():
        step(g1_ref, u1_ref, g0_ref, u0_ref)


def ffn_up(xn, w_gu, conv_w, conv_b, w_down, layer):
    m, k = xn.shape
    tm, tf = TM_MM, TF_FFN
    ni, nj = m // tm, D_FF // tf
    nt, cur_row, cur_col, prev_row, prev_col = _tile_maps(ni, nj)
    wd_rows = D_FF // nt
    assert wd_rows * nt == D_FF and wd_rows % (2 * V7X_SUBLANES) == 0
    nbytes = (2 * tm * k * 2 + 4 * k * tf * 4 + 2 * k * tf * 2 + 2 * tm * tf * 2
              + 4 * (tm + CARRY_ROWS) * tf * 4 + 2 * wd_rows * D_MODEL * 6)
    return pl.pallas_call(
        functools.partial(_ffn_up_kernel, tm=tm, ni=ni, nt=nt),
        out_shape=(jax.ShapeDtypeStruct((m, D_FF), BF16), jax.ShapeDtypeStruct((D_FF, D_MODEL), BF16)),
        grid=(nt + 1,),
        in_specs=[pl.BlockSpec((tm, k), lambda t: (cur_row(t), 0)),
                  pl.BlockSpec((None, k, tf), lambda t: (layer, 0, cur_col(t))),
                  pl.BlockSpec((None, k, tf), lambda t: (layer, 0, cur_col(t) + nj)),
                  pl.BlockSpec((None, FFN_CONV, tf), lambda t: (layer, 0, prev_col(t))),
                  pl.BlockSpec((None, 1, tf), lambda t: (layer, 0, prev_col(t))),
                  pl.BlockSpec((None, wd_rows, D_MODEL), lambda t: (layer, jnp.minimum(t, nt - 1), 0))],
        out_specs=(pl.BlockSpec((tm, tf), lambda t: (prev_row(t), prev_col(t))),
                   pl.BlockSpec((wd_rows, D_MODEL), lambda t: (jnp.minimum(t, nt - 1), 0))),
        scratch_shapes=[pltpu.VMEM((k, tf), BF16), pltpu.VMEM((k, tf), BF16),
                        pltpu.VMEM((tm + CARRY_ROWS, tf), F32), pltpu.VMEM((tm + CARRY_ROWS, tf), F32),
                        pltpu.VMEM((tm, tf), F32), pltpu.VMEM((tm, tf), F32)],
        compiler_params=_params(("arbitrary",), nbytes),
        name="ffn_up",
    )(xn, w_gu, w_gu, conv_w, conv_b.reshape(conv_b.shape[0], 1, D_FF), w_down)


def _in_odd_kernel(x_ref, wy_ref, wx_ref, cw_ref, cb_ref, y_ref, xc_ref,
                   wyb_ref, wxb_ref, b0_ref, b1_ref, r0_ref, r1_ref, *, tm, ni, nt):
    t = pl.program_id(0)
    i = lax.rem(t, ni)

    @pl.when(t == 0)
    def _():
        b1_ref[...] = jnp.zeros_like(b1_ref)
        r1_ref[...] = jnp.zeros_like(r1_ref)

    @pl.when(jnp.logical_and(i == 0, t < nt))
    def _():
        wyb_ref[...] = wy_ref[...].astype(BF16)
        wxb_ref[...] = wx_ref[...].astype(BF16)

    def step(b_cur, r_cur, b_prev, r_prev):
        b_cur[0:CARRY_ROWS, :] = jnp.where(i == 0, 0.0, b_prev[tm:tm + CARRY_ROWS, :])

        def epilogue_chunk(r0):
            y_ref[r0:r0 + EPI_ROWS, :] = jax.nn.gelu(r_prev[r0:r0 + EPI_ROWS, :], approximate=True)
            xc_ref[r0:r0 + EPI_ROWS, :] = _causal_conv(b_prev, cw_ref, cb_ref, LRU_CONV, r0, EPI_ROWS)

        def store_y(acc):
            r_cur[...] = acc

        def store_x(acc):
            b_cur[CARRY_ROWS:CARRY_ROWS + tm, :] = acc

        _interleaved_projections(x_ref, (wyb_ref, wxb_ref), (store_y, store_x), epilogue_chunk, tm)

    @pl.when(lax.rem(t, 2) == 0)
    def _():
        step(b0_ref, r0_ref, b1_ref, r1_ref)

    @pl.when(lax.rem(t, 2) == 1)
    def _():
        step(b1_ref, r1_ref, b0_ref, r0_ref)


def in_odd(xn, w_in, conv_w, conv_b, layer):
    m, k = xn.shape
    tm, tn = TM_MM, TN_ODD
    ni, nj = m // tm, LRU_WIDTH // tn
    nt, cur_row, cur_col, prev_row, prev_col = _tile_maps(ni, nj)
    nbytes = (2 * tm * k * 2 + 4 * k * tn * 4 + 2 * k * tn * 2 + 4 * tm * tn * 4
              + 4 * (tm + CARRY_ROWS) * tn * 4)
    out = pl.BlockSpec((tm, tn), lambda t: (prev_row(t), prev_col(t)))
    return pl.pallas_call(
        functools.partial(_in_odd_kernel, tm=tm, ni=ni, nt=nt),
        out_shape=(jax.ShapeDtypeStruct((m, LRU_WIDTH), F32), jax.ShapeDtypeStruct((m, LRU_WIDTH), F32)),
        grid=(nt + 1,),
        in_specs=[pl.BlockSpec((tm, k), lambda t: (cur_row(t), 0)),
                  pl.BlockSpec((None, k, tn), lambda t: (layer, 0, cur_col(t))),
                  pl.BlockSpec((None, k, tn), lambda t: (layer, 0, cur_col(t) + nj)),
                  pl.BlockSpec((None, LRU_CONV, tn), lambda t: (layer, 0, prev_col(t))),
                  pl.BlockSpec((None, 1, tn), lambda t: (layer, 0, prev_col(t)))],
        out_specs=(out, out),
        scratch_shapes=[pltpu.VMEM((k, tn), BF16), pltpu.VMEM((k, tn), BF16),
                        pltpu.VMEM((tm + CARRY_ROWS, tn), F32), pltpu.VMEM((tm + CARRY_ROWS, tn), F32),
                        pltpu.VMEM((tm, tn), F32), pltpu.VMEM((tm, tn), F32)],
        compiler_params=_params(("arbitrary",), nbytes),
        name="in_odd",
    )(xn, w_in, w_in, conv_w, conv_b.reshape(conv_b.shape[0], 1, LRU_WIDTH))


def _lru_kernel(xc_ref, y_ref, wa_ref, wx_ref, ba_ref, bx_ref, ap_ref, o_ref, a_ref, b_ref, hc_ref, *, tr):
    step = pl.program_id(0)

    @pl.when(step == 0)
    def _():
        hc_ref[...] = jnp.zeros_like(hc_ref)

    valid = (step * tr + _iota((tr, 1), 0)) >= PAD
    for blk in range(LRU_BLOCKS):
        sl = slice(blk * LRU_BDIM, (blk + 1) * LRU_BDIM)
        x = xc_ref[:, sl]
        xb = x.astype(BF16)
        gate_r = 0.5 * jnp.tanh(0.5 * (_dot(xb, wa_ref[blk]) + ba_ref[:, sl])) + 0.5
        gate_i = 0.5 * jnp.tanh(0.5 * (_dot(xb, wx_ref[blk]) + bx_ref[:, sl])) + 0.5
        log_a = gate_r * (LRU_C * jax.nn.log_sigmoid(ap_ref[:, sl]))
        th = jnp.tanh(0.5 * log_a)
        em = 2.0 * th / (1.0 - th)
        a_ref[:, sl] = 1.0 + em
        inp = jnp.sqrt(-em * (2.0 + em)) * (gate_i * x)
        b_ref[:, sl] = jnp.where(valid, inp, 0.0)

    half = SCAN_ROWS // 2
    ridx = _iota((half, CW_LRU), 0)

    def local_scan(a, b):
        shift = 1
        while shift < half:
            ok = ridx >= shift
            b = jnp.where(ok, a * pltpu.roll(b, shift, 0) + b, b)
            a = jnp.where(ok, a * pltpu.roll(a, shift, 0), a)
            shift *= 2
        return a, b

    for c in range(LRU_WIDTH // CW_LRU):
        cs = slice(c * CW_LRU, (c + 1) * CW_LRU)

        def body(g, carry, cs=cs):
            r0 = pl.multiple_of(g * SCAN_ROWS, SCAN_ROWS)
            a = a_ref[pl.ds(r0, SCAN_ROWS), cs]
            b = b_ref[pl.ds(r0, SCAN_ROWS), cs]
            a_top, b_top = local_scan(a[:half], b[:half])
            a_bot, b_bot = local_scan(a[half:], b[half:])
            h_top = a_top * carry + b_top
            h_bot = a_bot * h_top[half - 1:half, :] + b_bot
            h = jnp.concatenate([h_top, h_bot], axis=0)
            o_ref[pl.ds(r0, SCAN_ROWS), cs] = (h * y_ref[pl.ds(r0, SCAN_ROWS), cs]).astype(o_ref.dtype)
            return h_bot[half - 1:half, :]

        hc_ref[0:1, cs] = lax.fori_loop(0, tr // SCAN_ROWS, body, hc_ref[0:1, cs])


def rglru(xc, y, w_a, b_a, w_x, b_x, a_param):
    m, d = xc.shape
    tr = TR_LRU
    row = pl.BlockSpec((tr, d), lambda i: (i, 0))
    vec = pl.BlockSpec((1, d), lambda i: (0, 0))
    wspec = pl.BlockSpec((LRU_BLOCKS, LRU_BDIM, LRU_BDIM), lambda i: (0, 0, 0))
    nbytes = 4 * tr * d * 4 + 2 * tr * d * 2 + 2 * tr * d * 4 + 4 * LRU_BLOCKS * LRU_BDIM * LRU_BDIM * 2
    return pl.pallas_call(
        functools.partial(_lru_kernel, tr=tr),
        out_shape=jax.ShapeDtypeStruct((m, d), BF16),
        grid=(m // tr,),
        in_specs=[row, row, wspec, wspec, vec, vec, vec],
        out_specs=row,
        scratch_shapes=[pltpu.VMEM((tr, d), F32), pltpu.VMEM((tr, d), F32),
                        pltpu.VMEM((V7X_SUBLANES, d), F32)],
        compiler_params=_params(("arbitrary",), nbytes),
        name="rglru",
    )(xc, y, w_a.astype(BF16), w_x.astype(BF16), b_a.reshape(1, d), b_x.reshape(1, d),
      a_param.reshape(1, d))


def _hgrn_kernel(q_ref, f_ref, i_ref, g_ref, lbl_ref, gnw_ref, o_ref, st_ref, *, tr, layer_j):
    step = pl.program_id(1)

    @pl.when(step == 0)
    def _():
        st_ref[...] = jnp.zeros_like(st_ref)

    logits = lbl_ref[...]
    e = jnp.exp(logits - jnp.max(logits, axis=0, keepdims=True))
    lb = jnp.sum(e[0:layer_j + 1], axis=0, keepdims=True) / jnp.sum(e, axis=0, keepdims=True)

    valid = (step * tr + _iota((tr, 1), 0)) >= PAD
    q = jax.nn.silu(q_ref[...])
    forget = lb + (1.0 - lb) * jax.nn.sigmoid(f_ref[...])
    k = jnp.where(valid, 1.0 - forget, 0.0)
    g = jnp.where(valid, jnp.log(forget), 0.0)
    v = i_ref[...]

    pos = _iota((tr, A_KDIM), 0) & (HGRN_CHUNK - 1)
    b = g
    shift = 1
    while shift < HGRN_CHUNK:
        b = b + jnp.where(pos >= shift, pltpu.roll(b, shift, 0), 0.0)
        shift *= 2

    n_sub = HGRN_CHUNK // HGRN_SUB
    cpos = _iota((HGRN_CHUNK, A_KDIM), 0)
    causal = _iota((HGRN_CHUNK, HGRN_CHUNK), 0) >= _iota((HGRN_CHUNK, HGRN_CHUNK), 1)
    gnw = gnw_ref[...]
    chunks = range(tr // HGRN_CHUNK)
    sl = [slice(c * HGRN_CHUNK, (c + 1) * HGRN_CHUNK) for c in chunks]
    bc = [b[s] for s in sl]
    qc = [q[s] for s in sl]
    kc = [k[s] for s in sl]
    vcb = [v[s].astype(BF16) for s in sl]
    b_last = [x[HGRN_CHUNK - 1:HGRN_CHUNK] for x in bc]
    u_t = [_dot_tn(vcb[c], (kc[c] * jnp.exp(b_last[c] - bc[c])).astype(BF16)) for c in chunks]
    att = []
    for c in chunks:
        rows = []
        for i in range(n_sub):
            ss = slice(i * HGRN_SUB, (i + 1) * HGRN_SUB)
            ref = jnp.zeros((1, A_KDIM), F32) if i == 0 else bc[c][i * HGRN_SUB - 1:i * HGRN_SUB]
            q_sc = (qc[c][ss] * jnp.exp(bc[c][ss] - ref)).astype(BF16)
            expo = jnp.where(cpos < (i + 1) * HGRN_SUB, ref - bc[c], 0.0)
            k_sc = (kc[c] * jnp.exp(expo)).astype(BF16)
            rows.append(_dot_nt(q_sc, k_sc))
        att.append(jnp.where(causal, jnp.concatenate(rows, axis=0), 0.0).astype(BF16))
    o_intra = [_dot(att[c], vcb[c]) for c in chunks]
    states = []
    st = st_ref[...]
    for c in chunks:
        states.append(st.astype(BF16))
        st = st * jnp.exp(b_last[c]) + u_t[c]
    st_ref[...] = st
    o_inter = [_dot_nt((qc[c] * jnp.exp(bc[c])).astype(BF16), states[c]) for c in chunks]
    for c in chunks:
        gate = jax.nn.silu(g_ref[sl[c], :])
        o_ref[sl[c], :] = (_rms(o_inter[c] + o_intra[c], gnw) * gate).astype(o_ref.dtype)


def hgrn2(hproj, lb_logits, gn_w, layer_j):
    m = hproj.shape[0]
    tr = TR_HGRN
    nrow = lb_logits.shape[0]

    def col(off):
        return pl.BlockSpec((tr, A_KDIM), lambda h, t, off=off: (t, h + off))

    nbytes = 2 * 4 * tr * A_KDIM * 4 + 2 * tr * A_VDIM * 2 + 12 * tr * A_KDIM * 4
    return pl.pallas_call(
        functools.partial(_hgrn_kernel, tr=tr, layer_j=layer_j),
        out_shape=jax.ShapeDtypeStruct((m, A_WIDTH), BF16),
        grid=(A_HEADS, m // tr),
        in_specs=[col(0), col(A_HEADS), col(2 * A_HEADS), col(3 * A_HEADS),
                  pl.BlockSpec((nrow, A_KDIM), lambda h, t: (0, h)),
                  pl.BlockSpec((1, A_VDIM), lambda h, t: (0, 0))],
        out_specs=pl.BlockSpec((tr, A_VDIM), lambda h, t: (t, h)),
        scratch_shapes=[pltpu.VMEM((A_VDIM, A_KDIM), F32)],
        compiler_params=_params(("arbitrary", "arbitrary"), nbytes),
        name="hgrn2",
    )(hproj, hproj, hproj, hproj, lb_logits, gn_w.reshape(1, A_VDIM))


PAIRS = B_GROUP // 2
QROWS = PAIRS * ATTN_BLOCK


def _swa_kernel(sink_ref, q_ref, kc_ref, kp_ref, km_ref, vc_ref, vp_ref, vm_ref, o_ref):
    n = pl.program_id(0)
    scale = B_HDIM ** -0.5
    lane_lo = _iota((1, V7X_LANES), 1) < B_HDIM
    tq = _iota((QROWS, ATTN_BLOCK), 0) & (ATTN_BLOCK - 1)
    sk = _iota((QROWS, ATTN_BLOCK), 1)
    masks = (sk > tq + jnp.where(n >= 2, 0, 2 * ATTN_BLOCK),
             sk <= tq + jnp.where(n >= 1, 0, -2 * ATTN_BLOCK),
             jnp.logical_and(sk >= PAD, sk <= tq + jnp.where(n >= 1, ATTN_BLOCK, 0)))

    def split(x, natural_lo, fill):
        rolled = pltpu.roll(x, B_HDIM, 1)
        lo_src, hi_src = (x, rolled) if natural_lo else (rolled, x)
        return (jnp.where(lane_lo, lo_src, fill).astype(BF16), jnp.where(lane_lo, fill, hi_src).astype(BF16))

    chains = [(h, par) for h in range(B_KVHEADS) for par in range(2)]
    keys, vals, qs = {}, {}, {}
    for h in range(B_KVHEADS):
        tile = slice((h // 2) * V7X_LANES, (h // 2 + 1) * V7X_LANES)
        nat = h % 2 == 0
        keys[h] = [split(r[:, tile], nat, 0.0) for r in (kp_ref, kc_ref, km_ref)]
        vals[h] = [split(r[:, tile], nat, 1.0) for r in (vp_ref, vc_ref, vm_ref)]
        qs[h] = (jnp.concatenate(
            [q_ref[:, (h * PAIRS + p) * V7X_LANES:(h * PAIRS + p + 1) * V7X_LANES] for p in range(PAIRS)],
            axis=0) * scale).astype(BF16)
    logits = {c: [jnp.where(m, _dot_nt(qs[c[0]], kk[c[1]]), NEG_INF) for m, kk in zip(masks, keys[c[0]])]
              for c in chains}
    sinks = {(h, par): jnp.concatenate(
        [jnp.full((ATTN_BLOCK, 1), sink_ref[h * B_GROUP + 2 * p + par], F32) for p in range(PAIRS)], axis=0)
        for h, par in chains}
    mx = {c: jnp.maximum(jnp.max(jnp.maximum(jnp.maximum(logits[c][0], logits[c][1]), logits[c][2]),
                                 axis=-1, keepdims=True), sinks[c]) for c in chains}
    pv = {}
    for c in chains:
        for lg, vv in zip(logits[c], vals[c[0]]):
            part = _dot(jnp.exp(lg - mx[c]).astype(BF16), vv[c[1]])
            pv[c] = part if c not in pv else pv[c] + part
    out = {}
    for c in chains:
        den = pltpu.roll(pv[c], B_HDIM, 1) + jnp.exp(sinks[c] - mx[c])
        own_half = lane_lo if c[1] == 0 else jnp.logical_not(lane_lo)
        part = jnp.where(own_half, pv[c] / den, 0.0)
        out[c[0]] = part if c[0] not in out else out[c[0]] + part
    for h in range(B_KVHEADS):
        for p in range(PAIRS):
            o_ref[:, (h * PAIRS + p) * V7X_LANES:(h * PAIRS + p + 1) * V7X_LANES] = (
                out[h][p * ATTN_BLOCK:(p + 1) * ATTN_BLOCK].astype(o_ref.dtype))


def swa(hproj, sinks):
    m = hproj.shape[0]
    blk = ATTN_BLOCK
    q_col = (2 * A_FDIM + 2 * A_WIDTH) // B_WIDTH
    k_col = (2 * A_FDIM + 2 * A_WIDTH + B_WIDTH) // B_KVWIDTH
    v_col = k_col + 1
    qspec = pl.BlockSpec((blk, B_WIDTH), lambda n: (n, q_col))

    def kv(col, which):
        if which == "cur":
            return pl.BlockSpec((blk, B_KVWIDTH), lambda n: (n, col))
        if which == "prev":
            return pl.BlockSpec((blk, B_KVWIDTH), lambda n: (jnp.maximum(n - 1, 0), col))
        return pl.BlockSpec((blk, B_KVWIDTH), lambda n: (0, col))

    nbytes = 2 * blk * B_WIDTH * 4 + 12 * blk * B_KVWIDTH * 4 + 2 * blk * B_WIDTH * 2 + 40 * QROWS * 128 * 4
    return pl.pallas_call(
        _swa_kernel,
        out_shape=jax.ShapeDtypeStruct((m, B_WIDTH), BF16),
        grid=(m // blk,),
        in_specs=[pl.BlockSpec(memory_space=pltpu.SMEM), qspec,
                  kv(k_col, "cur"), kv(k_col, "prev"), kv(k_col, "meta"),
                  kv(v_col, "cur"), kv(v_col, "prev"), kv(v_col, "meta")],
        out_specs=pl.BlockSpec((blk, B_WIDTH), lambda n: (n, 0)),
        compiler_params=_params(("arbitrary",), nbytes),
        name="swa",
    )(sinks, hproj, hproj, hproj, hproj, hproj, hproj, hproj)


def kernel(x, meta_tokens, norm_w, w_in_even, lb_logits, hgrn_gn_w, attn_sinks, w_out_even,
           w_in_odd, lru_conv_w, lru_conv_b, lru_wa, lru_ba, lru_wx, lru_bx, lru_a_param, w_out_odd,
           ffn_w_gu, ffn_conv_w, ffn_conv_b, ffn_w_down):
    assert x.shape == (1, SEQ, D_MODEL) and norm_w.shape[0] == DEPTH
    x2d = x[0]
    xn = rms_cast_stream(x2d, meta_tokens, norm_w[0, 0])
    h, out = None, None
    for layer in range(DEPTH):
        j = layer // 2
        if layer % 2 == 0:
            hproj = matmul([xn], w_in_even, layer=j, tm=TM_IN_EVEN, tn=TN_IN_EVEN, name="in_even")
            o_a = hgrn2(hproj, lb_logits, hgrn_gn_w[j], j)
            o_b = swa(hproj, attn_sinks[j])
            mix = matmul([o_a, o_b], w_out_even, layer=j, tm=TM_OUT, tn=TN_OUT, name="out_even")
        else:
            y_br, x_br = in_odd(xn, w_in_odd, lru_conv_w, lru_conv_b, j)
            rec = rglru(x_br, y_br, lru_wa[j], lru_ba[j], lru_wx[j], lru_bx[j], lru_a_param[j])
            mix = matmul([rec], w_out_odd, layer=j, tm=TM_OUT, tn=TN_OUT, name="out_odd")
        if h is None:
            h, xn = resid_norm_stream(x2d, meta_tokens, mix, norm_w[layer, 1], norm_w[layer, 2])
        else:
            h, xn = resid_norm(h, mix, norm_w[layer, 1], norm_w[layer, 2])
        act, w_down_bf16 = ffn_up(xn, ffn_w_gu, ffn_conv_w, ffn_conv_b, ffn_w_down, layer)
        ff = matmul([act], w_down_bf16, tm=TM_DOWN, tn=TN_DOWN, name="ffn_down")
        if layer + 1 < DEPTH:
            h, xn = resid_norm(h, ff, norm_w[layer, 3], norm_w[layer + 1, 0])
        else:
            out = resid_final(h, ff, norm_w[layer, 3])
    return out[None]
```

```python
import functools

import jax
import jax.numpy as jnp
from jax import lax
from jax.experimental import pallas as pl
from jax.experimental.pallas import tpu as pltpu

F32 = jnp.float32
BF16 = jnp.bfloat16

D_MODEL = 4096
SEQ = 8192
DEPTH = 2
N_META = 16
A_HEADS = 16
A_KDIM = 128
A_VDIM = D_MODEL // 2 // A_HEADS
A_FDIM = A_HEADS * A_KDIM
A_WIDTH = A_HEADS * A_VDIM
HGRN_CHUNK = 64
HGRN_SUB = 16
B_HDIM = 64
B_QHEADS = D_MODEL // 2 // B_HDIM
B_KVHEADS = B_QHEADS // 8
B_GROUP = B_QHEADS // B_KVHEADS
B_WIDTH = B_QHEADS * B_HDIM
B_KVWIDTH = B_KVHEADS * B_HDIM
WINDOW = 128
ATTN_BLOCK = 128
EVEN_IN = 2 * A_FDIM + 2 * A_WIDTH + B_WIDTH + 2 * B_KVWIDTH
LRU_WIDTH = D_MODEL
LRU_BLOCKS = 16
LRU_BDIM = LRU_WIDTH // LRU_BLOCKS
LRU_CONV = 4
LRU_C = 8.0
D_FF = 256 * ((8 * D_MODEL // 3 + 255) // 256)
FFN_CONV = 3
NORM_EPS = 1e-6
NEG_INF = -1e30

PAD = ATTN_BLOCK - N_META
ROW0 = PAD + N_META
LP = ROW0 + SEQ
assert PAD % HGRN_CHUNK == HGRN_CHUNK - N_META and ROW0 == ATTN_BLOCK

V7X_LANES = 128
V7X_SUBLANES = 8
V7X_VMEM_LIMIT_CAP = 60 * 1024 * 1024
CARRY_ROWS = V7X_SUBLANES

TM_MM = 1040
TM_IN_EVEN = 1040
TN_IN_EVEN = 512
TM_OUT = 520
TN_OUT = 1024
TM_DOWN = 520
TN_DOWN = 512
TF_FFN = 256
TN_ODD = 256
EPI_ROWS = 80
K_CHUNK = 256
TR_NORM = 320
TR_LRU = 208
CW_LRU = 1024
TR_HGRN = 640
SCAN_ROWS = 16


def _vmem_limit(nbytes):
    return int(min(V7X_VMEM_LIMIT_CAP, nbytes * 1.15 + (4 << 20)))


def _params(sem, nbytes):
    return pltpu.CompilerParams(dimension_semantics=sem, vmem_limit_bytes=_vmem_limit(nbytes))


def _rms(x, w):
    return x * lax.rsqrt(jnp.mean(x * x, axis=-1, keepdims=True) + NORM_EPS) * w


def _iota(shape, dim):
    return lax.broadcasted_iota(jnp.int32, shape, dim)


def _dot(a, b):
    return jnp.dot(a, b, preferred_element_type=F32)


def _dot_nt(a, b):
    return lax.dot_general(a, b, (((1,), (1,)), ((), ())), preferred_element_type=F32)


def _dot_tn(a, b):
    return lax.dot_general(a, b, (((0,), (0,)), ((), ())), preferred_element_type=F32)


def _stream_rows(n, x_ref, meta_ref):
    first = jnp.concatenate([jnp.zeros((PAD, D_MODEL), F32), meta_ref[...].astype(F32)], axis=0)
    return jnp.where(n == 0, first, x_ref[...].astype(F32))


def _stream_specs():
    blk = ATTN_BLOCK
    return [pl.BlockSpec((blk, D_MODEL), lambda n: (jnp.maximum(n - 1, 0), 0)),
            pl.BlockSpec((N_META, D_MODEL), lambda n: (0, 0))]


def _rms_cast_stream_kernel(x_ref, meta_ref, w_ref, o_ref):
    h = _stream_rows(pl.program_id(0), x_ref, meta_ref)
    o_ref[...] = _rms(h, w_ref[...]).astype(o_ref.dtype)


def rms_cast_stream(x2d, meta, w):
    blk, d = ATTN_BLOCK, D_MODEL
    return pl.pallas_call(
        _rms_cast_stream_kernel,
        out_shape=jax.ShapeDtypeStruct((LP, d), BF16),
        grid=(LP // blk,),
        in_specs=_stream_specs() + [pl.BlockSpec((1, d), lambda n: (0, 0))],
        out_specs=pl.BlockSpec((blk, d), lambda n: (n, 0)),
        compiler_params=_params(("arbitrary",), 2 * blk * d * 6 + 4 * blk * d * 4),
        name="rms_cast_stream",
    )(x2d, meta, w.reshape(1, d))


def _resid_norm_stream_kernel(x_ref, meta_ref, y_ref, wp_ref, wn_ref, ho_ref, xo_ref):
    h = _stream_rows(pl.program_id(0), x_ref, meta_ref) + _rms(y_ref[...], wp_ref[...])
    ho_ref[...] = h
    xo_ref[...] = _rms(h, wn_ref[...]).astype(xo_ref.dtype)


def resid_norm_stream(x2d, meta, y, w_post, w_next):
    blk, d = ATTN_BLOCK, D_MODEL
    row = pl.BlockSpec((blk, d), lambda n: (n, 0))
    vec = pl.BlockSpec((1, d), lambda n: (0, 0))
    return pl.pallas_call(
        _resid_norm_stream_kernel,
        out_shape=(jax.ShapeDtypeStruct((LP, d), F32), jax.ShapeDtypeStruct((LP, d), BF16)),
        grid=(LP // blk,),
        in_specs=_stream_specs() + [row, vec, vec],
        out_specs=(row, row),
        compiler_params=_params(("arbitrary",), 2 * blk * d * 14 + 4 * blk * d * 4),
        name="resid_norm_stream",
    )(x2d, meta, y, w_post.reshape(1, d), w_next.reshape(1, d))


def _resid_norm_kernel(h_ref, y_ref, wp_ref, wn_ref, ho_ref, xo_ref):
    h = h_ref[...] + _rms(y_ref[...], wp_ref[...])
    ho_ref[...] = h
    xo_ref[...] = _rms(h, wn_ref[...]).astype(xo_ref.dtype)


def resid_norm(h, y, w_post, w_next):
    n, d = h.shape
    tr = TR_NORM
    row = pl.BlockSpec((tr, d), lambda i: (i, 0))
    vec = pl.BlockSpec((1, d), lambda i: (0, 0))
    return pl.pallas_call(
        _resid_norm_kernel,
        out_shape=(jax.ShapeDtypeStruct((n, d), F32), jax.ShapeDtypeStruct((n, d), BF16)),
        grid=(n // tr,),
        in_specs=[row, row, vec, vec],
        out_specs=(row, row),
        compiler_params=_params(("arbitrary",), 2 * tr * d * 14),
        name="resid_norm",
    )(h, y, w_post.reshape(1, d), w_next.reshape(1, d))


def _resid_final_kernel(h_ref, y_ref, wp_ref, o_ref):
    o_ref[...] = h_ref[...] + _rms(y_ref[...], wp_ref[...])


def resid_final(h, y, w_post):
    n, d = h.shape
    tr = ATTN_BLOCK
    off = ROW0 // tr
    src = pl.BlockSpec((tr, d), lambda i: (i + off, 0))
    return pl.pallas_call(
        _resid_final_kernel,
        out_shape=jax.ShapeDtypeStruct((n - ROW0, d), F32),
        grid=((n - ROW0) // tr,),
        in_specs=[src, src, pl.BlockSpec((1, d), lambda i: (0, 0))],
        out_specs=pl.BlockSpec((tr, d), lambda i: (i, 0)),
        compiler_params=_params(("arbitrary",), 2 * tr * d * 12),
        name="resid_final",
    )(h, y, w_post.reshape(1, d))


def _slab_specs(w, layer, k, tn, n_slabs, col_of):
    rows = k // n_slabs
    assert rows * n_slabs == k

    def spec(s):
        if w.ndim == 3:
            return pl.BlockSpec((None, rows, tn), lambda *g: (layer, s, col_of(*g, s)))
        return pl.BlockSpec((rows, tn), lambda *g: (s, col_of(*g, s)))

    return [spec(s) for s in range(n_slabs)]


def _cast_slabs(slab_refs, dst_ref):
    rows = slab_refs[0].shape[0]
    for s, ref in enumerate(slab_refs):
        dst_ref[s * rows:(s + 1) * rows, :] = ref[...].astype(BF16)


def _mm_kernel(*refs, n_lhs, n_slabs, cast):
    x_refs, w_refs, o_ref = refs[:n_lhs], refs[n_lhs:n_lhs + n_slabs], refs[n_lhs + n_slabs]
    if cast:
        w_ref = refs[n_lhs + n_slabs + 1]

        @pl.when(pl.program_id(1) == 0)
        def _():
            _cast_slabs(w_refs, w_ref)
    else:
        (w_ref,) = w_refs
    acc, k0 = None, 0
    for x_ref in x_refs:
        kk = x_ref.shape[1]
        part = _dot(x_ref[...], w_ref[k0:k0 + kk, :])
        acc = part if acc is None else acc + part
        k0 += kk
    o_ref[...] = acc.astype(o_ref.dtype)


def matmul(xs, w, *, layer=0, tm, tn, out_dtype=F32, name="matmul"):
    m = xs[0].shape[0]
    k, n = w.shape[-2:]
    assert sum(x.shape[1] for x in xs) == k
    cast = w.dtype != BF16
    ni, nj = m // tm, n // tn
    wbytes = k * tn * (4 if cast else 2)
    nbytes = 2 * tm * k * 2 + 2 * wbytes + (k * tn * 2 if cast else 0) + 2 * tm * tn * 4
    if cast:
        n_slabs = ni
        w_specs = _slab_specs(w, layer, k, tn, n_slabs,
                              lambda j, i, s: jnp.minimum(j + (i + ni - 1 - s) // ni, nj - 1))
    else:
        n_slabs = 1
        w_specs = _slab_specs(w, layer, k, tn, 1, lambda j, i, s: j)
    return pl.pallas_call(
        functools.partial(_mm_kernel, n_lhs=len(xs), n_slabs=n_slabs, cast=cast),
        out_shape=jax.ShapeDtypeStruct((m, n), out_dtype),
        grid=(nj, ni),
        in_specs=[pl.BlockSpec((tm, x.shape[1]), lambda j, i: (i, 0)) for x in xs] + w_specs,
        out_specs=pl.BlockSpec((tm, tn), lambda j, i: (i, j)),
        scratch_shapes=[pltpu.VMEM((k, tn), BF16)] if cast else [],
        compiler_params=_params(("arbitrary", "arbitrary"), nbytes),
        name=name,
    )(*xs, *([w] * n_slabs))


def _tile_maps(ni, nj):
    nt = ni * nj

    def cur_row(t):
        return lax.rem(jnp.minimum(t, nt - 1), ni)

    def cur_col(t):
        return lax.div(jnp.minimum(t, nt - 1), ni)

    def prev_row(t):
        return lax.rem(jnp.maximum(t - 1, 0), ni)

    def prev_col(t):
        return lax.div(jnp.maximum(t - 1, 0), ni)

    return nt, cur_row, cur_col, prev_row, prev_col


def _interleaved_projections(x_ref, w_refs, dst_stores, epilogue_chunk, tm):
    nk = x_ref.shape[1] // K_CHUNK
    chunk_rows = list(range(0, tm, EPI_ROWS))
    slots, slot, done = len(w_refs) * nk, 0, 0
    for w_ref, store in zip(w_refs, dst_stores):
        acc = None
        for kk in range(nk):
            ks = slice(kk * K_CHUNK, (kk + 1) * K_CHUNK)
            part = _dot(x_ref[:, ks], w_ref[ks, :])
            acc = part if acc is None else acc + part
            slot += 1
            while done < len(chunk_rows) * slot // slots:
                epilogue_chunk(chunk_rows[done])
                done += 1
        store(acc)


def _causal_conv(buf_ref, cw_ref, cb_ref, width, r0, rows):
    win = buf_ref[r0:r0 + CARRY_ROWS + rows, :]
    conv = cb_ref[...]
    for tap in range(width):
        back = width - 1 - tap
        shifted = pltpu.roll(win, back, 0) if back else win
        conv = conv + shifted[CARRY_ROWS:, :] * cw_ref[tap:tap + 1, :]
    return conv


def _ffn_up_kernel(*refs, tm, ni, nt):
    x_ref, wg_refs, wu_refs = refs[0], refs[1:1 + ni], refs[1 + ni:1 + 2 * ni]
    (cw_ref, cb_ref, wd_ref, o_ref, wdb_ref,
     wgb_ref, wub_ref, g0_ref, g1_ref, u0_ref, u1_ref) = refs[1 + 2 * ni:]
    t = pl.program_id(0)
    i = lax.rem(t, ni)

    @pl.when(t == 0)
    def _():
        g1_ref[...] = jnp.zeros_like(g1_ref)
        u1_ref[...] = jnp.zeros_like(u1_ref)

    @pl.when(jnp.logical_and(i == 0, t < nt))
    def _():
        _cast_slabs(wg_refs, wgb_ref)
        _cast_slabs(wu_refs, wub_ref)

    wdb_ref[...] = wd_ref[...].astype(BF16)

    def step(g_cur, u_cur, g_prev, u_prev):
        g_cur[0:CARRY_ROWS, :] = jnp.where(i == 0, 0.0, g_prev[tm:tm + CARRY_ROWS, :])

        def epilogue_chunk(r0):
            conv = _causal_conv(g_prev, cw_ref, cb_ref, FFN_CONV, r0, EPI_ROWS)
            o_ref[r0:r0 + EPI_ROWS, :] = (jax.nn.gelu(conv, approximate=True)
                                          * u_prev[r0:r0 + EPI_ROWS, :]).astype(o_ref.dtype)

        def store_gate(acc):
            g_cur[CARRY_ROWS:CARRY_ROWS + tm, :] = acc

        def store_up(acc):
            u_cur[...] = acc

        _interleaved_projections(x_ref, (wgb_ref, wub_ref), (store_gate, store_up), epilogue_chunk, tm)

    @pl.when(lax.rem(t, 2) == 0)
    def _():
        step(g0_ref, u0_ref, g1_ref, u1_ref)

    @pl.when(lax.rem(t, 2) == 1)
    def _():
        step(g1_ref, u1_ref, g0_ref, u0_ref)


def ffn_up(xn, w_gu, conv_w, conv_b, w_down, layer):
    m, k = xn.shape
    tm, tf = TM_MM, TF_FFN
    ni, nj = m // tm, D_FF // tf
    nt, cur_row, cur_col, prev_row, prev_col = _tile_maps(ni, nj)

    def slab_col(t, s):
        return jnp.minimum(lax.div(jnp.minimum(t, nt - 1) + ni - 1 - s, ni), nj - 1)

    wd_rows = D_FF // nt
    assert wd_rows * nt == D_FF and wd_rows % (2 * V7X_SUBLANES) == 0
    nbytes = (2 * tm * k * 2 + 4 * k * tf * 4 + 2 * k * tf * 2 + 2 * tm * tf * 2
              + 4 * (tm + CARRY_ROWS) * tf * 4 + 2 * wd_rows * D_MODEL * 6)
    return pl.pallas_call(
        functools.partial(_ffn_up_kernel, tm=tm, ni=ni, nt=nt),
        out_shape=(jax.ShapeDtypeStruct((m, D_FF), BF16), jax.ShapeDtypeStruct((D_FF, D_MODEL), BF16)),
        grid=(nt + 1,),
        in_specs=[pl.BlockSpec((tm, k), lambda t: (cur_row(t), 0))]
        + _slab_specs(w_gu, layer, k, tf, ni, slab_col)
        + _slab_specs(w_gu, layer, k, tf, ni, lambda t, s: slab_col(t, s) + nj)
        + [pl.BlockSpec((None, FFN_CONV, tf), lambda t: (layer, 0, prev_col(t))),
                  pl.BlockSpec((None, 1, tf), lambda t: (layer, 0, prev_col(t))),
                  pl.BlockSpec((None, wd_rows, D_MODEL), lambda t: (layer, jnp.minimum(t, nt - 1), 0))],
        out_specs=(pl.BlockSpec((tm, tf), lambda t: (prev_row(t), prev_col(t))),
                   pl.BlockSpec((wd_rows, D_MODEL), lambda t: (jnp.minimum(t, nt - 1), 0))),
        scratch_shapes=[pltpu.VMEM((k, tf), BF16), pltpu.VMEM((k, tf), BF16),
                        pltpu.VMEM((tm + CARRY_ROWS, tf), F32), pltpu.VMEM((tm + CARRY_ROWS, tf), F32),
                        pltpu.VMEM((tm, tf), F32), pltpu.VMEM((tm, tf), F32)],
        compiler_params=_params(("arbitrary",), nbytes),
        name="ffn_up",
    )(xn, *([w_gu] * (2 * ni)), conv_w, conv_b.reshape(conv_b.shape[0], 1, D_FF), w_down)


def _in_odd_kernel(*refs, tm, ni, nt):
    x_ref, wy_refs, wx_refs = refs[0], refs[1:1 + ni], refs[1 + ni:1 + 2 * ni]
    (cw_ref, cb_ref, y_ref, xc_ref,
     wyb_ref, wxb_ref, b0_ref, b1_ref, r0_ref, r1_ref) = refs[1 + 2 * ni:]
    t = pl.program_id(0)
    i = lax.rem(t, ni)

    @pl.when(t == 0)
    def _():
        b1_ref[...] = jnp.zeros_like(b1_ref)
        r1_ref[...] = jnp.zeros_like(r1_ref)

    @pl.when(jnp.logical_and(i == 0, t < nt))
    def _():
        _cast_slabs(wy_refs, wyb_ref)
        _cast_slabs(wx_refs, wxb_ref)

    def step(b_cur, r_cur, b_prev, r_prev):
        b_cur[0:CARRY_ROWS, :] = jnp.where(i == 0, 0.0, b_prev[tm:tm + CARRY_ROWS, :])

        def epilogue_chunk(r0):
            y_ref[r0:r0 + EPI_ROWS, :] = jax.nn.gelu(r_prev[r0:r0 + EPI_ROWS, :], approximate=True)
            xc_ref[r0:r0 + EPI_ROWS, :] = _causal_conv(b_prev, cw_ref, cb_ref, LRU_CONV, r0, EPI_ROWS)

        def store_y(acc):
            r_cur[...] = acc

        def store_x(acc):
            b_cur[CARRY_ROWS:CARRY_ROWS + tm, :] = acc

        _interleaved_projections(x_ref, (wyb_ref, wxb_ref), (store_y, store_x), epilogue_chunk, tm)

    @pl.when(lax.rem(t, 2) == 0)
    def _():
        step(b0_ref, r0_ref, b1_ref, r1_ref)

    @pl.when(lax.rem(t, 2) == 1)
    def _():
        step(b1_ref, r1_ref, b0_ref, r0_ref)


def in_odd(xn, w_in, conv_w, conv_b, layer):
    m, k = xn.shape
    tm, tn = TM_MM, TN_ODD
    ni, nj = m // tm, LRU_WIDTH // tn
    nt, cur_row, cur_col, prev_row, prev_col = _tile_maps(ni, nj)
    nbytes = (2 * tm * k * 2 + 4 * k * tn * 4 + 2 * k * tn * 2 + 4 * tm * tn * 4
              + 4 * (tm + CARRY_ROWS) * tn * 4)

    def slab_col(t, s):
        return jnp.minimum(lax.div(jnp.minimum(t, nt - 1) + ni - 1 - s, ni), nj - 1)

    out = pl.BlockSpec((tm, tn), lambda t: (prev_row(t), prev_col(t)))
    return pl.pallas_call(
        functools.partial(_in_odd_kernel, tm=tm, ni=ni, nt=nt),
        out_shape=(jax.ShapeDtypeStruct((m, LRU_WIDTH), F32), jax.ShapeDtypeStruct((m, LRU_WIDTH), F32)),
        grid=(nt + 1,),
        in_specs=[pl.BlockSpec((tm, k), lambda t: (cur_row(t), 0))]
        + _slab_specs(w_in, layer, k, tn, ni, slab_col)
        + _slab_specs(w_in, layer, k, tn, ni, lambda t, s: slab_col(t, s) + nj)
        + [pl.BlockSpec((None, LRU_CONV, tn), lambda t: (layer, 0, prev_col(t))),
                  pl.BlockSpec((None, 1, tn), lambda t: (layer, 0, prev_col(t)))],
        out_specs=(out, out),
        scratch_shapes=[pltpu.VMEM((k, tn), BF16), pltpu.VMEM((k, tn), BF16),
                        pltpu.VMEM((tm + CARRY_ROWS, tn), F32), pltpu.VMEM((tm + CARRY_ROWS, tn), F32),
                        pltpu.VMEM((tm, tn), F32), pltpu.VMEM((tm, tn), F32)],
        compiler_params=_params(("arbitrary",), nbytes),
        name="in_odd",
    )(xn, *([w_in] * (2 * ni)), conv_w, conv_b.reshape(conv_b.shape[0], 1, LRU_WIDTH))


def _lru_kernel(xc_ref, y_ref, wa_ref, wx_ref, ba_ref, bx_ref, ap_ref, o_ref, a_ref, b_ref, hc_ref, *, tr):
    step = pl.program_id(0)

    @pl.when(step == 0)
    def _():
        hc_ref[...] = jnp.zeros_like(hc_ref)

    valid = (step * tr + _iota((tr, 1), 0)) >= PAD
    for blk in range(LRU_BLOCKS):
        sl = slice(blk * LRU_BDIM, (blk + 1) * LRU_BDIM)
        x = xc_ref[:, sl]
        xb = x.astype(BF16)
        gate_r = 0.5 * jnp.tanh(0.5 * (_dot(xb, wa_ref[blk]) + ba_ref[:, sl])) + 0.5
        gate_i = 0.5 * jnp.tanh(0.5 * (_dot(xb, wx_ref[blk]) + bx_ref[:, sl])) + 0.5
        log_a = gate_r * (LRU_C * jax.nn.log_sigmoid(ap_ref[:, sl]))
        th = jnp.tanh(0.5 * log_a)
        em = 2.0 * th / (1.0 - th)
        a_ref[:, sl] = 1.0 + em
        inp = jnp.sqrt(-em * (2.0 + em)) * (gate_i * x)
        b_ref[:, sl] = jnp.where(valid, inp, 0.0)

    half = SCAN_ROWS // 2
    ridx = _iota((half, CW_LRU), 0)

    def local_scan(a, b):
        shift = 1
        while shift < half:
            ok = ridx >= shift
            b = jnp.where(ok, a * pltpu.roll(b, shift, 0) + b, b)
            a = jnp.where(ok, a * pltpu.roll(a, shift, 0), a)
            shift *= 2
        return a, b

    for c in range(LRU_WIDTH // CW_LRU):
        cs = slice(c * CW_LRU, (c + 1) * CW_LRU)

        def body(g, carry, cs=cs):
            r0 = pl.multiple_of(g * SCAN_ROWS, SCAN_ROWS)
            a = a_ref[pl.ds(r0, SCAN_ROWS), cs]
            b = b_ref[pl.ds(r0, SCAN_ROWS), cs]
            a_top, b_top = local_scan(a[:half], b[:half])
            a_bot, b_bot = local_scan(a[half:], b[half:])
            h_top = a_top * carry + b_top
            h_bot = a_bot * h_top[half - 1:half, :] + b_bot
            h = jnp.concatenate([h_top, h_bot], axis=0)
            o_ref[pl.ds(r0, SCAN_ROWS), cs] = (h * y_ref[pl.ds(r0, SCAN_ROWS), cs]).astype(o_ref.dtype)
            return h_bot[half - 1:half, :]

        hc_ref[0:1, cs] = lax.fori_loop(0, tr // SCAN_ROWS, body, hc_ref[0:1, cs])


def rglru(xc, y, w_a, b_a, w_x, b_x, a_param):
    m, d = xc.shape
    tr = TR_LRU
    row = pl.BlockSpec((tr, d), lambda i: (i, 0))
    vec = pl.BlockSpec((1, d), lambda i: (0, 0))
    wspec = pl.BlockSpec((LRU_BLOCKS, LRU_BDIM, LRU_BDIM), lambda i: (0, 0, 0))
    nbytes = 4 * tr * d * 4 + 2 * tr * d * 2 + 2 * tr * d * 4 + 4 * LRU_BLOCKS * LRU_BDIM * LRU_BDIM * 2
    return pl.pallas_call(
        functools.partial(_lru_kernel, tr=tr),
        out_shape=jax.ShapeDtypeStruct((m, d), BF16),
        grid=(m // tr,),
        in_specs=[row, row, wspec, wspec, vec, vec, vec],
        out_specs=row,
        scratch_shapes=[pltpu.VMEM((tr, d), F32), pltpu.VMEM((tr, d), F32),
                        pltpu.VMEM((V7X_SUBLANES, d), F32)],
        compiler_params=_params(("arbitrary",), nbytes),
        name="rglru",
    )(xc, y, w_a.astype(BF16), w_x.astype(BF16), b_a.reshape(1, d), b_x.reshape(1, d),
      a_param.reshape(1, d))


def _hgrn_kernel(q_ref, f_ref, i_ref, g_ref, lbl_ref, gnw_ref, o_ref, st_ref, *, tr, layer_j):
    step = pl.program_id(1)

    @pl.when(step == 0)
    def _():
        st_ref[...] = jnp.zeros_like(st_ref)

    logits = lbl_ref[...]
    e = jnp.exp(logits - jnp.max(logits, axis=0, keepdims=True))
    lb = jnp.sum(e[0:layer_j + 1], axis=0, keepdims=True) / jnp.sum(e, axis=0, keepdims=True)

    valid = (step * tr + _iota((tr, 1), 0)) >= PAD
    q = jax.nn.silu(q_ref[...])
    forget = lb + (1.0 - lb) * jax.nn.sigmoid(f_ref[...])
    k = jnp.where(valid, 1.0 - forget, 0.0)
    g = jnp.where(valid, jnp.log(forget), 0.0)
    v = i_ref[...]

    pos = _iota((tr, A_KDIM), 0) & (HGRN_CHUNK - 1)
    b = g
    shift = 1
    while shift < HGRN_CHUNK:
        b = b + jnp.where(pos >= shift, pltpu.roll(b, shift, 0), 0.0)
        shift *= 2

    n_sub = HGRN_CHUNK // HGRN_SUB
    cpos = _iota((HGRN_CHUNK, A_KDIM), 0)
    causal = _iota((HGRN_CHUNK, HGRN_CHUNK), 0) >= _iota((HGRN_CHUNK, HGRN_CHUNK), 1)
    gnw = gnw_ref[...]
    chunks = range(tr // HGRN_CHUNK)
    sl = [slice(c * HGRN_CHUNK, (c + 1) * HGRN_CHUNK) for c in chunks]
    bc = [b[s] for s in sl]
    qc = [q[s] for s in sl]
    kc = [k[s] for s in sl]
    vcb = [v[s].astype(BF16) for s in sl]
    b_last = [x[HGRN_CHUNK - 1:HGRN_CHUNK] for x in bc]
    u_t = [_dot_tn(vcb[c], (kc[c] * jnp.exp(b_last[c] - bc[c])).astype(BF16)) for c in chunks]
    att = []
    for c in chunks:
        rows = []
        for i in range(n_sub):
            ss = slice(i * HGRN_SUB, (i + 1) * HGRN_SUB)
            ref = jnp.zeros((1, A_KDIM), F32) if i == 0 else bc[c][i * HGRN_SUB - 1:i * HGRN_SUB]
            q_sc = (qc[c][ss] * jnp.exp(bc[c][ss] - ref)).astype(BF16)
            expo = jnp.where(cpos < (i + 1) * HGRN_SUB, ref - bc[c], 0.0)
            k_sc = (kc[c] * jnp.exp(expo)).astype(BF16)
            rows.append(_dot_nt(q_sc, k_sc))
        att.append(jnp.where(causal, jnp.concatenate(rows, axis=0), 0.0).astype(BF16))
    o_intra = [_dot(att[c], vcb[c]) for c in chunks]
    states = []
    st = st_ref[...]
    for c in chunks:
        states.append(st.astype(BF16))
        st = st * jnp.exp(b_last[c]) + u_t[c]
    st_ref[...] = st
    o_inter = [_dot_nt((qc[c] * jnp.exp(bc[c])).astype(BF16), states[c]) for c in chunks]
    for c in chunks:
        gate = jax.nn.silu(g_ref[sl[c], :])
        o_ref[sl[c], :] = (_rms(o_inter[c] + o_intra[c], gnw) * gate).astype(o_ref.dtype)


def hgrn2(hproj, lb_logits, gn_w, layer_j):
    m = hproj.shape[0]
    tr = TR_HGRN
    nrow = lb_logits.shape[0]

    def col(off):
        return pl.BlockSpec((tr, A_KDIM), lambda h, t, off=off: (t, h + off))

    nbytes = 2 * 4 * tr * A_KDIM * 4 + 2 * tr * A_VDIM * 2 + 12 * tr * A_KDIM * 4
    return pl.pallas_call(
        functools.partial(_hgrn_kernel, tr=tr, layer_j=layer_j),
        out_shape=jax.ShapeDtypeStruct((m, A_WIDTH), BF16),
        grid=(A_HEADS, m // tr),
        in_specs=[col(0), col(A_HEADS), col(2 * A_HEADS), col(3 * A_HEADS),
                  pl.BlockSpec((nrow, A_KDIM), lambda h, t: (0, h)),
                  pl.BlockSpec((1, A_VDIM), lambda h, t: (0, 0))],
        out_specs=pl.BlockSpec((tr, A_VDIM), lambda h, t: (t, h)),
        scratch_shapes=[pltpu.VMEM((A_VDIM, A_KDIM), F32)],
        compiler_params=_params(("arbitrary", "arbitrary"), nbytes),
        name="hgrn2",
    )(hproj, hproj, hproj, hproj, lb_logits, gn_w.reshape(1, A_VDIM))


PAIRS = B_GROUP // 2
QROWS = PAIRS * ATTN_BLOCK


def _swa_kernel(sink_ref, q_ref, kc_ref, kp_ref, km_ref, vc_ref, vp_ref, vm_ref, o_ref):
    n = pl.program_id(0)
    scale = B_HDIM ** -0.5
    lane_lo = _iota((1, V7X_LANES), 1) < B_HDIM
    tq = _iota((QROWS, ATTN_BLOCK), 0) & (ATTN_BLOCK - 1)
    sk = _iota((QROWS, ATTN_BLOCK), 1)
    masks = (sk > tq + jnp.where(n >= 2, 0, 2 * ATTN_BLOCK),
             sk <= tq + jnp.where(n >= 1, 0, -2 * ATTN_BLOCK),
             jnp.logical_and(sk >= PAD, sk <= tq + jnp.where(n >= 1, ATTN_BLOCK, 0)))

    def split(x, natural_lo, fill):
        rolled = pltpu.roll(x, B_HDIM, 1)
        lo_src, hi_src = (x, rolled) if natural_lo else (rolled, x)
        return (jnp.where(lane_lo, lo_src, fill).astype(BF16), jnp.where(lane_lo, fill, hi_src).astype(BF16))

    chains = [(h, par) for h in range(B_KVHEADS) for par in range(2)]
    keys, vals, qs = {}, {}, {}
    for h in range(B_KVHEADS):
        tile = slice((h // 2) * V7X_LANES, (h // 2 + 1) * V7X_LANES)
        nat = h % 2 == 0
        keys[h] = [split(r[:, tile], nat, 0.0) for r in (kp_ref, kc_ref, km_ref)]
        vals[h] = [split(r[:, tile], nat, 1.0) for r in (vp_ref, vc_ref, vm_ref)]
        qs[h] = (jnp.concatenate(
            [q_ref[:, (h * PAIRS + p) * V7X_LANES:(h * PAIRS + p + 1) * V7X_LANES] for p in range(PAIRS)],
            axis=0) * scale).astype(BF16)
    logits = {c: [jnp.where(m, _dot_nt(qs[c[0]], kk[c[1]]), NEG_INF) for m, kk in zip(masks, keys[c[0]])]
              for c in chains}
    sinks = {(h, par): jnp.concatenate(
        [jnp.full((ATTN_BLOCK, 1), sink_ref[h * B_GROUP + 2 * p + par], F32) for p in range(PAIRS)], axis=0)
        for h, par in chains}
    mx = {c: jnp.maximum(jnp.max(jnp.maximum(jnp.maximum(logits[c][0], logits[c][1]), logits[c][2]),
                                 axis=-1, keepdims=True), sinks[c]) for c in chains}
    pv = {}
    for c in chains:
        for lg, vv in zip(logits[c], vals[c[0]]):
            part = _dot(jnp.exp(lg - mx[c]).astype(BF16), vv[c[1]])
            pv[c] = part if c not in pv else pv[c] + part
    out = {}
    for c in chains:
        den = pltpu.roll(pv[c], B_HDIM, 1) + jnp.exp(sinks[c] - mx[c])
        own_half = lane_lo if c[1] == 0 else jnp.logical_not(lane_lo)
        part = jnp.where(own_half, pv[c] / den, 0.0)
        out[c[0]] = part if c[0] not in out else out[c[0]] + part
    for h in range(B_KVHEADS):
        for p in range(PAIRS):
            o_ref[:, (h * PAIRS + p) * V7X_LANES:(h * PAIRS + p + 1) * V7X_LANES] = (
                out[h][p * ATTN_BLOCK:(p + 1) * ATTN_BLOCK].astype(o_ref.dtype))


def swa(hproj, sinks):
    m = hproj.shape[0]
    blk = ATTN_BLOCK
    q_col = (2 * A_FDIM + 2 * A_WIDTH) // B_WIDTH
    k_col = (2 * A_FDIM + 2 * A_WIDTH + B_WIDTH) // B_KVWIDTH
    v_col = k_col + 1
    qspec = pl.BlockSpec((blk, B_WIDTH), lambda n: (n, q_col))

    def kv(col, which):
        if which == "cur":
            return pl.BlockSpec((blk, B_KVWIDTH), lambda n: (n, col))
        if which == "prev":
            return pl.BlockSpec((blk, B_KVWIDTH), lambda n: (jnp.maximum(n - 1, 0), col))
        return pl.BlockSpec((blk, B_KVWIDTH), lambda n: (0, col))

    nbytes = 2 * blk * B_WIDTH * 4 + 12 * blk * B_KVWIDTH * 4 + 2 * blk * B_WIDTH * 2 + 40 * QROWS * 128 * 4
    return pl.pallas_call(
        _swa_kernel,
        out_shape=jax.ShapeDtypeStruct((m, B_WIDTH), BF16),
        grid=(m // blk,),
        in_specs=[pl.BlockSpec(memory_space=pltpu.SMEM), qspec,
                  kv(k_col, "cur"), kv(k_col, "prev"), kv(k_col, "meta"),
                  kv(v_col, "cur"), kv(v_col, "prev"), kv(v_col, "meta")],
        out_specs=pl.BlockSpec((blk, B_WIDTH), lambda n: (n, 0)),
        compiler_params=_params(("arbitrary",), nbytes),
        name="swa",
    )(sinks, hproj, hproj, hproj, hproj, hproj, hproj, hproj)


def kernel(x, meta_tokens, norm_w, w_in_even, lb_logits, hgrn_gn_w, attn_sinks, w_out_even,
           w_in_odd, lru_conv_w, lru_conv_b, lru_wa, lru_ba, lru_wx, lru_bx, lru_a_param, w_out_odd,
           ffn_w_gu, ffn_conv_w, ffn_conv_b, ffn_w_down):
    assert x.shape == (1, SEQ, D_MODEL) and norm_w.shape[0] == DEPTH
    x2d = x[0]
    xn = rms_cast_stream(x2d, meta_tokens, norm_w[0, 0])
    h, out = None, None
    for layer in range(DEPTH):
        j = layer // 2
        if layer % 2 == 0:
            hproj = matmul([xn], w_in_even, layer=j, tm=TM_IN_EVEN, tn=TN_IN_EVEN, name="in_even")
            o_a = hgrn2(hproj, lb_logits, hgrn_gn_w[j], j)
            o_b = swa(hproj, attn_sinks[j])
            mix = matmul([o_a, o_b], w_out_even, layer=j, tm=TM_OUT, tn=TN_OUT, name="out_even")
        else:
            y_br, x_br = in_odd(xn, w_in_odd, lru_conv_w, lru_conv_b, j)
            rec = rglru(x_br, y_br, lru_wa[j], lru_ba[j], lru_wx[j], lru_bx[j], lru_a_param[j])
            mix = matmul([rec], w_out_odd, layer=j, tm=TM_OUT, tn=TN_OUT, name="out_odd")
        if h is None:
            h, xn = resid_norm_stream(x2d, meta_tokens, mix, norm_w[layer, 1], norm_w[layer, 2])
        else:
            h, xn = resid_norm(h, mix, norm_w[layer, 1], norm_w[layer, 2])
        act, w_down_bf16 = ffn_up(xn, ffn_w_gu, ffn_conv_w, ffn_conv_b, ffn_w_down, layer)
        ff = matmul([act], w_down_bf16, tm=TM_DOWN, tn=TN_DOWN, name="ffn_down")
        if layer + 1 < DEPTH:
            h, xn = resid_norm(h, ff, norm_w[layer, 3], norm_w[layer + 1, 0])
        else:
            out = resid_final(h, ff, norm_w[layer, 3])
    return out[None]
```

```python
import functools

import jax
import jax.numpy as jnp
from jax import lax
from jax.experimental import pallas as pl
from jax.experimental.pallas import tpu as pltpu

F32 = jnp.float32
BF16 = jnp.bfloat16

D_MODEL = 4096
SEQ = 8192
DEPTH = 2
N_META = 16
A_HEADS = 16
A_KDIM = 128
A_VDIM = D_MODEL // 2 // A_HEADS
A_FDIM = A_HEADS * A_KDIM
A_WIDTH = A_HEADS * A_VDIM
HGRN_CHUNK = 64
HGRN_SUB = 16
B_HDIM = 64
B_QHEADS = D_MODEL // 2 // B_HDIM
B_KVHEADS = B_QHEADS // 8
B_GROUP = B_QHEADS // B_KVHEADS
B_WIDTH = B_QHEADS * B_HDIM
B_KVWIDTH = B_KVHEADS * B_HDIM
WINDOW = 128
ATTN_BLOCK = 128
EVEN_IN = 2 * A_FDIM + 2 * A_WIDTH + B_WIDTH + 2 * B_KVWIDTH
LRU_WIDTH = D_MODEL
LRU_BLOCKS = 16
LRU_BDIM = LRU_WIDTH // LRU_BLOCKS
LRU_CONV = 4
LRU_C = 8.0
D_FF = 256 * ((8 * D_MODEL // 3 + 255) // 256)
FFN_CONV = 3
NORM_EPS = 1e-6
NEG_INF = -1e30

PAD = ATTN_BLOCK - N_META
ROW0 = PAD + N_META
LP = ROW0 + SEQ
assert PAD % HGRN_CHUNK == HGRN_CHUNK - N_META and ROW0 == ATTN_BLOCK

V7X_LANES = 128
V7X_SUBLANES = 8
V7X_VMEM_LIMIT_CAP = 60 * 1024 * 1024
CARRY_ROWS = V7X_SUBLANES

TM_MM = 1040
TM_IN_EVEN = 1040
TN_IN_EVEN = 512
TM_OUT = 520
TN_OUT = 1024
TM_DOWN = 520
TN_DOWN = 512
TF_FFN = 256
TN_ODD = 256
EPI_ROWS = 80
K_CHUNK = 256
MAX_WEIGHT_SLABS = 8
TR_NORM = 320
TR_LRU = 208
CW_LRU = 1024
TR_HGRN = 640
SCAN_ROWS = 16


def _vmem_limit(nbytes):
    return int(min(V7X_VMEM_LIMIT_CAP, nbytes * 1.15 + (4 << 20)))


def _params(sem, nbytes):
    return pltpu.CompilerParams(dimension_semantics=sem, vmem_limit_bytes=_vmem_limit(nbytes))


def _rms(x, w):
    x = x.astype(F32)
    return x * lax.rsqrt(jnp.mean(x * x, axis=-1, keepdims=True) + NORM_EPS) * w


def _iota(shape, dim):
    return lax.broadcasted_iota(jnp.int32, shape, dim)


def _dot(a, b):
    return jnp.dot(a, b, preferred_element_type=F32)


def _dot_nt(a, b):
    return lax.dot_general(a, b, (((1,), (1,)), ((), ())), preferred_element_type=F32)


def _dot_tn(a, b):
    return lax.dot_general(a, b, (((0,), (0,)), ((), ())), preferred_element_type=F32)


def _stream_rows(n, x_ref, meta_ref):
    first = jnp.concatenate([jnp.zeros((PAD, D_MODEL), F32), meta_ref[...].astype(F32)], axis=0)
    return jnp.where(n == 0, first, x_ref[...].astype(F32))


def _stream_specs():
    blk = ATTN_BLOCK
    return [pl.BlockSpec((blk, D_MODEL), lambda n: (jnp.maximum(n - 1, 0), 0)),
            pl.BlockSpec((N_META, D_MODEL), lambda n: (0, 0))]


def _rms_cast_stream_kernel(x_ref, meta_ref, w_ref, o_ref):
    h = _stream_rows(pl.program_id(0), x_ref, meta_ref)
    o_ref[...] = _rms(h, w_ref[...]).astype(o_ref.dtype)


def rms_cast_stream(x2d, meta, w):
    blk, d = ATTN_BLOCK, D_MODEL
    return pl.pallas_call(
        _rms_cast_stream_kernel,
        out_shape=jax.ShapeDtypeStruct((LP, d), BF16),
        grid=(LP // blk,),
        in_specs=_stream_specs() + [pl.BlockSpec((1, d), lambda n: (0, 0))],
        out_specs=pl.BlockSpec((blk, d), lambda n: (n, 0)),
        compiler_params=_params(("arbitrary",), 2 * blk * d * 6 + 4 * blk * d * 4),
        name="rms_cast_stream",
    )(x2d, meta, w.reshape(1, d))


def _resid_norm_stream_kernel(x_ref, meta_ref, y_ref, wp_ref, wn_ref, ho_ref, xo_ref):
    h = _stream_rows(pl.program_id(0), x_ref, meta_ref) + _rms(y_ref[...], wp_ref[...])
    ho_ref[...] = h
    xo_ref[...] = _rms(h, wn_ref[...]).astype(xo_ref.dtype)


def resid_norm_stream(x2d, meta, y, w_post, w_next):
    blk, d = ATTN_BLOCK, D_MODEL
    row = pl.BlockSpec((blk, d), lambda n: (n, 0))
    vec = pl.BlockSpec((1, d), lambda n: (0, 0))
    return pl.pallas_call(
        _resid_norm_stream_kernel,
        out_shape=(jax.ShapeDtypeStruct((LP, d), F32), jax.ShapeDtypeStruct((LP, d), BF16)),
        grid=(LP // blk,),
        in_specs=_stream_specs() + [row, vec, vec],
        out_specs=(row, row),
        compiler_params=_params(("arbitrary",), 2 * blk * d * 14 + 4 * blk * d * 4),
        name="resid_norm_stream",
    )(x2d, meta, y, w_post.reshape(1, d), w_next.reshape(1, d))


def _resid_norm_kernel(h_ref, y_ref, wp_ref, wn_ref, ho_ref, xo_ref):
    h = h_ref[...] + _rms(y_ref[...], wp_ref[...])
    ho_ref[...] = h
    xo_ref[...] = _rms(h, wn_ref[...]).astype(xo_ref.dtype)


def resid_norm(h, y, w_post, w_next):
    n, d = h.shape
    tr = TR_NORM
    row = pl.BlockSpec((tr, d), lambda i: (i, 0))
    vec = pl.BlockSpec((1, d), lambda i: (0, 0))
    return pl.pallas_call(
        _resid_norm_kernel,
        out_shape=(jax.ShapeDtypeStruct((n, d), F32), jax.ShapeDtypeStruct((n, d), BF16)),
        grid=(n // tr,),
        in_specs=[row, row, vec, vec],
        out_specs=(row, row),
        compiler_params=_params(("arbitrary",), 2 * tr * d * 14),
        name="resid_norm",
    )(h, y, w_post.reshape(1, d), w_next.reshape(1, d))


def _resid_final_kernel(h_ref, y_ref, wp_ref, o_ref):
    o_ref[...] = h_ref[...] + _rms(y_ref[...], wp_ref[...])


def resid_final(h, y, w_post):
    n, d = h.shape
    tr = ATTN_BLOCK
    off = ROW0 // tr
    src = pl.BlockSpec((tr, d), lambda i: (i + off, 0))
    return pl.pallas_call(
        _resid_final_kernel,
        out_shape=jax.ShapeDtypeStruct((n - ROW0, d), F32),
        grid=((n - ROW0) // tr,),
        in_specs=[src, src, pl.BlockSpec((1, d), lambda i: (0, 0))],
        out_specs=pl.BlockSpec((tr, d), lambda i: (i, 0)),
        compiler_params=_params(("arbitrary",), 2 * tr * d * 12),
        name="resid_final",
    )(h, y, w_post.reshape(1, d))


def _slab_specs(w, layer, k, tn, n_slabs, col_of):
    rows = k // n_slabs
    assert rows * n_slabs == k

    def spec(s):
        if w.ndim == 3:
            return pl.BlockSpec((None, rows, tn), lambda *g: (layer, s, col_of(*g, s)))
        return pl.BlockSpec((rows, tn), lambda *g: (s, col_of(*g, s)))

    return [spec(s) for s in range(n_slabs)]


def _cast_slabs(slab_refs, dst_ref):
    rows = slab_refs[0].shape[0]
    for s, ref in enumerate(slab_refs):
        dst_ref[s * rows:(s + 1) * rows, :] = ref[...].astype(BF16)


def _mm_kernel(*refs, n_lhs, n_slabs, cast):
    x_refs, w_refs, o_ref = refs[:n_lhs], refs[n_lhs:n_lhs + n_slabs], refs[n_lhs + n_slabs]
    if cast:
        w_ref = refs[n_lhs + n_slabs + 1]

        @pl.when(pl.program_id(1) == 0)
        def _():
            _cast_slabs(w_refs, w_ref)
    else:
        (w_ref,) = w_refs
    acc, k0 = None, 0
    for x_ref in x_refs:
        kk = x_ref.shape[1]
        part = _dot(x_ref[...], w_ref[k0:k0 + kk, :])
        acc = part if acc is None else acc + part
        k0 += kk
    o_ref[...] = acc.astype(o_ref.dtype)


def matmul(xs, w, *, layer=0, tm, tn, out_dtype=F32, name="matmul"):
    m = xs[0].shape[0]
    k, n = w.shape[-2:]
    assert sum(x.shape[1] for x in xs) == k
    cast = w.dtype != BF16
    ni, nj = m // tm, n // tn
    wbytes = k * tn * (4 if cast else 2)
    nbytes = 2 * tm * k * 2 + 2 * wbytes + (k * tn * 2 if cast else 0) + 2 * tm * tn * 4
    if cast and ni <= MAX_WEIGHT_SLABS:
        n_slabs = ni
        w_specs = _slab_specs(w, layer, k, tn, n_slabs,
                              lambda j, i, s: jnp.minimum(j + (i + ni - 1 - s) // ni, nj - 1))
    else:
        n_slabs = 1
        w_specs = _slab_specs(w, layer, k, tn, 1, lambda j, i, s: j)
    return pl.pallas_call(
        functools.partial(_mm_kernel, n_lhs=len(xs), n_slabs=n_slabs, cast=cast),
        out_shape=jax.ShapeDtypeStruct((m, n), out_dtype),
        grid=(nj, ni),
        in_specs=[pl.BlockSpec((tm, x.shape[1]), lambda j, i: (i, 0)) for x in xs] + w_specs,
        out_specs=pl.BlockSpec((tm, tn), lambda j, i: (i, j)),
        scratch_shapes=[pltpu.VMEM((k, tn), BF16)] if cast else [],
        compiler_params=_params(("arbitrary", "arbitrary"), nbytes),
        name=name,
    )(*xs, *([w] * n_slabs))


def _tile_maps(ni, nj):
    nt = ni * nj

    def cur_row(t):
        return lax.rem(jnp.minimum(t, nt - 1), ni)

    def cur_col(t):
        return lax.div(jnp.minimum(t, nt - 1), ni)

    def prev_row(t):
        return lax.rem(jnp.maximum(t - 1, 0), ni)

    def prev_col(t):
        return lax.div(jnp.maximum(t - 1, 0), ni)

    return nt, cur_row, cur_col, prev_row, prev_col


def _interleaved_projections(x_ref, w_refs, dst_stores, epilogue_chunk, tm):
    nk = x_ref.shape[1] // K_CHUNK
    chunk_rows = list(range(0, tm, EPI_ROWS))
    slots, slot, done = len(w_refs) * nk, 0, 0
    for w_ref, store in zip(w_refs, dst_stores):
        acc = None
        for kk in range(nk):
            ks = slice(kk * K_CHUNK, (kk + 1) * K_CHUNK)
            part = _dot(x_ref[:, ks], w_ref[ks, :])
            acc = part if acc is None else acc + part
            slot += 1
            while done < len(chunk_rows) * slot // slots:
                epilogue_chunk(chunk_rows[done])
                done += 1
        store(acc)


def _causal_conv(buf_ref, cw_ref, cb_ref, width, r0, rows):
    win = buf_ref[r0:r0 + CARRY_ROWS + rows, :]
    conv = cb_ref[...]
    for tap in range(width):
        back = width - 1 - tap
        shifted = pltpu.roll(win, back, 0) if back else win
        conv = conv + shifted[CARRY_ROWS:, :] * cw_ref[tap:tap + 1, :]
    return conv


def _ffn_up_kernel(*refs, tm, ni, nt):
    x_ref, wg_refs, wu_refs = refs[0], refs[1:1 + ni], refs[1 + ni:1 + 2 * ni]
    (cw_ref, cb_ref, wd_ref, o_ref, wdb_ref,
     wgb_ref, wub_ref, g0_ref, g1_ref, u0_ref, u1_ref) = refs[1 + 2 * ni:]
    t = pl.program_id(0)
    i = lax.rem(t, ni)

    @pl.when(t == 0)
    def _():
        g1_ref[...] = jnp.zeros_like(g1_ref)
        u1_ref[...] = jnp.zeros_like(u1_ref)

    @pl.when(jnp.logical_and(i == 0, t < nt))
    def _():
        _cast_slabs(wg_refs, wgb_ref)
        _cast_slabs(wu_refs, wub_ref)

    wdb_ref[...] = wd_ref[...].astype(BF16)

    def step(g_cur, u_cur, g_prev, u_prev):
        g_cur[0:CARRY_ROWS, :] = jnp.where(i == 0, 0.0, g_prev[tm:tm + CARRY_ROWS, :])

        def epilogue_chunk(r0):
            conv = _causal_conv(g_prev, cw_ref, cb_ref, FFN_CONV, r0, EPI_ROWS)
            o_ref[r0:r0 + EPI_ROWS, :] = (jax.nn.gelu(conv, approximate=True)
                                          * u_prev[r0:r0 + EPI_ROWS, :]).astype(o_ref.dtype)

        def store_gate(acc):
            g_cur[CARRY_ROWS:CARRY_ROWS + tm, :] = acc

        def store_up(acc):
            u_cur[...] = acc

        _interleaved_projections(x_ref, (wgb_ref, wub_ref), (store_gate, store_up), epilogue_chunk, tm)

    @pl.when(lax.rem(t, 2) == 0)
    def _():
        step(g0_ref, u0_ref, g1_ref, u1_ref)

    @pl.when(lax.rem(t, 2) == 1)
    def _():
        step(g1_ref, u1_ref, g0_ref, u0_ref)


def ffn_up(xn, w_gu, conv_w, conv_b, w_down, layer):
    m, k = xn.shape
    tm, tf = TM_MM, TF_FFN
    ni, nj = m // tm, D_FF // tf
    nt, cur_row, cur_col, prev_row, prev_col = _tile_maps(ni, nj)

    def slab_col(t, s):
        return jnp.minimum(lax.div(jnp.minimum(t, nt - 1) + ni - 1 - s, ni), nj - 1)

    wd_rows = D_FF // nt
    assert wd_rows * nt == D_FF and wd_rows % (2 * V7X_SUBLANES) == 0
    nbytes = (2 * tm * k * 2 + 4 * k * tf * 4 + 2 * k * tf * 2 + 2 * tm * tf * 2
              + 4 * (tm + CARRY_ROWS) * tf * 4 + 2 * wd_rows * D_MODEL * 6)
    return pl.pallas_call(
        functools.partial(_ffn_up_kernel, tm=tm, ni=ni, nt=nt),
        out_shape=(jax.ShapeDtypeStruct((m, D_FF), BF16), jax.ShapeDtypeStruct((D_FF, D_MODEL), BF16)),
        grid=(nt + 1,),
        in_specs=[pl.BlockSpec((tm, k), lambda t: (cur_row(t), 0))]
        + _slab_specs(w_gu, layer, k, tf, ni, slab_col)
        + _slab_specs(w_gu, layer, k, tf, ni, lambda t, s: slab_col(t, s) + nj)
        + [pl.BlockSpec((None, FFN_CONV, tf), lambda t: (layer, 0, prev_col(t))),
                  pl.BlockSpec((None, 1, tf), lambda t: (layer, 0, prev_col(t))),
                  pl.BlockSpec((None, wd_rows, D_MODEL), lambda t: (layer, jnp.minimum(t, nt - 1), 0))],
        out_specs=(pl.BlockSpec((tm, tf), lambda t: (prev_row(t), prev_col(t))),
                   pl.BlockSpec((wd_rows, D_MODEL), lambda t: (jnp.minimum(t, nt - 1), 0))),
        scratch_shapes=[pltpu.VMEM((k, tf), BF16), pltpu.VMEM((k, tf), BF16),
                        pltpu.VMEM((tm + CARRY_ROWS, tf), F32), pltpu.VMEM((tm + CARRY_ROWS, tf), F32),
                        pltpu.VMEM((tm, tf), F32), pltpu.VMEM((tm, tf), F32)],
        compiler_params=_params(("arbitrary",), nbytes),
        name="ffn_up",
    )(xn, *([w_gu] * (2 * ni)), conv_w, conv_b.reshape(conv_b.shape[0], 1, D_FF), w_down)


def _in_odd_kernel(*refs, tm, ni, nt):
    x_ref, wy_refs, wx_refs = refs[0], refs[1:1 + ni], refs[1 + ni:1 + 2 * ni]
    (cw_ref, cb_ref, y_ref, xc_ref,
     wyb_ref, wxb_ref, b0_ref, b1_ref, r0_ref, r1_ref) = refs[1 + 2 * ni:]
    t = pl.program_id(0)
    i = lax.rem(t, ni)

    @pl.when(t == 0)
    def _():
        b1_ref[...] = jnp.zeros_like(b1_ref)
        r1_ref[...] = jnp.zeros_like(r1_ref)

    @pl.when(jnp.logical_and(i == 0, t < nt))
    def _():
        _cast_slabs(wy_refs, wyb_ref)
        _cast_slabs(wx_refs, wxb_ref)

    def step(b_cur, r_cur, b_prev, r_prev):
        b_cur[0:CARRY_ROWS, :] = jnp.where(i == 0, 0.0, b_prev[tm:tm + CARRY_ROWS, :])

        def epilogue_chunk(r0):
            y_ref[r0:r0 + EPI_ROWS, :] = jax.nn.gelu(r_prev[r0:r0 + EPI_ROWS, :],
                                                     approximate=True).astype(y_ref.dtype)
            xc_ref[r0:r0 + EPI_ROWS, :] = _causal_conv(b_prev, cw_ref, cb_ref, LRU_CONV, r0, EPI_ROWS)

        def store_y(acc):
            r_cur[...] = acc

        def store_x(acc):
            b_cur[CARRY_ROWS:CARRY_ROWS + tm, :] = acc

        _interleaved_projections(x_ref, (wyb_ref, wxb_ref), (store_y, store_x), epilogue_chunk, tm)

    @pl.when(lax.rem(t, 2) == 0)
    def _():
        step(b0_ref, r0_ref, b1_ref, r1_ref)

    @pl.when(lax.rem(t, 2) == 1)
    def _():
        step(b1_ref, r1_ref, b0_ref, r0_ref)


def in_odd(xn, w_in, conv_w, conv_b, layer):
    m, k = xn.shape
    tm, tn = TM_MM, TN_ODD
    ni, nj = m // tm, LRU_WIDTH // tn
    nt, cur_row, cur_col, prev_row, prev_col = _tile_maps(ni, nj)
    nbytes = (2 * tm * k * 2 + 4 * k * tn * 4 + 2 * k * tn * 2 + 4 * tm * tn * 4
              + 4 * (tm + CARRY_ROWS) * tn * 4)

    def slab_col(t, s):
        return jnp.minimum(lax.div(jnp.minimum(t, nt - 1) + ni - 1 - s, ni), nj - 1)

    out = pl.BlockSpec((tm, tn), lambda t: (prev_row(t), prev_col(t)))
    return pl.pallas_call(
        functools.partial(_in_odd_kernel, tm=tm, ni=ni, nt=nt),
        out_shape=(jax.ShapeDtypeStruct((m, LRU_WIDTH), BF16), jax.ShapeDtypeStruct((m, LRU_WIDTH), F32)),
        grid=(nt + 1,),
        in_specs=[pl.BlockSpec((tm, k), lambda t: (cur_row(t), 0))]
        + _slab_specs(w_in, layer, k, tn, ni, slab_col)
        + _slab_specs(w_in, layer, k, tn, ni, lambda t, s: slab_col(t, s) + nj)
        + [pl.BlockSpec((None, LRU_CONV, tn), lambda t: (layer, 0, prev_col(t))),
                  pl.BlockSpec((None, 1, tn), lambda t: (layer, 0, prev_col(t)))],
        out_specs=(out, out),
        scratch_shapes=[pltpu.VMEM((k, tn), BF16), pltpu.VMEM((k, tn), BF16),
                        pltpu.VMEM((tm + CARRY_ROWS, tn), F32), pltpu.VMEM((tm + CARRY_ROWS, tn), F32),
                        pltpu.VMEM((tm, tn), F32), pltpu.VMEM((tm, tn), F32)],
        compiler_params=_params(("arbitrary",), nbytes),
        name="in_odd",
    )(xn, *([w_in] * (2 * ni)), conv_w, conv_b.reshape(conv_b.shape[0], 1, LRU_WIDTH))


def _lru_kernel(xc_ref, y_ref, wa_ref, wx_ref, ba_ref, bx_ref, ap_ref, o_ref, a_ref, b_ref, hc_ref, *, tr):
    step = pl.program_id(0)

    @pl.when(step == 0)
    def _():
        hc_ref[...] = jnp.zeros_like(hc_ref)

    valid = (step * tr + _iota((tr, 1), 0)) >= PAD
    for blk in range(LRU_BLOCKS):
        sl = slice(blk * LRU_BDIM, (blk + 1) * LRU_BDIM)
        x = xc_ref[:, sl]
        xb = x.astype(BF16)
        gate_r = 0.5 * jnp.tanh(0.5 * (_dot(xb, wa_ref[blk]) + ba_ref[:, sl])) + 0.5
        gate_i = 0.5 * jnp.tanh(0.5 * (_dot(xb, wx_ref[blk]) + bx_ref[:, sl])) + 0.5
        log_a = gate_r * (LRU_C * jax.nn.log_sigmoid(ap_ref[:, sl]))
        th = jnp.tanh(0.5 * log_a)
        em = 2.0 * th / (1.0 - th)
        a_ref[:, sl] = 1.0 + em
        inp = jnp.sqrt(-em * (2.0 + em)) * (gate_i * x)
        b_ref[:, sl] = jnp.where(valid, inp, 0.0)

    half = SCAN_ROWS // 2
    ridx = _iota((half, CW_LRU), 0)

    def local_scan(a, b):
        shift = 1
        while shift < half:
            ok = ridx >= shift
            b = jnp.where(ok, a * pltpu.roll(b, shift, 0) + b, b)
            a = jnp.where(ok, a * pltpu.roll(a, shift, 0), a)
            shift *= 2
        return a, b

    for c in range(LRU_WIDTH // CW_LRU):
        cs = slice(c * CW_LRU, (c + 1) * CW_LRU)

        def body(g, carry, cs=cs):
            r0 = pl.multiple_of(g * SCAN_ROWS, SCAN_ROWS)
            a = a_ref[pl.ds(r0, SCAN_ROWS), cs]
            b = b_ref[pl.ds(r0, SCAN_ROWS), cs]
            a_top, b_top = local_scan(a[:half], b[:half])
            a_bot, b_bot = local_scan(a[half:], b[half:])
            h_top = a_top * carry + b_top
            h_bot = a_bot * h_top[half - 1:half, :] + b_bot
            h = jnp.concatenate([h_top, h_bot], axis=0)
            o_ref[pl.ds(r0, SCAN_ROWS), cs] = (
                h * y_ref[pl.ds(r0, SCAN_ROWS), cs].astype(F32)).astype(o_ref.dtype)
            return h_bot[half - 1:half, :]

        hc_ref[0:1, cs] = lax.fori_loop(0, tr // SCAN_ROWS, body, hc_ref[0:1, cs])


def rglru(xc, y, w_a, b_a, w_x, b_x, a_param):
    m, d = xc.shape
    tr = TR_LRU
    row = pl.BlockSpec((tr, d), lambda i: (i, 0))
    vec = pl.BlockSpec((1, d), lambda i: (0, 0))
    wspec = pl.BlockSpec((LRU_BLOCKS, LRU_BDIM, LRU_BDIM), lambda i: (0, 0, 0))
    nbytes = 4 * tr * d * 4 + 2 * tr * d * 2 + 2 * tr * d * 4 + 4 * LRU_BLOCKS * LRU_BDIM * LRU_BDIM * 2
    return pl.pallas_call(
        functools.partial(_lru_kernel, tr=tr),
        out_shape=jax.ShapeDtypeStruct((m, d), BF16),
        grid=(m // tr,),
        in_specs=[row, row, wspec, wspec, vec, vec, vec],
        out_specs=row,
        scratch_shapes=[pltpu.VMEM((tr, d), F32), pltpu.VMEM((tr, d), F32),
                        pltpu.VMEM((V7X_SUBLANES, d), F32)],
        compiler_params=_params(("arbitrary",), nbytes),
        name="rglru",
    )(xc, y, w_a.astype(BF16), w_x.astype(BF16), b_a.reshape(1, d), b_x.reshape(1, d),
      a_param.reshape(1, d))


def _hgrn_kernel(q_ref, f_ref, i_ref, g_ref, lbl_ref, gnw_ref, o_ref, st_ref, *, tr, layer_j):
    step = pl.program_id(1)

    @pl.when(step == 0)
    def _():
        st_ref[...] = jnp.zeros_like(st_ref)

    logits = lbl_ref[...]
    e = jnp.exp(logits - jnp.max(logits, axis=0, keepdims=True))
    lb = jnp.sum(e[0:layer_j + 1], axis=0, keepdims=True) / jnp.sum(e, axis=0, keepdims=True)

    valid = (step * tr + _iota((tr, 1), 0)) >= PAD
    q = jax.nn.silu(q_ref[...].astype(F32))
    forget = lb + (1.0 - lb) * jax.nn.sigmoid(f_ref[...].astype(F32))
    k = jnp.where(valid, 1.0 - forget, 0.0)
    g = jnp.where(valid, jnp.log(forget), 0.0)
    v = i_ref[...]

    pos = _iota((tr, A_KDIM), 0) & (HGRN_CHUNK - 1)
    b = g
    shift = 1
    while shift < HGRN_CHUNK:
        b = b + jnp.where(pos >= shift, pltpu.roll(b, shift, 0), 0.0)
        shift *= 2

    n_sub = HGRN_CHUNK // HGRN_SUB
    cpos = _iota((HGRN_CHUNK, A_KDIM), 0)
    causal = _iota((HGRN_CHUNK, HGRN_CHUNK), 0) >= _iota((HGRN_CHUNK, HGRN_CHUNK), 1)
    gnw = gnw_ref[...]
    chunks = range(tr // HGRN_CHUNK)
    sl = [slice(c * HGRN_CHUNK, (c + 1) * HGRN_CHUNK) for c in chunks]
    bc = [b[s] for s in sl]
    qc = [q[s] for s in sl]
    kc = [k[s] for s in sl]
    vcb = [v[s].astype(BF16) for s in sl]
    b_last = [x[HGRN_CHUNK - 1:HGRN_CHUNK] for x in bc]
    u_t = [_dot_tn(vcb[c], (kc[c] * jnp.exp(b_last[c] - bc[c])).astype(BF16)) for c in chunks]
    att = []
    for c in chunks:
        rows = []
        for i in range(n_sub):
            ss = slice(i * HGRN_SUB, (i + 1) * HGRN_SUB)
            ref = jnp.zeros((1, A_KDIM), F32) if i == 0 else bc[c][i * HGRN_SUB - 1:i * HGRN_SUB]
            q_sc = (qc[c][ss] * jnp.exp(bc[c][ss] - ref)).astype(BF16)
            expo = jnp.where(cpos < (i + 1) * HGRN_SUB, ref - bc[c], 0.0)
            k_sc = (kc[c] * jnp.exp(expo)).astype(BF16)
            rows.append(_dot_nt(q_sc, k_sc))
        att.append(jnp.where(causal, jnp.concatenate(rows, axis=0), 0.0).astype(BF16))
    o_intra = [_dot(att[c], vcb[c]) for c in chunks]
    states = []
    st = st_ref[...]
    for c in chunks:
        states.append(st.astype(BF16))
        st = st * jnp.exp(b_last[c]) + u_t[c]
    st_ref[...] = st
    o_inter = [_dot_nt((qc[c] * jnp.exp(bc[c])).astype(BF16), states[c]) for c in chunks]
    for c in chunks:
        gate = jax.nn.silu(g_ref[sl[c], :].astype(F32))
        o_ref[sl[c], :] = (_rms(o_inter[c] + o_intra[c], gnw) * gate).astype(o_ref.dtype)


def hgrn2(hproj, lb_logits, gn_w, layer_j):
    m = hproj.shape[0]
    tr = TR_HGRN
    nrow = lb_logits.shape[0]

    def col(off):
        return pl.BlockSpec((tr, A_KDIM), lambda h, t, off=off: (t, h + off))

    nbytes = 2 * 4 * tr * A_KDIM * 4 + 2 * tr * A_VDIM * 2 + 12 * tr * A_KDIM * 4
    return pl.pallas_call(
        functools.partial(_hgrn_kernel, tr=tr, layer_j=layer_j),
        out_shape=jax.ShapeDtypeStruct((m, A_WIDTH), BF16),
        grid=(A_HEADS, m // tr),
        in_specs=[col(0), col(A_HEADS), col(2 * A_HEADS), col(3 * A_HEADS),
                  pl.BlockSpec((nrow, A_KDIM), lambda h, t: (0, h)),
                  pl.BlockSpec((1, A_VDIM), lambda h, t: (0, 0))],
        out_specs=pl.BlockSpec((tr, A_VDIM), lambda h, t: (t, h)),
        scratch_shapes=[pltpu.VMEM((A_VDIM, A_KDIM), F32)],
        compiler_params=_params(("arbitrary", "arbitrary"), nbytes),
        name="hgrn2",
    )(hproj, hproj, hproj, hproj, lb_logits, gn_w.reshape(1, A_VDIM))


PAIRS = B_GROUP // 2
QROWS = PAIRS * ATTN_BLOCK


def _swa_kernel(sink_ref, q_ref, kc_ref, kp_ref, km_ref, vc_ref, vp_ref, vm_ref, o_ref):
    n = pl.program_id(0)
    scale = B_HDIM ** -0.5
    lane_lo = _iota((1, V7X_LANES), 1) < B_HDIM
    tq = _iota((QROWS, ATTN_BLOCK), 0) & (ATTN_BLOCK - 1)
    sk = _iota((QROWS, ATTN_BLOCK), 1)
    masks = (sk > tq + jnp.where(n >= 2, 0, 2 * ATTN_BLOCK),
             sk <= tq + jnp.where(n >= 1, 0, -2 * ATTN_BLOCK),
             jnp.logical_and(sk >= PAD, sk <= tq + jnp.where(n >= 1, ATTN_BLOCK, 0)))

    def split(x, natural_lo, fill):
        rolled = pltpu.roll(x, B_HDIM, 1)
        lo_src, hi_src = (x, rolled) if natural_lo else (rolled, x)
        return (jnp.where(lane_lo, lo_src, fill).astype(BF16), jnp.where(lane_lo, fill, hi_src).astype(BF16))

    chains = [(h, par) for h in range(B_KVHEADS) for par in range(2)]
    keys, vals, qs = {}, {}, {}
    for h in range(B_KVHEADS):
        tile = slice((h // 2) * V7X_LANES, (h // 2 + 1) * V7X_LANES)
        nat = h % 2 == 0
        keys[h] = [split(r[:, tile].astype(F32), nat, 0.0) for r in (kp_ref, kc_ref, km_ref)]
        vals[h] = [split(r[:, tile].astype(F32), nat, 1.0) for r in (vp_ref, vc_ref, vm_ref)]
        qs[h] = (jnp.concatenate(
            [q_ref[:, (h * PAIRS + p) * V7X_LANES:(h * PAIRS + p + 1) * V7X_LANES] for p in range(PAIRS)],
            axis=0).astype(F32) * scale).astype(BF16)
    logits = {c: [jnp.where(m, _dot_nt(qs[c[0]], kk[c[1]]), NEG_INF) for m, kk in zip(masks, keys[c[0]])]
              for c in chains}
    sinks = {(h, par): jnp.concatenate(
        [jnp.full((ATTN_BLOCK, 1), sink_ref[h * B_GROUP + 2 * p + par], F32) for p in range(PAIRS)], axis=0)
        for h, par in chains}
    mx = {c: jnp.maximum(jnp.max(jnp.maximum(jnp.maximum(logits[c][0], logits[c][1]), logits[c][2]),
                                 axis=-1, keepdims=True), sinks[c]) for c in chains}
    pv = {}
    for c in chains:
        for lg, vv in zip(logits[c], vals[c[0]]):
            part = _dot(jnp.exp(lg - mx[c]).astype(BF16), vv[c[1]])
            pv[c] = part if c not in pv else pv[c] + part
    out = {}
    for c in chains:
        den = pltpu.roll(pv[c], B_HDIM, 1) + jnp.exp(sinks[c] - mx[c])
        own_half = lane_lo if c[1] == 0 else jnp.logical_not(lane_lo)
        part = jnp.where(own_half, pv[c] / den, 0.0)
        out[c[0]] = part if c[0] not in out else out[c[0]] + part
    for h in range(B_KVHEADS):
        for p in range(PAIRS):
            o_ref[:, (h * PAIRS + p) * V7X_LANES:(h * PAIRS + p + 1) * V7X_LANES] = (
                out[h][p * ATTN_BLOCK:(p + 1) * ATTN_BLOCK].astype(o_ref.dtype))


def swa(hproj, sinks):
    m = hproj.shape[0]
    blk = ATTN_BLOCK
    q_col = (2 * A_FDIM + 2 * A_WIDTH) // B_WIDTH
    k_col = (2 * A_FDIM + 2 * A_WIDTH + B_WIDTH) // B_KVWIDTH
    v_col = k_col + 1
    qspec = pl.BlockSpec((blk, B_WIDTH), lambda n: (n, q_col))

    def kv(col, which):
        if which == "cur":
            return pl.BlockSpec((blk, B_KVWIDTH), lambda n: (n, col))
        if which == "prev":
            return pl.BlockSpec((blk, B_KVWIDTH), lambda n: (jnp.maximum(n - 1, 0), col))
        return pl.BlockSpec((blk, B_KVWIDTH), lambda n: (0, col))

    nbytes = 2 * blk * B_WIDTH * 4 + 12 * blk * B_KVWIDTH * 4 + 2 * blk * B_WIDTH * 2 + 40 * QROWS * 128 * 4
    return pl.pallas_call(
        _swa_kernel,
        out_shape=jax.ShapeDtypeStruct((m, B_WIDTH), BF16),
        grid=(m // blk,),
        in_specs=[pl.BlockSpec(memory_space=pltpu.SMEM), qspec,
                  kv(k_col, "cur"), kv(k_col, "prev"), kv(k_col, "meta"),
                  kv(v_col, "cur"), kv(v_col, "prev"), kv(v_col, "meta")],
        out_specs=pl.BlockSpec((blk, B_WIDTH), lambda n: (n, 0)),
        compiler_params=_params(("arbitrary",), nbytes),
        name="swa",
    )(sinks, hproj, hproj, hproj, hproj, hproj, hproj, hproj)


def kernel(x, meta_tokens, norm_w, w_in_even, lb_logits, hgrn_gn_w, attn_sinks, w_out_even,
           w_in_odd, lru_conv_w, lru_conv_b, lru_wa, lru_ba, lru_wx, lru_bx, lru_a_param, w_out_odd,
           ffn_w_gu, ffn_conv_w, ffn_conv_b, ffn_w_down):
    assert x.shape == (1, SEQ, D_MODEL) and norm_w.shape[0] == DEPTH
    x2d = x[0]
    xn = rms_cast_stream(x2d, meta_tokens, norm_w[0, 0])
    h, out = None, None
    for layer in range(DEPTH):
        j = layer // 2
        if layer % 2 == 0:
            hproj = matmul([xn], w_in_even, layer=j, tm=TM_IN_EVEN, tn=TN_IN_EVEN, out_dtype=BF16,
                           name="in_even")
            o_a = hgrn2(hproj, lb_logits, hgrn_gn_w[j], j)
            o_b = swa(hproj, attn_sinks[j])
            mix = matmul([o_a, o_b], w_out_even, layer=j, tm=TM_OUT, tn=TN_OUT, out_dtype=BF16,
                         name="out_even")
        else:
            y_br, x_br = in_odd(xn, w_in_odd, lru_conv_w, lru_conv_b, j)
            rec = rglru(x_br, y_br, lru_wa[j], lru_ba[j], lru_wx[j], lru_bx[j], lru_a_param[j])
            mix = matmul([rec], w_out_odd, layer=j, tm=TM_OUT, tn=TN_OUT, out_dtype=BF16, name="out_odd")
        if h is None:
            h, xn = resid_norm_stream(x2d, meta_tokens, mix, norm_w[layer, 1], norm_w[layer, 2])
        else:
            h, xn = resid_norm(h, mix, norm_w[layer, 1], norm_w[layer, 2])
        act, w_down_bf16 = ffn_up(xn, ffn_w_gu, ffn_conv_w, ffn_conv_b, ffn_w_down, layer)
        ff = matmul([act], w_down_bf16, tm=TM_DOWN, tn=TN_DOWN, out_dtype=BF16, name="ffn_down")
        if layer + 1 < DEPTH:
            h, xn = resid_norm(h, ff, norm_w[layer, 3], norm_w[layer + 1, 0])
        else:
            out = resid_final(h, ff, norm_w[layer, 3])
    return out[None]
```

```python
import functools

import jax
import jax.numpy as jnp
from jax import lax
from jax.experimental import pallas as pl
from jax.experimental.pallas import tpu as pltpu

F32 = jnp.float32
BF16 = jnp.bfloat16

D_MODEL = 4096
SEQ = 8192
DEPTH = 2
N_META = 16
A_HEADS = 16
A_KDIM = 128
A_VDIM = D_MODEL // 2 // A_HEADS
A_FDIM = A_HEADS * A_KDIM
A_WIDTH = A_HEADS * A_VDIM
HGRN_CHUNK = 64
HGRN_SUB = 16
B_HDIM = 64
B_QHEADS = D_MODEL // 2 // B_HDIM
B_KVHEADS = B_QHEADS // 8
B_GROUP = B_QHEADS // B_KVHEADS
B_WIDTH = B_QHEADS * B_HDIM
B_KVWIDTH = B_KVHEADS * B_HDIM
WINDOW = 128
ATTN_BLOCK = 128
EVEN_IN = 2 * A_FDIM + 2 * A_WIDTH + B_WIDTH + 2 * B_KVWIDTH
LRU_WIDTH = D_MODEL
LRU_BLOCKS = 16
LRU_BDIM = LRU_WIDTH // LRU_BLOCKS
LRU_CONV = 4
LRU_C = 8.0
D_FF = 256 * ((8 * D_MODEL // 3 + 255) // 256)
FFN_CONV = 3
NORM_EPS = 1e-6
NEG_INF = -1e30

PAD = ATTN_BLOCK - N_META
ROW0 = PAD + N_META
LP = ROW0 + SEQ
assert PAD % HGRN_CHUNK == HGRN_CHUNK - N_META and ROW0 == ATTN_BLOCK
assert WINDOW == ATTN_BLOCK

V7X_LANES = 128
V7X_SUBLANES = 8
V7X_VMEM_LIMIT_CAP = 60 * 1024 * 1024
CARRY_ROWS = V7X_SUBLANES

TM_MM = 1040
TM_IN_EVEN = 1040
TN_IN_EVEN = 512
TM_OUT = 520
TN_OUT = 1024
TM_DOWN = 520
TN_DOWN = 512
TF_FFN = 256
TN_ODD = 256
EPI_ROWS = 80
K_CHUNK = 256
MAX_WEIGHT_SLABS = 8
TR_NORM = 320
TR_LRU = 208
CW_LRU = 1024
TR_HGRN = 640
HGRN_HEADS_PER_STEP = 4
SCAN_ROWS = 16


def _vmem_limit(nbytes):
    return int(min(V7X_VMEM_LIMIT_CAP, nbytes * 1.15 + (4 << 20)))


def _params(sem, nbytes):
    return pltpu.CompilerParams(dimension_semantics=sem, vmem_limit_bytes=_vmem_limit(nbytes))


def _rms(x, w):
    x = x.astype(F32)
    return x * lax.rsqrt(jnp.mean(x * x, axis=-1, keepdims=True) + NORM_EPS) * w


def _iota(shape, dim):
    return lax.broadcasted_iota(jnp.int32, shape, dim)


def _dot(a, b):
    return jnp.dot(a, b, preferred_element_type=F32)


def _dot_nt(a, b):
    return lax.dot_general(a, b, (((1,), (1,)), ((), ())), preferred_element_type=F32)


def _dot_tn(a, b):
    return lax.dot_general(a, b, (((0,), (0,)), ((), ())), preferred_element_type=F32)


def _stream_rows(n, x_ref, meta_ref):
    first = jnp.concatenate([jnp.zeros((PAD, D_MODEL), F32), meta_ref[...].astype(F32)], axis=0)
    return jnp.where(n == 0, first, x_ref[...].astype(F32))


def _stream_specs():
    blk = ATTN_BLOCK
    return [pl.BlockSpec((blk, D_MODEL), lambda n: (jnp.maximum(n - 1, 0), 0)),
            pl.BlockSpec((N_META, D_MODEL), lambda n: (0, 0))]


def _rms_cast_stream_kernel(x_ref, meta_ref, w_ref, o_ref):
    h = _stream_rows(pl.program_id(0), x_ref, meta_ref)
    o_ref[...] = _rms(h, w_ref[...]).astype(o_ref.dtype)


def rms_cast_stream(x2d, meta, w):
    blk, d = ATTN_BLOCK, D_MODEL
    return pl.pallas_call(
        _rms_cast_stream_kernel,
        out_shape=jax.ShapeDtypeStruct((LP, d), BF16),
        grid=(LP // blk,),
        in_specs=_stream_specs() + [pl.BlockSpec((1, d), lambda n: (0, 0))],
        out_specs=pl.BlockSpec((blk, d), lambda n: (n, 0)),
        compiler_params=_params(("arbitrary",), 2 * blk * d * 6 + 4 * blk * d * 4),
        name="rms_cast_stream",
    )(x2d, meta, w.reshape(1, d))


def _resid_norm_stream_kernel(x_ref, meta_ref, y_ref, wp_ref, wn_ref, ho_ref, xo_ref):
    h = _stream_rows(pl.program_id(0), x_ref, meta_ref) + _rms(y_ref[...], wp_ref[...])
    ho_ref[...] = h
    xo_ref[...] = _rms(h, wn_ref[...]).astype(xo_ref.dtype)


def resid_norm_stream(x2d, meta, y, w_post, w_next):
    blk, d = ATTN_BLOCK, D_MODEL
    row = pl.BlockSpec((blk, d), lambda n: (n, 0))
    vec = pl.BlockSpec((1, d), lambda n: (0, 0))
    return pl.pallas_call(
        _resid_norm_stream_kernel,
        out_shape=(jax.ShapeDtypeStruct((LP, d), F32), jax.ShapeDtypeStruct((LP, d), BF16)),
        grid=(LP // blk,),
        in_specs=_stream_specs() + [row, vec, vec],
        out_specs=(row, row),
        compiler_params=_params(("arbitrary",), 2 * blk * d * 14 + 4 * blk * d * 4),
        name="resid_norm_stream",
    )(x2d, meta, y, w_post.reshape(1, d), w_next.reshape(1, d))


def _resid_norm_kernel(h_ref, y_ref, wp_ref, wn_ref, ho_ref, xo_ref):
    h = h_ref[...] + _rms(y_ref[...], wp_ref[...])
    ho_ref[...] = h
    xo_ref[...] = _rms(h, wn_ref[...]).astype(xo_ref.dtype)


def resid_norm(h, y, w_post, w_next):
    n, d = h.shape
    tr = TR_NORM
    row = pl.BlockSpec((tr, d), lambda i: (i, 0))
    vec = pl.BlockSpec((1, d), lambda i: (0, 0))
    return pl.pallas_call(
        _resid_norm_kernel,
        out_shape=(jax.ShapeDtypeStruct((n, d), F32), jax.ShapeDtypeStruct((n, d), BF16)),
        grid=(n // tr,),
        in_specs=[row, row, vec, vec],
        out_specs=(row, row),
        compiler_params=_params(("arbitrary",), 2 * tr * d * 14),
        name="resid_norm",
    )(h, y, w_post.reshape(1, d), w_next.reshape(1, d))


def _resid_final_kernel(h_ref, y_ref, wp_ref, o_ref):
    o_ref[...] = h_ref[...] + _rms(y_ref[...], wp_ref[...])


def resid_final(h, y, w_post):
    n, d = h.shape
    tr = ATTN_BLOCK
    off = ROW0 // tr
    src = pl.BlockSpec((tr, d), lambda i: (i + off, 0))
    return pl.pallas_call(
        _resid_final_kernel,
        out_shape=jax.ShapeDtypeStruct((n - ROW0, d), F32),
        grid=((n - ROW0) // tr,),
        in_specs=[src, src, pl.BlockSpec((1, d), lambda i: (0, 0))],
        out_specs=pl.BlockSpec((tr, d), lambda i: (i, 0)),
        compiler_params=_params(("arbitrary",), 2 * tr * d * 12),
        name="resid_final",
    )(h, y, w_post.reshape(1, d))


def _slab_specs(w, layer, k, tn, n_slabs, col_of):
    rows = k // n_slabs
    assert rows * n_slabs == k

    def spec(s):
        if w.ndim == 3:
            return pl.BlockSpec((None, rows, tn), lambda *g: (layer, s, col_of(*g, s)))
        return pl.BlockSpec((rows, tn), lambda *g: (s, col_of(*g, s)))

    return [spec(s) for s in range(n_slabs)]


def _cast_slabs(slab_refs, dst_ref):
    rows = slab_refs[0].shape[0]
    for s, ref in enumerate(slab_refs):
        dst_ref[s * rows:(s + 1) * rows, :] = ref[...].astype(BF16)


def _mm_kernel(*refs, n_lhs, n_slabs, cast):
    x_refs, w_refs, o_ref = refs[:n_lhs], refs[n_lhs:n_lhs + n_slabs], refs[n_lhs + n_slabs]
    if cast:
        w_ref = refs[n_lhs + n_slabs + 1]

        @pl.when(pl.program_id(1) == 0)
        def _():
            _cast_slabs(w_refs, w_ref)
    else:
        (w_ref,) = w_refs
    acc, k0 = None, 0
    for x_ref in x_refs:
        kk = x_ref.shape[1]
        part = _dot(x_ref[...], w_ref[k0:k0 + kk, :])
        acc = part if acc is None else acc + part
        k0 += kk
    o_ref[...] = acc.astype(o_ref.dtype)


def matmul(xs, w, *, layer=0, tm, tn, out_dtype=F32, name="matmul"):
    m = xs[0].shape[0]
    k, n = w.shape[-2:]
    assert sum(x.shape[1] for x in xs) == k
    cast = w.dtype != BF16
    ni, nj = m // tm, n // tn
    wbytes = k * tn * (4 if cast else 2)
    nbytes = 2 * tm * k * 2 + 2 * wbytes + (k * tn * 2 if cast else 0) + 2 * tm * tn * 4
    if cast and ni <= MAX_WEIGHT_SLABS:
        n_slabs = ni
        w_specs = _slab_specs(w, layer, k, tn, n_slabs,
                              lambda j, i, s: jnp.minimum(j + (i + ni - 1 - s) // ni, nj - 1))
    else:
        n_slabs = 1
        w_specs = _slab_specs(w, layer, k, tn, 1, lambda j, i, s: j)
    return pl.pallas_call(
        functools.partial(_mm_kernel, n_lhs=len(xs), n_slabs=n_slabs, cast=cast),
        out_shape=jax.ShapeDtypeStruct((m, n), out_dtype),
        grid=(nj, ni),
        in_specs=[pl.BlockSpec((tm, x.shape[1]), lambda j, i: (i, 0)) for x in xs] + w_specs,
        out_specs=pl.BlockSpec((tm, tn), lambda j, i: (i, j)),
        scratch_shapes=[pltpu.VMEM((k, tn), BF16)] if cast else [],
        compiler_params=_params(("arbitrary", "arbitrary"), nbytes),
        name=name,
    )(*xs, *([w] * n_slabs))


def _tile_maps(ni, nj):
    nt = ni * nj

    def cur_row(t):
        return lax.rem(jnp.minimum(t, nt - 1), ni)

    def prev_row(t):
        return lax.rem(jnp.maximum(t - 1, 0), ni)

    def prev_col(t):
        return lax.div(jnp.maximum(t - 1, 0), ni)

    return nt, cur_row, prev_row, prev_col


def _interleaved_projections(x_ref, w_refs, dst_stores, epilogue_chunk, tm):
    nk = x_ref.shape[1] // K_CHUNK
    chunk_rows = list(range(0, tm, EPI_ROWS))
    slots, slot, done = len(w_refs) * nk, 0, 0
    for w_ref, store in zip(w_refs, dst_stores):
        acc = None
        for kk in range(nk):
            ks = slice(kk * K_CHUNK, (kk + 1) * K_CHUNK)
            part = _dot(x_ref[:, ks], w_ref[ks, :])
            acc = part if acc is None else acc + part
            slot += 1
            while done < len(chunk_rows) * slot // slots:
                epilogue_chunk(chunk_rows[done])
                done += 1
        store(acc)


def _causal_conv(buf_ref, cw_ref, cb_ref, width, r0, rows):
    win = buf_ref[r0:r0 + CARRY_ROWS + rows, :]
    conv = cb_ref[...]
    for tap in range(width):
        back = width - 1 - tap
        shifted = pltpu.roll(win, back, 0) if back else win
        conv = conv + shifted[CARRY_ROWS:, :] * cw_ref[tap:tap + 1, :]
    return conv


def _ffn_up_kernel(*refs, tm, ni, nt):
    x_ref, wg_refs, wu_refs = refs[0], refs[1:1 + ni], refs[1 + ni:1 + 2 * ni]
    (cw_ref, cb_ref, wd_ref, o_ref, wdb_ref,
     wgb_ref, wub_ref, g0_ref, g1_ref, u0_ref, u1_ref) = refs[1 + 2 * ni:]
    t = pl.program_id(0)
    i = lax.rem(t, ni)

    @pl.when(t == 0)
    def _():
        g1_ref[...] = jnp.zeros_like(g1_ref)
        u1_ref[...] = jnp.zeros_like(u1_ref)

    @pl.when(jnp.logical_and(i == 0, t < nt))
    def _():
        _cast_slabs(wg_refs, wgb_ref)
        _cast_slabs(wu_refs, wub_ref)

    wdb_ref[...] = wd_ref[...].astype(BF16)

    def step(g_cur, u_cur, g_prev, u_prev):
        g_cur[0:CARRY_ROWS, :] = jnp.where(i == 0, 0.0, g_prev[tm:tm + CARRY_ROWS, :])

        def epilogue_chunk(r0):
            conv = _causal_conv(g_prev, cw_ref, cb_ref, FFN_CONV, r0, EPI_ROWS)
            o_ref[r0:r0 + EPI_ROWS, :] = (jax.nn.gelu(conv, approximate=True)
                                          * u_prev[r0:r0 + EPI_ROWS, :]).astype(o_ref.dtype)

        def store_gate(acc):
            g_cur[CARRY_ROWS:CARRY_ROWS + tm, :] = acc

        def store_up(acc):
            u_cur[...] = acc

        _interleaved_projections(x_ref, (wgb_ref, wub_ref), (store_gate, store_up), epilogue_chunk, tm)

    @pl.when(lax.rem(t, 2) == 0)
    def _():
        step(g0_ref, u0_ref, g1_ref, u1_ref)

    @pl.when(lax.rem(t, 2) == 1)
    def _():
        step(g1_ref, u1_ref, g0_ref, u0_ref)


def ffn_up(xn, w_gu, conv_w, conv_b, w_down, layer):
    m, k = xn.shape
    tm, tf = TM_MM, TF_FFN
    ni, nj = m // tm, D_FF // tf
    nt, cur_row, prev_row, prev_col = _tile_maps(ni, nj)

    def slab_col(t, s):
        return jnp.minimum(lax.div(jnp.minimum(t, nt - 1) + ni - 1 - s, ni), nj - 1)

    wd_rows = D_FF // nt
    assert wd_rows * nt == D_FF and wd_rows % (2 * V7X_SUBLANES) == 0
    nbytes = (2 * tm * k * 2 + 4 * k * tf * 4 + 2 * k * tf * 2 + 2 * tm * tf * 2
              + 4 * (tm + CARRY_ROWS) * tf * 4 + 2 * wd_rows * D_MODEL * 6)
    return pl.pallas_call(
        functools.partial(_ffn_up_kernel, tm=tm, ni=ni, nt=nt),
        out_shape=(jax.ShapeDtypeStruct((m, D_FF), BF16), jax.ShapeDtypeStruct((D_FF, D_MODEL), BF16)),
        grid=(nt + 1,),
        in_specs=[pl.BlockSpec((tm, k), lambda t: (cur_row(t), 0))]
        + _slab_specs(w_gu, layer, k, tf, ni, slab_col)
        + _slab_specs(w_gu, layer, k, tf, ni, lambda t, s: slab_col(t, s) + nj)
        + [pl.BlockSpec((None, FFN_CONV, tf), lambda t: (layer, 0, prev_col(t))),
                  pl.BlockSpec((None, 1, tf), lambda t: (layer, 0, prev_col(t))),
                  pl.BlockSpec((None, wd_rows, D_MODEL), lambda t: (layer, jnp.minimum(t, nt - 1), 0))],
        out_specs=(pl.BlockSpec((tm, tf), lambda t: (prev_row(t), prev_col(t))),
                   pl.BlockSpec((wd_rows, D_MODEL), lambda t: (jnp.minimum(t, nt - 1), 0))),
        scratch_shapes=[pltpu.VMEM((k, tf), BF16), pltpu.VMEM((k, tf), BF16),
                        pltpu.VMEM((tm + CARRY_ROWS, tf), F32), pltpu.VMEM((tm + CARRY_ROWS, tf), F32),
                        pltpu.VMEM((tm, tf), F32), pltpu.VMEM((tm, tf), F32)],
        compiler_params=_params(("arbitrary",), nbytes),
        name="ffn_up",
    )(xn, *([w_gu] * (2 * ni)), conv_w, conv_b.reshape(conv_b.shape[0], 1, D_FF), w_down)


def _in_odd_kernel(*refs, tm, ni, nt):
    x_ref, wy_refs, wx_refs = refs[0], refs[1:1 + ni], refs[1 + ni:1 + 2 * ni]
    (cw_ref, cb_ref, y_ref, xc_ref,
     wyb_ref, wxb_ref, b0_ref, b1_ref, r0_ref, r1_ref) = refs[1 + 2 * ni:]
    t = pl.program_id(0)
    i = lax.rem(t, ni)

    @pl.when(t == 0)
    def _():
        b1_ref[...] = jnp.zeros_like(b1_ref)
        r1_ref[...] = jnp.zeros_like(r1_ref)

    @pl.when(jnp.logical_and(i == 0, t < nt))
    def _():
        _cast_slabs(wy_refs, wyb_ref)
        _cast_slabs(wx_refs, wxb_ref)

    def step(b_cur, r_cur, b_prev, r_prev):
        b_cur[0:CARRY_ROWS, :] = jnp.where(i == 0, 0.0, b_prev[tm:tm + CARRY_ROWS, :])

        def epilogue_chunk(r0):
            y_ref[r0:r0 + EPI_ROWS, :] = jax.nn.gelu(r_prev[r0:r0 + EPI_ROWS, :],
                                                     approximate=True).astype(y_ref.dtype)
            xc_ref[r0:r0 + EPI_ROWS, :] = _causal_conv(b_prev, cw_ref, cb_ref, LRU_CONV, r0, EPI_ROWS)

        def store_y(acc):
            r_cur[...] = acc

        def store_x(acc):
            b_cur[CARRY_ROWS:CARRY_ROWS + tm, :] = acc

        _interleaved_projections(x_ref, (wyb_ref, wxb_ref), (store_y, store_x), epilogue_chunk, tm)

    @pl.when(lax.rem(t, 2) == 0)
    def _():
        step(b0_ref, r0_ref, b1_ref, r1_ref)

    @pl.when(lax.rem(t, 2) == 1)
    def _():
        step(b1_ref, r1_ref, b0_ref, r0_ref)


def in_odd(xn, w_in, conv_w, conv_b, layer):
    m, k = xn.shape
    tm, tn = TM_MM, TN_ODD
    ni, nj = m // tm, LRU_WIDTH // tn
    nt, cur_row, prev_row, prev_col = _tile_maps(ni, nj)
    nbytes = (2 * tm * k * 2 + 4 * k * tn * 4 + 2 * k * tn * 2 + 4 * tm * tn * 4
              + 4 * (tm + CARRY_ROWS) * tn * 4)

    def slab_col(t, s):
        return jnp.minimum(lax.div(jnp.minimum(t, nt - 1) + ni - 1 - s, ni), nj - 1)

    out = pl.BlockSpec((tm, tn), lambda t: (prev_row(t), prev_col(t)))
    return pl.pallas_call(
        functools.partial(_in_odd_kernel, tm=tm, ni=ni, nt=nt),
        out_shape=(jax.ShapeDtypeStruct((m, LRU_WIDTH), BF16), jax.ShapeDtypeStruct((m, LRU_WIDTH), F32)),
        grid=(nt + 1,),
        in_specs=[pl.BlockSpec((tm, k), lambda t: (cur_row(t), 0))]
        + _slab_specs(w_in, layer, k, tn, ni, slab_col)
        + _slab_specs(w_in, layer, k, tn, ni, lambda t, s: slab_col(t, s) + nj)
        + [pl.BlockSpec((None, LRU_CONV, tn), lambda t: (layer, 0, prev_col(t))),
                  pl.BlockSpec((None, 1, tn), lambda t: (layer, 0, prev_col(t)))],
        out_specs=(out, out),
        scratch_shapes=[pltpu.VMEM((k, tn), BF16), pltpu.VMEM((k, tn), BF16),
                        pltpu.VMEM((tm + CARRY_ROWS, tn), F32), pltpu.VMEM((tm + CARRY_ROWS, tn), F32),
                        pltpu.VMEM((tm, tn), F32), pltpu.VMEM((tm, tn), F32)],
        compiler_params=_params(("arbitrary",), nbytes),
        name="in_odd",
    )(xn, *([w_in] * (2 * ni)), conv_w, conv_b.reshape(conv_b.shape[0], 1, LRU_WIDTH))


def _lru_kernel(xc_ref, y_ref, wa_ref, wx_ref, ba_ref, bx_ref, ap_ref, o_ref, a_ref, b_ref, hc_ref, *, tr):
    step = pl.program_id(0)

    @pl.when(step == 0)
    def _():
        hc_ref[...] = jnp.zeros_like(hc_ref)

    valid = (step * tr + _iota((tr, 1), 0)) >= PAD
    for blk in range(LRU_BLOCKS):
        sl = slice(blk * LRU_BDIM, (blk + 1) * LRU_BDIM)
        x = xc_ref[:, sl]
        xb = x.astype(BF16)
        gate_r = 0.5 * jnp.tanh(0.5 * (_dot(xb, wa_ref[blk]) + ba_ref[:, sl])) + 0.5
        gate_i = 0.5 * jnp.tanh(0.5 * (_dot(xb, wx_ref[blk]) + bx_ref[:, sl])) + 0.5
        log_a = gate_r * (LRU_C * jax.nn.log_sigmoid(ap_ref[:, sl]))
        th = jnp.tanh(0.5 * log_a)
        em = 2.0 * th / (1.0 - th)
        a_ref[:, sl] = 1.0 + em
        inp = jnp.sqrt(-em * (2.0 + em)) * (gate_i * x)
        b_ref[:, sl] = jnp.where(valid, inp, 0.0)

    half = SCAN_ROWS // 2
    ridx = _iota((half, CW_LRU), 0)

    def local_scan(a, b):
        shift = 1
        while shift < half:
            ok = ridx >= shift
            b = jnp.where(ok, a * pltpu.roll(b, shift, 0) + b, b)
            a = jnp.where(ok, a * pltpu.roll(a, shift, 0), a)
            shift *= 2
        return a, b

    for c in range(LRU_WIDTH // CW_LRU):
        cs = slice(c * CW_LRU, (c + 1) * CW_LRU)

        def body(g, carry, cs=cs):
            r0 = pl.multiple_of(g * SCAN_ROWS, SCAN_ROWS)
            a = a_ref[pl.ds(r0, SCAN_ROWS), cs]
            b = b_ref[pl.ds(r0, SCAN_ROWS), cs]
            a_top, b_top = local_scan(a[:half], b[:half])
            a_bot, b_bot = local_scan(a[half:], b[half:])
            h_top = a_top * carry + b_top
            h_bot = a_bot * h_top[half - 1:half, :] + b_bot
            h = jnp.concatenate([h_top, h_bot], axis=0)
            o_ref[pl.ds(r0, SCAN_ROWS), cs] = (
                h * y_ref[pl.ds(r0, SCAN_ROWS), cs].astype(F32)).astype(o_ref.dtype)
            return h_bot[half - 1:half, :]

        hc_ref[0:1, cs] = lax.fori_loop(0, tr // SCAN_ROWS, body, hc_ref[0:1, cs])


def rglru(xc, y, w_a, b_a, w_x, b_x, a_param):
    m, d = xc.shape
    tr = TR_LRU
    row = pl.BlockSpec((tr, d), lambda i: (i, 0))
    vec = pl.BlockSpec((1, d), lambda i: (0, 0))
    wspec = pl.BlockSpec((LRU_BLOCKS, LRU_BDIM, LRU_BDIM), lambda i: (0, 0, 0))
    nbytes = 4 * tr * d * 4 + 2 * tr * d * 2 + 2 * tr * d * 4 + 4 * LRU_BLOCKS * LRU_BDIM * LRU_BDIM * 2
    return pl.pallas_call(
        functools.partial(_lru_kernel, tr=tr),
        out_shape=jax.ShapeDtypeStruct((m, d), BF16),
        grid=(m // tr,),
        in_specs=[row, row, wspec, wspec, vec, vec, vec],
        out_specs=row,
        scratch_shapes=[pltpu.VMEM((tr, d), F32), pltpu.VMEM((tr, d), F32),
                        pltpu.VMEM((V7X_SUBLANES, d), F32)],
        compiler_params=_params(("arbitrary",), nbytes),
        name="rglru",
    )(xc, y, w_a.astype(BF16), w_x.astype(BF16), b_a.reshape(1, d), b_x.reshape(1, d),
      a_param.reshape(1, d))


def _hgrn_kernel(q_ref, f_ref, i_ref, g_ref, lbl_ref, gnw_ref, o_ref, st_ref, *, tr, layer_j):
    step = pl.program_id(1)

    @pl.when(step == 0)
    def _():
        st_ref[...] = jnp.zeros_like(st_ref)

    logits = lbl_ref[...]
    e = jnp.exp(logits - jnp.max(logits, axis=0, keepdims=True))
    lb = jnp.sum(e[0:layer_j + 1], axis=0, keepdims=True) / jnp.sum(e, axis=0, keepdims=True)

    width = HGRN_HEADS_PER_STEP * A_KDIM
    valid = (step * tr + _iota((tr, 1), 0)) >= PAD
    q = jax.nn.silu(q_ref[...].astype(F32))
    forget = lb + (1.0 - lb) * jax.nn.sigmoid(f_ref[...].astype(F32))
    k = jnp.where(valid, 1.0 - forget, 0.0)
    g = jnp.where(valid, jnp.log(forget), 0.0)
    v = i_ref[...]

    pos = _iota((tr, width), 0) & (HGRN_CHUNK - 1)
    b = g
    shift = 1
    while shift < HGRN_CHUNK:
        b = b + jnp.where(pos >= shift, pltpu.roll(b, shift, 0), 0.0)
        shift *= 2

    n_sub = HGRN_CHUNK // HGRN_SUB
    cpos = _iota((HGRN_CHUNK, A_KDIM), 0)
    causal = _iota((HGRN_CHUNK, HGRN_CHUNK), 0) >= _iota((HGRN_CHUNK, HGRN_CHUNK), 1)
    gnw = gnw_ref[...]
    heads = range(HGRN_HEADS_PER_STEP)
    chunks = range(tr // HGRN_CHUNK)
    pairs = [(hh, c) for hh in heads for c in chunks]

    def part(x, p):
        hh, c = p
        return x[c * HGRN_CHUNK:(c + 1) * HGRN_CHUNK, hh * A_KDIM:(hh + 1) * A_KDIM]

    bc = {p: part(b, p) for p in pairs}
    qc = {p: part(q, p) for p in pairs}
    kc = {p: part(k, p) for p in pairs}
    vcb = {p: part(v, p).astype(BF16) for p in pairs}
    b_last = {p: bc[p][HGRN_CHUNK - 1:HGRN_CHUNK] for p in pairs}
    u_t = {p: _dot_tn(vcb[p], (kc[p] * jnp.exp(b_last[p] - bc[p])).astype(BF16)) for p in pairs}
    att = {}
    for p in pairs:
        rows = []
        for i in range(n_sub):
            ss = slice(i * HGRN_SUB, (i + 1) * HGRN_SUB)
            ref = jnp.zeros((1, A_KDIM), F32) if i == 0 else bc[p][i * HGRN_SUB - 1:i * HGRN_SUB]
            q_sc = (qc[p][ss] * jnp.exp(bc[p][ss] - ref)).astype(BF16)
            expo = jnp.where(cpos < (i + 1) * HGRN_SUB, ref - bc[p], 0.0)
            k_sc = (kc[p] * jnp.exp(expo)).astype(BF16)
            rows.append(_dot_nt(q_sc, k_sc))
        att[p] = jnp.where(causal, jnp.concatenate(rows, axis=0), 0.0).astype(BF16)
    o_intra = {p: _dot(att[p], vcb[p]) for p in pairs}
    states = {}
    for hh in heads:
        st = st_ref[hh]
        for c in chunks:
            states[(hh, c)] = st.astype(BF16)
            st = st * jnp.exp(b_last[(hh, c)]) + u_t[(hh, c)]
        st_ref[hh] = st
    o_inter = {p: _dot_nt((qc[p] * jnp.exp(bc[p])).astype(BF16), states[p]) for p in pairs}
    for hh, c in pairs:
        rs = slice(c * HGRN_CHUNK, (c + 1) * HGRN_CHUNK)
        cs = slice(hh * A_VDIM, (hh + 1) * A_VDIM)
        gate = jax.nn.silu(g_ref[rs, cs].astype(F32))
        o_ref[rs, cs] = (_rms(o_inter[(hh, c)] + o_intra[(hh, c)], gnw) * gate).astype(o_ref.dtype)


def hgrn2(hproj, lb_logits, gn_w, layer_j):
    m = hproj.shape[0]
    tr = TR_HGRN
    nrow = lb_logits.shape[0]
    hps = HGRN_HEADS_PER_STEP
    groups = A_HEADS // hps

    def col(off):
        return pl.BlockSpec((tr, hps * A_KDIM), lambda h, t, off=off: (t, h + off * groups))

    nbytes = 2 * 4 * tr * hps * A_KDIM * 2 + 2 * tr * hps * A_VDIM * 2 + 16 * tr * hps * A_KDIM * 4
    return pl.pallas_call(
        functools.partial(_hgrn_kernel, tr=tr, layer_j=layer_j),
        out_shape=jax.ShapeDtypeStruct((m, A_WIDTH), BF16),
        grid=(groups, m // tr),
        in_specs=[col(0), col(1), col(2), col(3),
                  pl.BlockSpec((nrow, hps * A_KDIM), lambda h, t: (0, h)),
                  pl.BlockSpec((1, A_VDIM), lambda h, t: (0, 0))],
        out_specs=pl.BlockSpec((tr, hps * A_VDIM), lambda h, t: (t, h)),
        scratch_shapes=[pltpu.VMEM((hps, A_VDIM, A_KDIM), F32)],
        compiler_params=_params(("arbitrary", "arbitrary"), nbytes),
        name="hgrn2",
    )(hproj, hproj, hproj, hproj, lb_logits, gn_w.reshape(1, A_VDIM))


PAIRS = B_GROUP // 2
QROWS = PAIRS * ATTN_BLOCK


def _swa_kernel(sink_ref, q_ref, kc_ref, kp_ref, km_ref, vc_ref, vp_ref, vm_ref, o_ref):
    n = pl.program_id(0)
    scale = B_HDIM ** -0.5
    lane_lo = _iota((1, V7X_LANES), 1) < B_HDIM
    tq = _iota((QROWS, ATTN_BLOCK), 0) & (ATTN_BLOCK - 1)
    sk = _iota((QROWS, ATTN_BLOCK), 1)
    masks = (sk > tq + jnp.where(n >= 2, 0, 2 * ATTN_BLOCK),
             sk <= tq + jnp.where(n >= 1, 0, -2 * ATTN_BLOCK),
             jnp.logical_and(sk >= PAD, sk <= tq + jnp.where(n >= 1, ATTN_BLOCK, 0)))

    def split(x, natural_lo, fill):
        rolled = pltpu.roll(x, B_HDIM, 1)
        lo_src, hi_src = (x, rolled) if natural_lo else (rolled, x)
        return (jnp.where(lane_lo, lo_src, fill).astype(BF16), jnp.where(lane_lo, fill, hi_src).astype(BF16))

    chains = [(h, par) for h in range(B_KVHEADS) for par in range(2)]
    keys, vals, qs = {}, {}, {}
    for h in range(B_KVHEADS):
        tile = slice((h // 2) * V7X_LANES, (h // 2 + 1) * V7X_LANES)
        nat = h % 2 == 0
        keys[h] = [split(r[:, tile].astype(F32), nat, 0.0) for r in (kp_ref, kc_ref, km_ref)]
        vals[h] = [split(r[:, tile].astype(F32), nat, 1.0) for r in (vp_ref, vc_ref, vm_ref)]
        qs[h] = (jnp.concatenate(
            [q_ref[:, (h * PAIRS + p) * V7X_LANES:(h * PAIRS + p + 1) * V7X_LANES] for p in range(PAIRS)],
            axis=0).astype(F32) * scale).astype(BF16)
    logits = {c: [jnp.where(m, _dot_nt(qs[c[0]], kk[c[1]]), NEG_INF) for m, kk in zip(masks, keys[c[0]])]
              for c in chains}
    sinks = {(h, par): jnp.concatenate(
        [jnp.full((ATTN_BLOCK, 1), sink_ref[h * B_GROUP + 2 * p + par], F32) for p in range(PAIRS)], axis=0)
        for h, par in chains}
    mx = {c: jnp.maximum(jnp.max(jnp.maximum(jnp.maximum(logits[c][0], logits[c][1]), logits[c][2]),
                                 axis=-1, keepdims=True), sinks[c]) for c in chains}
    pv = {}
    for c in chains:
        for lg, vv in zip(logits[c], vals[c[0]]):
            part = _dot(jnp.exp(lg - mx[c]).astype(BF16), vv[c[1]])
            pv[c] = part if c not in pv else pv[c] + part
    out = {}
    for c in chains:
        den = pltpu.roll(pv[c], B_HDIM, 1) + jnp.exp(sinks[c] - mx[c])
        own_half = lane_lo if c[1] == 0 else jnp.logical_not(lane_lo)
        part = jnp.where(own_half, pv[c] / den, 0.0)
        out[c[0]] = part if c[0] not in out else out[c[0]] + part
    for h in range(B_KVHEADS):
        for p in range(PAIRS):
            o_ref[:, (h * PAIRS + p) * V7X_LANES:(h * PAIRS + p + 1) * V7X_LANES] = (
                out[h][p * ATTN_BLOCK:(p + 1) * ATTN_BLOCK].astype(o_ref.dtype))


def swa(hproj, sinks):
    m = hproj.shape[0]
    blk = ATTN_BLOCK
    q_col = (2 * A_FDIM + 2 * A_WIDTH) // B_WIDTH
    k_col = (2 * A_FDIM + 2 * A_WIDTH + B_WIDTH) // B_KVWIDTH
    v_col = k_col + 1
    qspec = pl.BlockSpec((blk, B_WIDTH), lambda n: (n, q_col))

    def kv(col, which):
        if which == "cur":
            return pl.BlockSpec((blk, B_KVWIDTH), lambda n: (n, col))
        if which == "prev":
            return pl.BlockSpec((blk, B_KVWIDTH), lambda n: (jnp.maximum(n - 1, 0), col))
        return pl.BlockSpec((blk, B_KVWIDTH), lambda n: (0, col))

    nbytes = 2 * blk * B_WIDTH * 4 + 12 * blk * B_KVWIDTH * 4 + 2 * blk * B_WIDTH * 2 + 40 * QROWS * 128 * 4
    return pl.pallas_call(
        _swa_kernel,
        out_shape=jax.ShapeDtypeStruct((m, B_WIDTH), BF16),
        grid=(m // blk,),
        in_specs=[pl.BlockSpec(memory_space=pltpu.SMEM), qspec,
                  kv(k_col, "cur"), kv(k_col, "prev"), kv(k_col, "meta"),
                  kv(v_col, "cur"), kv(v_col, "prev"), kv(v_col, "meta")],
        out_specs=pl.BlockSpec((blk, B_WIDTH), lambda n: (n, 0)),
        compiler_params=_params(("arbitrary",), nbytes),
        name="swa",
    )(sinks, hproj, hproj, hproj, hproj, hproj, hproj, hproj)


def kernel(x, meta_tokens, norm_w, w_in_even, lb_logits, hgrn_gn_w, attn_sinks, w_out_even,
           w_in_odd, lru_conv_w, lru_conv_b, lru_wa, lru_ba, lru_wx, lru_bx, lru_a_param, w_out_odd,
           ffn_w_gu, ffn_conv_w, ffn_conv_b, ffn_w_down):
    assert x.shape == (1, SEQ, D_MODEL) and norm_w.shape[0] == DEPTH and w_in_even.shape[-1] == EVEN_IN
    x2d = x[0]
    xn = rms_cast_stream(x2d, meta_tokens, norm_w[0, 0])
    h, out = None, None
    for layer in range(DEPTH):
        j = layer // 2
        if layer % 2 == 0:
            hproj = matmul([xn], w_in_even, layer=j, tm=TM_IN_EVEN, tn=TN_IN_EVEN, out_dtype=BF16,
                           name="in_even")
            o_a = hgrn2(hproj, lb_logits, hgrn_gn_w[j], j)
            o_b = swa(hproj, attn_sinks[j])
            mix = matmul([o_a, o_b], w_out_even, layer=j, tm=TM_OUT, tn=TN_OUT, out_dtype=BF16,
                         name="out_even")
        else:
            y_br, x_br = in_odd(xn, w_in_odd, lru_conv_w, lru_conv_b, j)
            rec = rglru(x_br, y_br, lru_wa[j], lru_ba[j], lru_wx[j], lru_bx[j], lru_a_param[j])
            mix = matmul([rec], w_out_odd, layer=j, tm=TM_OUT, tn=TN_OUT, out_dtype=BF16, name="out_odd")
        if h is None:
            h, xn = resid_norm_stream(x2d, meta_tokens, mix, norm_w[layer, 1], norm_w[layer, 2])
        else:
            h, xn = resid_norm(h, mix, norm_w[layer, 1], norm_w[layer, 2])
        act, w_down_bf16 = ffn_up(xn, ffn_w_gu, ffn_conv_w, ffn_conv_b, ffn_w_down, layer)
        ff = matmul([act], w_down_bf16, tm=TM_DOWN, tn=TN_DOWN, out_dtype=BF16, name="ffn_down")
        if layer + 1 < DEPTH:
            h, xn = resid_norm(h, ff, norm_w[layer, 3], norm_w[layer + 1, 0])
        else:
            out = resid_final(h, ff, norm_w[layer, 3])
    return out[None]
```

```python
import functools

import jax
import jax.numpy as jnp
from jax import lax
from jax.experimental import pallas as pl
from jax.experimental.pallas import tpu as pltpu

F32 = jnp.float32
BF16 = jnp.bfloat16

D_MODEL = 4096
SEQ = 8192
DEPTH = 2
N_META = 16
A_HEADS = 16
A_KDIM = 128
A_VDIM = D_MODEL // 2 // A_HEADS
A_FDIM = A_HEADS * A_KDIM
A_WIDTH = A_HEADS * A_VDIM
HGRN_CHUNK = 64
HGRN_SUB = 16
B_HDIM = 64
B_QHEADS = D_MODEL // 2 // B_HDIM
B_KVHEADS = B_QHEADS // 8
B_GROUP = B_QHEADS // B_KVHEADS
B_WIDTH = B_QHEADS * B_HDIM
B_KVWIDTH = B_KVHEADS * B_HDIM
WINDOW = 128
ATTN_BLOCK = 128
EVEN_IN = 2 * A_FDIM + 2 * A_WIDTH + B_WIDTH + 2 * B_KVWIDTH
LRU_WIDTH = D_MODEL
LRU_BLOCKS = 16
LRU_BDIM = LRU_WIDTH // LRU_BLOCKS
LRU_CONV = 4
LRU_C = 8.0
D_FF = 256 * ((8 * D_MODEL // 3 + 255) // 256)
FFN_CONV = 3
NORM_EPS = 1e-6
NEG_INF = -1e30

PAD = ATTN_BLOCK - N_META
ROW0 = PAD + N_META
LP = ROW0 + SEQ
assert PAD % HGRN_CHUNK == HGRN_CHUNK - N_META and ROW0 == ATTN_BLOCK
assert WINDOW == ATTN_BLOCK

V7X_LANES = 128
V7X_SUBLANES = 8
V7X_VMEM_LIMIT_CAP = 60 * 1024 * 1024
CARRY_ROWS = V7X_SUBLANES

TM_MM = 1040
TM_IN_EVEN = 1040
TN_IN_EVEN = 512
TM_OUT = 520
TN_OUT = 1024
TM_DOWN = 520
TN_DOWN = 512
TF_FFN = 256
TN_ODD = 256
EPI_ROWS = 80
K_CHUNK = 256
MAX_WEIGHT_SLABS = 8
TR_NORM = 320
STREAM_GROUPS = 5
FINAL_GROUPS = 4
TR_LRU = 208
CW_LRU = 1024
TR_HGRN = 640
HGRN_HEADS_PER_STEP = 4
SCAN_ROWS = 16


def _vmem_limit(nbytes):
    return int(min(V7X_VMEM_LIMIT_CAP, nbytes * 1.15 + (4 << 20)))


def _params(sem, nbytes):
    return pltpu.CompilerParams(dimension_semantics=sem, vmem_limit_bytes=_vmem_limit(nbytes))


def _rms(x, w):
    x = x.astype(F32)
    return x * lax.rsqrt(jnp.mean(x * x, axis=-1, keepdims=True) + NORM_EPS) * w


def _iota(shape, dim):
    return lax.broadcasted_iota(jnp.int32, shape, dim)


def _dot(a, b):
    return jnp.dot(a, b, preferred_element_type=F32)


def _dot_nt(a, b):
    return lax.dot_general(a, b, (((1,), (1,)), ((), ())), preferred_element_type=F32)


def _dot_tn(a, b):
    return lax.dot_general(a, b, (((0,), (0,)), ((), ())), preferred_element_type=F32)


def _stream_rows(n, x_ref, meta_ref):
    first = jnp.concatenate([jnp.zeros((PAD, D_MODEL), F32), meta_ref[...].astype(F32)], axis=0)
    return jnp.where(n == 0, first, x_ref[...].astype(F32))


def _stream_specs():
    blk = ATTN_BLOCK
    return [pl.BlockSpec((blk, D_MODEL), lambda n: (jnp.maximum(n - 1, 0), 0)),
            pl.BlockSpec((N_META, D_MODEL), lambda n: (0, 0))]


def _rms_cast_stream_kernel(*refs, groups):
    x_refs, meta_ref, w_ref, o_ref = refs[:groups], refs[groups], refs[groups + 1], refs[groups + 2]
    blk = ATTN_BLOCK
    for q, x_ref in enumerate(x_refs):
        rows = _stream_rows(pl.program_id(0), x_ref, meta_ref) if q == 0 else x_ref[...].astype(F32)
        o_ref[q * blk:(q + 1) * blk, :] = _rms(rows, w_ref[...]).astype(o_ref.dtype)


def rms_cast_stream(x2d, meta, w):
    blk, d, groups = ATTN_BLOCK, D_MODEL, STREAM_GROUPS
    rows = blk * groups
    x_specs = [pl.BlockSpec((blk, d), lambda n, q=q: (jnp.maximum(groups * n + q - 1, 0), 0))
               for q in range(groups)]
    return pl.pallas_call(
        functools.partial(_rms_cast_stream_kernel, groups=groups),
        out_shape=jax.ShapeDtypeStruct((LP, d), BF16),
        grid=(LP // rows,),
        in_specs=x_specs + [pl.BlockSpec((N_META, d), lambda n: (0, 0)), pl.BlockSpec((1, d), lambda n: (0, 0))],
        out_specs=pl.BlockSpec((rows, d), lambda n: (n, 0)),
        compiler_params=_params(("arbitrary",), 2 * rows * d * 6),
        name="rms_cast_stream",
    )(*([x2d] * groups), meta, w.reshape(1, d))


def _resid_norm_stream_kernel(x_ref, meta_ref, y_ref, wp_ref, wn_ref, ho_ref, xo_ref):
    h = _stream_rows(pl.program_id(0), x_ref, meta_ref) + _rms(y_ref[...], wp_ref[...])
    ho_ref[...] = h
    xo_ref[...] = _rms(h, wn_ref[...]).astype(xo_ref.dtype)


def resid_norm_stream(x2d, meta, y, w_post, w_next):
    blk, d = ATTN_BLOCK, D_MODEL
    row = pl.BlockSpec((blk, d), lambda n: (n, 0))
    vec = pl.BlockSpec((1, d), lambda n: (0, 0))
    return pl.pallas_call(
        _resid_norm_stream_kernel,
        out_shape=(jax.ShapeDtypeStruct((LP, d), F32), jax.ShapeDtypeStruct((LP, d), BF16)),
        grid=(LP // blk,),
        in_specs=_stream_specs() + [row, vec, vec],
        out_specs=(row, row),
        compiler_params=_params(("arbitrary",), 2 * blk * d * 14 + 4 * blk * d * 4),
        name="resid_norm_stream",
    )(x2d, meta, y, w_post.reshape(1, d), w_next.reshape(1, d))


def _resid_norm_kernel(h_ref, y_ref, wp_ref, wn_ref, ho_ref, xo_ref):
    h = h_ref[...] + _rms(y_ref[...], wp_ref[...])
    ho_ref[...] = h
    xo_ref[...] = _rms(h, wn_ref[...]).astype(xo_ref.dtype)


def resid_norm(h, y, w_post, w_next):
    n, d = h.shape
    tr = TR_NORM
    row = pl.BlockSpec((tr, d), lambda i: (i, 0))
    vec = pl.BlockSpec((1, d), lambda i: (0, 0))
    return pl.pallas_call(
        _resid_norm_kernel,
        out_shape=(jax.ShapeDtypeStruct((n, d), F32), jax.ShapeDtypeStruct((n, d), BF16)),
        grid=(n // tr,),
        in_specs=[row, row, vec, vec],
        out_specs=(row, row),
        compiler_params=_params(("arbitrary",), 2 * tr * d * 14),
        name="resid_norm",
    )(h, y, w_post.reshape(1, d), w_next.reshape(1, d))


def _resid_final_kernel(*refs, groups):
    h_refs, y_refs, wp_ref, o_ref = refs[:groups], refs[groups:2 * groups], refs[2 * groups], refs[2 * groups + 1]
    blk = ATTN_BLOCK
    for q in range(groups):
        o_ref[q * blk:(q + 1) * blk, :] = h_refs[q][...] + _rms(y_refs[q][...], wp_ref[...])


def resid_final(h, y, w_post):
    n, d = h.shape
    blk, groups = ATTN_BLOCK, FINAL_GROUPS
    rows = blk * groups
    off = ROW0 // blk
    src = [pl.BlockSpec((blk, d), lambda i, q=q: (groups * i + q + off, 0)) for q in range(groups)]
    return pl.pallas_call(
        functools.partial(_resid_final_kernel, groups=groups),
        out_shape=jax.ShapeDtypeStruct((n - ROW0, d), F32),
        grid=((n - ROW0) // rows,),
        in_specs=src + src + [pl.BlockSpec((1, d), lambda i: (0, 0))],
        out_specs=pl.BlockSpec((rows, d), lambda i: (i, 0)),
        compiler_params=_params(("arbitrary",), 2 * rows * d * 10),
        name="resid_final",
    )(*([h] * groups), *([y] * groups), w_post.reshape(1, d))


def _slab_specs(w, layer, k, tn, n_slabs, col_of):
    rows = k // n_slabs
    assert rows * n_slabs == k

    def spec(s):
        if w.ndim == 3:
            return pl.BlockSpec((None, rows, tn), lambda *g: (layer, s, col_of(*g, s)))
        return pl.BlockSpec((rows, tn), lambda *g: (s, col_of(*g, s)))

    return [spec(s) for s in range(n_slabs)]


def _cast_slabs(slab_refs, dst_ref):
    rows = slab_refs[0].shape[0]
    for s, ref in enumerate(slab_refs):
        dst_ref[s * rows:(s + 1) * rows, :] = ref[...].astype(BF16)


def _mm_kernel(*refs, n_lhs, n_slabs, cast):
    x_refs, w_refs, o_ref = refs[:n_lhs], refs[n_lhs:n_lhs + n_slabs], refs[n_lhs + n_slabs]
    if cast:
        w_ref = refs[n_lhs + n_slabs + 1]

        @pl.when(pl.program_id(1) == 0)
        def _():
            _cast_slabs(w_refs, w_ref)
    else:
        (w_ref,) = w_refs
    acc, k0 = None, 0
    for x_ref in x_refs:
        kk = x_ref.shape[1]
        part = _dot(x_ref[...], w_ref[k0:k0 + kk, :])
        acc = part if acc is None else acc + part
        k0 += kk
    o_ref[...] = acc.astype(o_ref.dtype)


def matmul(xs, w, *, layer=0, tm, tn, out_dtype=F32, name="matmul"):
    m = xs[0].shape[0]
    k, n = w.shape[-2:]
    assert sum(x.shape[1] for x in xs) == k
    cast = w.dtype != BF16
    ni, nj = m // tm, n // tn
    wbytes = k * tn * (4 if cast else 2)
    nbytes = 2 * tm * k * 2 + 2 * wbytes + (k * tn * 2 if cast else 0) + 2 * tm * tn * 4
    if cast and ni <= MAX_WEIGHT_SLABS:
        n_slabs = ni
        w_specs = _slab_specs(w, layer, k, tn, n_slabs,
                              lambda j, i, s: jnp.minimum(j + (i + ni - 1 - s) // ni, nj - 1))
    else:
        n_slabs = 1
        w_specs = _slab_specs(w, layer, k, tn, 1, lambda j, i, s: j)
    return pl.pallas_call(
        functools.partial(_mm_kernel, n_lhs=len(xs), n_slabs=n_slabs, cast=cast),
        out_shape=jax.ShapeDtypeStruct((m, n), out_dtype),
        grid=(nj, ni),
        in_specs=[pl.BlockSpec((tm, x.shape[1]), lambda j, i: (i, 0)) for x in xs] + w_specs,
        out_specs=pl.BlockSpec((tm, tn), lambda j, i: (i, j)),
        scratch_shapes=[pltpu.VMEM((k, tn), BF16)] if cast else [],
        compiler_params=_params(("arbitrary", "arbitrary"), nbytes),
        name=name,
    )(*xs, *([w] * n_slabs))


def _tile_maps(ni, nj):
    nt = ni * nj

    def cur_row(t):
        return lax.rem(jnp.minimum(t, nt - 1), ni)

    def prev_row(t):
        return lax.rem(jnp.maximum(t - 1, 0), ni)

    def prev_col(t):
        return lax.div(jnp.maximum(t - 1, 0), ni)

    return nt, cur_row, prev_row, prev_col


def _interleaved_projections(x_ref, w_refs, dst_stores, epilogue_chunk, tm):
    nk = x_ref.shape[1] // K_CHUNK
    chunk_rows = list(range(0, tm, EPI_ROWS))
    slots, slot, done = len(w_refs) * nk, 0, 0
    for w_ref, store in zip(w_refs, dst_stores):
        acc = None
        for kk in range(nk):
            ks = slice(kk * K_CHUNK, (kk + 1) * K_CHUNK)
            part = _dot(x_ref[:, ks], w_ref[ks, :])
            acc = part if acc is None else acc + part
            slot += 1
            while done < len(chunk_rows) * slot // slots:
                epilogue_chunk(chunk_rows[done])
                done += 1
        store(acc)


def _causal_conv(buf_ref, cw_ref, cb_ref, width, r0, rows):
    win = buf_ref[r0:r0 + CARRY_ROWS + rows, :]
    conv = cb_ref[...]
    for tap in range(width):
        back = width - 1 - tap
        shifted = pltpu.roll(win, back, 0) if back else win
        conv = conv + shifted[CARRY_ROWS:, :] * cw_ref[tap:tap + 1, :]
    return conv


def _ffn_up_kernel(*refs, tm, ni, nt):
    x_ref, wg_refs, wu_refs = refs[0], refs[1:1 + ni], refs[1 + ni:1 + 2 * ni]
    (cw_ref, cb_ref, wd_ref, o_ref, wdb_ref,
     wgb_ref, wub_ref, g0_ref, g1_ref, u0_ref, u1_ref) = refs[1 + 2 * ni:]
    t = pl.program_id(0)
    i = lax.rem(t, ni)

    @pl.when(t == 0)
    def _():
        g1_ref[...] = jnp.zeros_like(g1_ref)
        u1_ref[...] = jnp.zeros_like(u1_ref)

    @pl.when(jnp.logical_and(i == 0, t < nt))
    def _():
        _cast_slabs(wg_refs, wgb_ref)
        _cast_slabs(wu_refs, wub_ref)

    wdb_ref[...] = wd_ref[...].astype(BF16)

    def step(g_cur, u_cur, g_prev, u_prev):
        g_cur[0:CARRY_ROWS, :] = jnp.where(i == 0, 0.0, g_prev[tm:tm + CARRY_ROWS, :])

        def epilogue_chunk(r0):
            conv = _causal_conv(g_prev, cw_ref, cb_ref, FFN_CONV, r0, EPI_ROWS)
            o_ref[r0:r0 + EPI_ROWS, :] = (jax.nn.gelu(conv, approximate=True)
                                          * u_prev[r0:r0 + EPI_ROWS, :]).astype(o_ref.dtype)

        def store_gate(acc):
            g_cur[CARRY_ROWS:CARRY_ROWS + tm, :] = acc

        def store_up(acc):
            u_cur[...] = acc

        _interleaved_projections(x_ref, (wgb_ref, wub_ref), (store_gate, store_up), epilogue_chunk, tm)

    @pl.when(lax.rem(t, 2) == 0)
    def _():
        step(g0_ref, u0_ref, g1_ref, u1_ref)

    @pl.when(lax.rem(t, 2) == 1)
    def _():
        step(g1_ref, u1_ref, g0_ref, u0_ref)


def ffn_up(xn, w_gu, conv_w, conv_b, w_down, layer):
    m, k = xn.shape
    tm, tf = TM_MM, TF_FFN
    ni, nj = m // tm, D_FF // tf
    nt, cur_row, prev_row, prev_col = _tile_maps(ni, nj)

    def slab_col(t, s):
        return jnp.minimum(lax.div(jnp.minimum(t, nt - 1) + ni - 1 - s, ni), nj - 1)

    wd_rows = D_FF // nt
    assert wd_rows * nt == D_FF and wd_rows % (2 * V7X_SUBLANES) == 0
    nbytes = (2 * tm * k * 2 + 4 * k * tf * 4 + 2 * k * tf * 2 + 2 * tm * tf * 2
              + 4 * (tm + CARRY_ROWS) * tf * 4 + 2 * wd_rows * D_MODEL * 6)
    return pl.pallas_call(
        functools.partial(_ffn_up_kernel, tm=tm, ni=ni, nt=nt),
        out_shape=(jax.ShapeDtypeStruct((m, D_FF), BF16), jax.ShapeDtypeStruct((D_FF, D_MODEL), BF16)),
        grid=(nt + 1,),
        in_specs=[pl.BlockSpec((tm, k), lambda t: (cur_row(t), 0))]
        + _slab_specs(w_gu, layer, k, tf, ni, slab_col)
        + _slab_specs(w_gu, layer, k, tf, ni, lambda t, s: slab_col(t, s) + nj)
        + [pl.BlockSpec((None, FFN_CONV, tf), lambda t: (layer, 0, prev_col(t))),
                  pl.BlockSpec((None, 1, tf), lambda t: (layer, 0, prev_col(t))),
                  pl.BlockSpec((None, wd_rows, D_MODEL), lambda t: (layer, jnp.minimum(t, nt - 1), 0))],
        out_specs=(pl.BlockSpec((tm, tf), lambda t: (prev_row(t), prev_col(t))),
                   pl.BlockSpec((wd_rows, D_MODEL), lambda t: (jnp.minimum(t, nt - 1), 0))),
        scratch_shapes=[pltpu.VMEM((k, tf), BF16), pltpu.VMEM((k, tf), BF16),
                        pltpu.VMEM((tm + CARRY_ROWS, tf), F32), pltpu.VMEM((tm + CARRY_ROWS, tf), F32),
                        pltpu.VMEM((tm, tf), F32), pltpu.VMEM((tm, tf), F32)],
        compiler_params=_params(("arbitrary",), nbytes),
        name="ffn_up",
    )(xn, *([w_gu] * (2 * ni)), conv_w, conv_b.reshape(conv_b.shape[0], 1, D_FF), w_down)


def _in_odd_kernel(*refs, tm, ni, nt):
    x_ref, wy_refs, wx_refs = refs[0], refs[1:1 + ni], refs[1 + ni:1 + 2 * ni]
    (cw_ref, cb_ref, y_ref, xc_ref,
     wyb_ref, wxb_ref, b0_ref, b1_ref, r0_ref, r1_ref) = refs[1 + 2 * ni:]
    t = pl.program_id(0)
    i = lax.rem(t, ni)

    @pl.when(t == 0)
    def _():
        b1_ref[...] = jnp.zeros_like(b1_ref)
        r1_ref[...] = jnp.zeros_like(r1_ref)

    @pl.when(jnp.logical_and(i == 0, t < nt))
    def _():
        _cast_slabs(wy_refs, wyb_ref)
        _cast_slabs(wx_refs, wxb_ref)

    def step(b_cur, r_cur, b_prev, r_prev):
        b_cur[0:CARRY_ROWS, :] = jnp.where(i == 0, 0.0, b_prev[tm:tm + CARRY_ROWS, :])

        def epilogue_chunk(r0):
            y_ref[r0:r0 + EPI_ROWS, :] = jax.nn.gelu(r_prev[r0:r0 + EPI_ROWS, :],
                                                     approximate=True).astype(y_ref.dtype)
            xc_ref[r0:r0 + EPI_ROWS, :] = _causal_conv(b_prev, cw_ref, cb_ref, LRU_CONV, r0, EPI_ROWS)

        def store_y(acc):
            r_cur[...] = acc

        def store_x(acc):
            b_cur[CARRY_ROWS:CARRY_ROWS + tm, :] = acc

        _interleaved_projections(x_ref, (wyb_ref, wxb_ref), (store_y, store_x), epilogue_chunk, tm)

    @pl.when(lax.rem(t, 2) == 0)
    def _():
        step(b0_ref, r0_ref, b1_ref, r1_ref)

    @pl.when(lax.rem(t, 2) == 1)
    def _():
        step(b1_ref, r1_ref, b0_ref, r0_ref)


def in_odd(xn, w_in, conv_w, conv_b, layer):
    m, k = xn.shape
    tm, tn = TM_MM, TN_ODD
    ni, nj = m // tm, LRU_WIDTH // tn
    nt, cur_row, prev_row, prev_col = _tile_maps(ni, nj)
    nbytes = (2 * tm * k * 2 + 4 * k * tn * 4 + 2 * k * tn * 2 + 4 * tm * tn * 4
              + 4 * (tm + CARRY_ROWS) * tn * 4)

    def slab_col(t, s):
        return jnp.minimum(lax.div(jnp.minimum(t, nt - 1) + ni - 1 - s, ni), nj - 1)

    out = pl.BlockSpec((tm, tn), lambda t: (prev_row(t), prev_col(t)))
    return pl.pallas_call(
        functools.partial(_in_odd_kernel, tm=tm, ni=ni, nt=nt),
        out_shape=(jax.ShapeDtypeStruct((m, LRU_WIDTH), BF16), jax.ShapeDtypeStruct((m, LRU_WIDTH), F32)),
        grid=(nt + 1,),
        in_specs=[pl.BlockSpec((tm, k), lambda t: (cur_row(t), 0))]
        + _slab_specs(w_in, layer, k, tn, ni, slab_col)
        + _slab_specs(w_in, layer, k, tn, ni, lambda t, s: slab_col(t, s) + nj)
        + [pl.BlockSpec((None, LRU_CONV, tn), lambda t: (layer, 0, prev_col(t))),
                  pl.BlockSpec((None, 1, tn), lambda t: (layer, 0, prev_col(t)))],
        out_specs=(out, out),
        scratch_shapes=[pltpu.VMEM((k, tn), BF16), pltpu.VMEM((k, tn), BF16),
                        pltpu.VMEM((tm + CARRY_ROWS, tn), F32), pltpu.VMEM((tm + CARRY_ROWS, tn), F32),
                        pltpu.VMEM((tm, tn), F32), pltpu.VMEM((tm, tn), F32)],
        compiler_params=_params(("arbitrary",), nbytes),
        name="in_odd",
    )(xn, *([w_in] * (2 * ni)), conv_w, conv_b.reshape(conv_b.shape[0], 1, LRU_WIDTH))


def _lru_kernel(xc_ref, y_ref, wa_ref, wx_ref, ba_ref, bx_ref, ap_ref, o_ref, a_ref, b_ref, hc_ref, *, tr):
    step = pl.program_id(0)

    @pl.when(step == 0)
    def _():
        hc_ref[...] = jnp.zeros_like(hc_ref)

    valid = (step * tr + _iota((tr, 1), 0)) >= PAD
    for blk in range(LRU_BLOCKS):
        sl = slice(blk * LRU_BDIM, (blk + 1) * LRU_BDIM)
        x = xc_ref[:, sl]
        xb = x.astype(BF16)
        t_r = jnp.tanh(_dot(xb, wa_ref[blk]) + ba_ref[:, sl])
        t_i = jnp.tanh(_dot(xb, wx_ref[blk]) + bx_ref[:, sl])
        c = (LRU_C / 4.0) * jax.nn.log_sigmoid(ap_ref[:, sl])
        th = jnp.tanh(t_r * c + c)
        em = (th + th) / (1.0 - th)
        a_ref[:, sl] = 1.0 + em
        xh = 0.5 * x
        inp = jnp.sqrt(em * (-2.0 - em)) * (t_i * xh + xh)
        b_ref[:, sl] = jnp.where(valid, inp, 0.0)

    half = SCAN_ROWS // 2
    ridx = _iota((half, CW_LRU), 0)

    def local_scan(a, b):
        shift = 1
        while shift < half:
            ok = ridx >= shift
            b = jnp.where(ok, a * pltpu.roll(b, shift, 0) + b, b)
            a = jnp.where(ok, a * pltpu.roll(a, shift, 0), a)
            shift *= 2
        return a, b

    for c in range(LRU_WIDTH // CW_LRU):
        cs = slice(c * CW_LRU, (c + 1) * CW_LRU)

        def body(g, carry, cs=cs):
            r0 = pl.multiple_of(g * SCAN_ROWS, SCAN_ROWS)
            a = a_ref[pl.ds(r0, SCAN_ROWS), cs]
            b = b_ref[pl.ds(r0, SCAN_ROWS), cs]
            a_top, b_top = local_scan(a[:half], b[:half])
            a_bot, b_bot = local_scan(a[half:], b[half:])
            h_top = a_top * carry + b_top
            h_bot = a_bot * h_top[half - 1:half, :] + b_bot
            h = jnp.concatenate([h_top, h_bot], axis=0)
            o_ref[pl.ds(r0, SCAN_ROWS), cs] = (
                h * y_ref[pl.ds(r0, SCAN_ROWS), cs].astype(F32)).astype(o_ref.dtype)
            return h_bot[half - 1:half, :]

        hc_ref[0:1, cs] = lax.fori_loop(0, tr // SCAN_ROWS, body, hc_ref[0:1, cs])


def rglru(xc, y, w_a, b_a, w_x, b_x, a_param):
    m, d = xc.shape
    tr = TR_LRU
    row = pl.BlockSpec((tr, d), lambda i: (i, 0))
    vec = pl.BlockSpec((1, d), lambda i: (0, 0))
    wspec = pl.BlockSpec((LRU_BLOCKS, LRU_BDIM, LRU_BDIM), lambda i: (0, 0, 0))
    nbytes = 4 * tr * d * 4 + 2 * tr * d * 2 + 2 * tr * d * 4 + 4 * LRU_BLOCKS * LRU_BDIM * LRU_BDIM * 2
    return pl.pallas_call(
        functools.partial(_lru_kernel, tr=tr),
        out_shape=jax.ShapeDtypeStruct((m, d), BF16),
        grid=(m // tr,),
        in_specs=[row, row, wspec, wspec, vec, vec, vec],
        out_specs=row,
        scratch_shapes=[pltpu.VMEM((tr, d), F32), pltpu.VMEM((tr, d), F32),
                        pltpu.VMEM((V7X_SUBLANES, d), F32)],
        compiler_params=_params(("arbitrary",), nbytes),
        name="rglru",
    )(xc, y, (0.5 * w_a).astype(BF16), (0.5 * w_x).astype(BF16), 0.5 * b_a.reshape(1, d),
      0.5 * b_x.reshape(1, d), a_param.reshape(1, d))


def _hgrn_kernel(q_ref, f_ref, i_ref, g_ref, lbl_ref, gnw_ref, o_ref, st_ref, *, tr, layer_j):
    step = pl.program_id(1)

    @pl.when(step == 0)
    def _():
        st_ref[...] = jnp.zeros_like(st_ref)

    logits = lbl_ref[...]
    e = jnp.exp(logits - jnp.max(logits, axis=0, keepdims=True))
    lb = jnp.sum(e[0:layer_j + 1], axis=0, keepdims=True) / jnp.sum(e, axis=0, keepdims=True)

    width = HGRN_HEADS_PER_STEP * A_KDIM
    valid = (step * tr + _iota((tr, 1), 0)) >= PAD
    q = jax.nn.silu(q_ref[...].astype(F32))
    forget = lb + (1.0 - lb) * jax.nn.sigmoid(f_ref[...].astype(F32))
    k = jnp.where(valid, 1.0 - forget, 0.0)
    g = jnp.where(valid, jnp.log(forget), 0.0)
    v = i_ref[...]

    pos = _iota((tr, width), 0) & (HGRN_CHUNK - 1)
    b = g
    shift = 1
    while shift < HGRN_CHUNK:
        b = b + jnp.where(pos >= shift, pltpu.roll(b, shift, 0), 0.0)
        shift *= 2

    n_sub = HGRN_CHUNK // HGRN_SUB
    cpos = _iota((HGRN_CHUNK, A_KDIM), 0)
    causal = _iota((HGRN_CHUNK, HGRN_CHUNK), 0) >= _iota((HGRN_CHUNK, HGRN_CHUNK), 1)
    gnw = gnw_ref[...]
    heads = range(HGRN_HEADS_PER_STEP)
    chunks = range(tr // HGRN_CHUNK)
    pairs = [(hh, c) for hh in heads for c in chunks]

    def part(x, p):
        hh, c = p
        return x[c * HGRN_CHUNK:(c + 1) * HGRN_CHUNK, hh * A_KDIM:(hh + 1) * A_KDIM]

    bc = {p: part(b, p) for p in pairs}
    qc = {p: part(q, p) for p in pairs}
    kc = {p: part(k, p) for p in pairs}
    vcb = {p: part(v, p).astype(BF16) for p in pairs}
    b_last = {p: bc[p][HGRN_CHUNK - 1:HGRN_CHUNK] for p in pairs}
    u_t = {p: _dot_tn(vcb[p], (kc[p] * jnp.exp(b_last[p] - bc[p])).astype(BF16)) for p in pairs}
    att = {}
    for p in pairs:
        rows = []
        for i in range(n_sub):
            ss = slice(i * HGRN_SUB, (i + 1) * HGRN_SUB)
            ref = jnp.zeros((1, A_KDIM), F32) if i == 0 else bc[p][i * HGRN_SUB - 1:i * HGRN_SUB]
            q_sc = (qc[p][ss] * jnp.exp(bc[p][ss] - ref)).astype(BF16)
            expo = jnp.where(cpos < (i + 1) * HGRN_SUB, ref - bc[p], 0.0)
            k_sc = (kc[p] * jnp.exp(expo)).astype(BF16)
            rows.append(_dot_nt(q_sc, k_sc))
        att[p] = jnp.where(causal, jnp.concatenate(rows, axis=0), 0.0).astype(BF16)
    o_intra = {p: _dot(att[p], vcb[p]) for p in pairs}
    states = {}
    for hh in heads:
        st = st_ref[hh]
        for c in chunks:
            states[(hh, c)] = st.astype(BF16)
            st = st * jnp.exp(b_last[(hh, c)]) + u_t[(hh, c)]
        st_ref[hh] = st
    o_inter = {p: _dot_nt((qc[p] * jnp.exp(bc[p])).astype(BF16), states[p]) for p in pairs}
    for hh, c in pairs:
        rs = slice(c * HGRN_CHUNK, (c + 1) * HGRN_CHUNK)
        cs = slice(hh * A_VDIM, (hh + 1) * A_VDIM)
        gate = jax.nn.silu(g_ref[rs, cs].astype(F32))
        o_ref[rs, cs] = (_rms(o_inter[(hh, c)] + o_intra[(hh, c)], gnw) * gate).astype(o_ref.dtype)


def hgrn2(hproj, lb_logits, gn_w, layer_j):
    m = hproj.shape[0]
    tr = TR_HGRN
    nrow = lb_logits.shape[0]
    hps = HGRN_HEADS_PER_STEP
    groups = A_HEADS // hps

    def col(off):
        return pl.BlockSpec((tr, hps * A_KDIM), lambda h, t, off=off: (t, h + off * groups))

    nbytes = 2 * 4 * tr * hps * A_KDIM * 2 + 2 * tr * hps * A_VDIM * 2 + 16 * tr * hps * A_KDIM * 4
    return pl.pallas_call(
        functools.partial(_hgrn_kernel, tr=tr, layer_j=layer_j),
        out_shape=jax.ShapeDtypeStruct((m, A_WIDTH), BF16),
        grid=(groups, m // tr),
        in_specs=[col(0), col(1), col(2), col(3),
                  pl.BlockSpec((nrow, hps * A_KDIM), lambda h, t: (0, h)),
                  pl.BlockSpec((1, A_VDIM), lambda h, t: (0, 0))],
        out_specs=pl.BlockSpec((tr, hps * A_VDIM), lambda h, t: (t, h)),
        scratch_shapes=[pltpu.VMEM((hps, A_VDIM, A_KDIM), F32)],
        compiler_params=_params(("arbitrary", "arbitrary"), nbytes),
        name="hgrn2",
    )(hproj, hproj, hproj, hproj, lb_logits, gn_w.reshape(1, A_VDIM))


PAIRS = B_GROUP // 2
QROWS = PAIRS * ATTN_BLOCK


def _swa_kernel(sink_ref, q_ref, kc_ref, kp_ref, km_ref, vc_ref, vp_ref, vm_ref, o_ref):
    n = pl.program_id(0)
    scale = B_HDIM ** -0.5
    lane_lo = _iota((1, V7X_LANES), 1) < B_HDIM
    tq = _iota((QROWS, ATTN_BLOCK), 0) & (ATTN_BLOCK - 1)
    sk = _iota((QROWS, ATTN_BLOCK), 1)
    masks = (sk > tq + jnp.where(n >= 2, 0, 2 * ATTN_BLOCK),
             sk <= tq + jnp.where(n >= 1, 0, -2 * ATTN_BLOCK),
             jnp.logical_and(sk >= PAD, sk <= tq + jnp.where(n >= 1, ATTN_BLOCK, 0)))

    def split(x, natural_lo, fill):
        rolled = pltpu.roll(x, B_HDIM, 1)
        lo_src, hi_src = (x, rolled) if natural_lo else (rolled, x)
        return (jnp.where(lane_lo, lo_src, fill).astype(BF16), jnp.where(lane_lo, fill, hi_src).astype(BF16))

    chains = [(h, par) for h in range(B_KVHEADS) for par in range(2)]
    keys, vals, qs = {}, {}, {}
    for h in range(B_KVHEADS):
        tile = slice((h // 2) * V7X_LANES, (h // 2 + 1) * V7X_LANES)
        nat = h % 2 == 0
        keys[h] = [split(r[:, tile].astype(F32), nat, 0.0) for r in (kp_ref, kc_ref, km_ref)]
        vals[h] = [split(r[:, tile].astype(F32), nat, 1.0) for r in (vp_ref, vc_ref, vm_ref)]
        qs[h] = (jnp.concatenate(
            [q_ref[:, (h * PAIRS + p) * V7X_LANES:(h * PAIRS + p + 1) * V7X_LANES] for p in range(PAIRS)],
            axis=0).astype(F32) * scale).astype(BF16)
    logits = {c: [jnp.where(m, _dot_nt(qs[c[0]], kk[c[1]]), NEG_INF) for m, kk in zip(masks, keys[c[0]])]
              for c in chains}
    sinks = {(h, par): jnp.concatenate(
        [jnp.full((ATTN_BLOCK, 1), sink_ref[h * B_GROUP + 2 * p + par], F32) for p in range(PAIRS)], axis=0)
        for h, par in chains}
    mx = {c: jnp.maximum(jnp.max(jnp.maximum(jnp.maximum(logits[c][0], logits[c][1]), logits[c][2]),
                                 axis=-1, keepdims=True), sinks[c]) for c in chains}
    pv = {}
    for c in chains:
        for lg, vv in zip(logits[c], vals[c[0]]):
            part = _dot(jnp.exp(lg - mx[c]).astype(BF16), vv[c[1]])
            pv[c] = part if c not in pv else pv[c] + part
    out = {}
    for c in chains:
        den = pltpu.roll(pv[c], B_HDIM, 1) + jnp.exp(sinks[c] - mx[c])
        own_half = lane_lo if c[1] == 0 else jnp.logical_not(lane_lo)
        part = jnp.where(own_half, pv[c] / den, 0.0)
        out[c[0]] = part if c[0] not in out else out[c[0]] + part
    for h in range(B_KVHEADS):
        for p in range(PAIRS):
            o_ref[:, (h * PAIRS + p) * V7X_LANES:(h * PAIRS + p + 1) * V7X_LANES] = (
                out[h][p * ATTN_BLOCK:(p + 1) * ATTN_BLOCK].astype(o_ref.dtype))


def swa(hproj, sinks):
    m = hproj.shape[0]
    blk = ATTN_BLOCK
    q_col = (2 * A_FDIM + 2 * A_WIDTH) // B_WIDTH
    k_col = (2 * A_FDIM + 2 * A_WIDTH + B_WIDTH) // B_KVWIDTH
    v_col = k_col + 1
    qspec = pl.BlockSpec((blk, B_WIDTH), lambda n: (n, q_col))

    def kv(col, which):
        if which == "cur":
            return pl.BlockSpec((blk, B_KVWIDTH), lambda n: (n, col))
        if which == "prev":
            return pl.BlockSpec((blk, B_KVWIDTH), lambda n: (jnp.maximum(n - 1, 0), col))
        return pl.BlockSpec((blk, B_KVWIDTH), lambda n: (0, col))

    nbytes = 2 * blk * B_WIDTH * 4 + 12 * blk * B_KVWIDTH * 4 + 2 * blk * B_WIDTH * 2 + 40 * QROWS * 128 * 4
    return pl.pallas_call(
        _swa_kernel,
        out_shape=jax.ShapeDtypeStruct((m, B_WIDTH), BF16),
        grid=(m // blk,),
        in_specs=[pl.BlockSpec(memory_space=pltpu.SMEM), qspec,
                  kv(k_col, "cur"), kv(k_col, "prev"), kv(k_col, "meta"),
                  kv(v_col, "cur"), kv(v_col, "prev"), kv(v_col, "meta")],
        out_specs=pl.BlockSpec((blk, B_WIDTH), lambda n: (n, 0)),
        compiler_params=_params(("arbitrary",), nbytes),
        name="swa",
    )(sinks, hproj, hproj, hproj, hproj, hproj, hproj, hproj)


def kernel(x, meta_tokens, norm_w, w_in_even, lb_logits, hgrn_gn_w, attn_sinks, w_out_even,
           w_in_odd, lru_conv_w, lru_conv_b, lru_wa, lru_ba, lru_wx, lru_bx, lru_a_param, w_out_odd,
           ffn_w_gu, ffn_conv_w, ffn_conv_b, ffn_w_down):
    assert x.shape == (1, SEQ, D_MODEL) and norm_w.shape[0] == DEPTH and w_in_even.shape[-1] == EVEN_IN
    x2d = x[0]
    xn = rms_cast_stream(x2d, meta_tokens, norm_w[0, 0])
    h, out = None, None
    for layer in range(DEPTH):
        j = layer // 2
        if layer % 2 == 0:
            hproj = matmul([xn], w_in_even, layer=j, tm=TM_IN_EVEN, tn=TN_IN_EVEN, out_dtype=BF16,
                           name="in_even")
            o_a = hgrn2(hproj, lb_logits, hgrn_gn_w[j], j)
            o_b = swa(hproj, attn_sinks[j])
            mix = matmul([o_a, o_b], w_out_even, layer=j, tm=TM_OUT, tn=TN_OUT, out_dtype=BF16,
                         name="out_even")
        else:
            y_br, x_br = in_odd(xn, w_in_odd, lru_conv_w, lru_conv_b, j)
            rec = rglru(x_br, y_br, lru_wa[j], lru_ba[j], lru_wx[j], lru_bx[j], lru_a_param[j])
            mix = matmul([rec], w_out_odd, layer=j, tm=TM_OUT, tn=TN_OUT, out_dtype=BF16, name="out_odd")
        if h is None:
            h, xn = resid_norm_stream(x2d, meta_tokens, mix, norm_w[layer, 1], norm_w[layer, 2])
        else:
            h, xn = resid_norm(h, mix, norm_w[layer, 1], norm_w[layer, 2])
        act, w_down_bf16 = ffn_up(xn, ffn_w_gu, ffn_conv_w, ffn_conv_b, ffn_w_down, layer)
        ff = matmul([act], w_down_bf16, tm=TM_DOWN, tn=TN_DOWN, out_dtype=BF16, name="ffn_down")
        if layer + 1 < DEPTH:
            h, xn = resid_norm(h, ff, norm_w[layer, 3], norm_w[layer + 1, 0])
        else:
            out = resid_final(h, ff, norm_w[layer, 3])
    return out[None]
```

```python
import functools

import jax
import jax.numpy as jnp
from jax import lax
from jax.experimental import pallas as pl
from jax.experimental.pallas import tpu as pltpu

F32 = jnp.float32
BF16 = jnp.bfloat16

D_MODEL = 4096
SEQ = 8192
DEPTH = 2
N_META = 16
A_HEADS = 16
A_KDIM = 128
A_VDIM = D_MODEL // 2 // A_HEADS
A_FDIM = A_HEADS * A_KDIM
A_WIDTH = A_HEADS * A_VDIM
HGRN_CHUNK = 64
HGRN_SUB = 16
B_HDIM = 64
B_QHEADS = D_MODEL // 2 // B_HDIM
B_KVHEADS = B_QHEADS // 8
B_GROUP = B_QHEADS // B_KVHEADS
B_WIDTH = B_QHEADS * B_HDIM
B_KVWIDTH = B_KVHEADS * B_HDIM
WINDOW = 128
ATTN_BLOCK = 128
EVEN_IN = 2 * A_FDIM + 2 * A_WIDTH + B_WIDTH + 2 * B_KVWIDTH
LRU_WIDTH = D_MODEL
LRU_BLOCKS = 16
LRU_BDIM = LRU_WIDTH // LRU_BLOCKS
LRU_CONV = 4
LRU_C = 8.0
D_FF = 256 * ((8 * D_MODEL // 3 + 255) // 256)
FFN_CONV = 3
NORM_EPS = 1e-6
NEG_INF = -1e30

PAD = ATTN_BLOCK - N_META
ROW0 = PAD + N_META
LP = ROW0 + SEQ
assert PAD % HGRN_CHUNK == HGRN_CHUNK - N_META and ROW0 == ATTN_BLOCK
assert WINDOW == ATTN_BLOCK

V7X_LANES = 128
V7X_SUBLANES = 8
V7X_VMEM_LIMIT_CAP = 60 * 1024 * 1024
CARRY_ROWS = V7X_SUBLANES

TM_MM = 1040
TM_IN_EVEN = 1040
TN_IN_EVEN = 512
TM_OUT = 520
TN_OUT = 1024
TM_DOWN = 520
TN_DOWN = 512
TF_FFN = 256
TN_ODD = 256
EPI_ROWS = 80
K_CHUNK = 256
MAX_WEIGHT_SLABS = 8
TR_NORM = 320
STREAM_GROUPS = 5
FINAL_GROUPS = 4
TR_LRU = 208
CW_LRU = 1024
TR_HGRN = 640
HGRN_HEADS_PER_STEP = 4
SCAN_ROWS = 16


def _vmem_limit(nbytes):
    return int(min(V7X_VMEM_LIMIT_CAP, nbytes * 1.15 + (4 << 20)))


def _params(sem, nbytes):
    return pltpu.CompilerParams(dimension_semantics=sem, vmem_limit_bytes=_vmem_limit(nbytes))


def _rms(x, w):
    x = x.astype(F32)
    return x * lax.rsqrt(jnp.mean(x * x, axis=-1, keepdims=True) + NORM_EPS) * w


def _iota(shape, dim):
    return lax.broadcasted_iota(jnp.int32, shape, dim)


def _dot(a, b):
    return jnp.dot(a, b, preferred_element_type=F32)


def _dot_nt(a, b):
    return lax.dot_general(a, b, (((1,), (1,)), ((), ())), preferred_element_type=F32)


def _dot_tn(a, b):
    return lax.dot_general(a, b, (((0,), (0,)), ((), ())), preferred_element_type=F32)


def _stream_rows(n, x_ref, meta_ref):
    first = jnp.concatenate([jnp.zeros((PAD, D_MODEL), F32), meta_ref[...].astype(F32)], axis=0)
    return jnp.where(n == 0, first, x_ref[...].astype(F32))


def _stream_specs():
    blk = ATTN_BLOCK
    return [pl.BlockSpec((blk, D_MODEL), lambda n: (jnp.maximum(n - 1, 0), 0)),
            pl.BlockSpec((N_META, D_MODEL), lambda n: (0, 0))]


def _rms_cast_stream_kernel(*refs, groups):
    x_refs, meta_ref, w_ref, o_ref = refs[:groups], refs[groups], refs[groups + 1], refs[groups + 2]
    blk = ATTN_BLOCK
    for q, x_ref in enumerate(x_refs):
        rows = _stream_rows(pl.program_id(0), x_ref, meta_ref) if q == 0 else x_ref[...].astype(F32)
        o_ref[q * blk:(q + 1) * blk, :] = _rms(rows, w_ref[...]).astype(o_ref.dtype)


def rms_cast_stream(x2d, meta, w):
    blk, d, groups = ATTN_BLOCK, D_MODEL, STREAM_GROUPS
    rows = blk * groups
    x_specs = [pl.BlockSpec((blk, d), lambda n, q=q: (jnp.maximum(groups * n + q - 1, 0), 0))
               for q in range(groups)]
    return pl.pallas_call(
        functools.partial(_rms_cast_stream_kernel, groups=groups),
        out_shape=jax.ShapeDtypeStruct((LP, d), BF16),
        grid=(LP // rows,),
        in_specs=x_specs + [pl.BlockSpec((N_META, d), lambda n: (0, 0)), pl.BlockSpec((1, d), lambda n: (0, 0))],
        out_specs=pl.BlockSpec((rows, d), lambda n: (n, 0)),
        compiler_params=_params(("arbitrary",), 2 * rows * d * 6),
        name="rms_cast_stream",
    )(*([x2d] * groups), meta, w.reshape(1, d))


def _resid_norm_stream_kernel(x_ref, meta_ref, y_ref, wp_ref, wn_ref, ho_ref, xo_ref):
    h = _stream_rows(pl.program_id(0), x_ref, meta_ref) + _rms(y_ref[...], wp_ref[...])
    ho_ref[...] = h
    xo_ref[...] = _rms(h, wn_ref[...]).astype(xo_ref.dtype)


def resid_norm_stream(x2d, meta, y, w_post, w_next):
    blk, d = ATTN_BLOCK, D_MODEL
    row = pl.BlockSpec((blk, d), lambda n: (n, 0))
    vec = pl.BlockSpec((1, d), lambda n: (0, 0))
    return pl.pallas_call(
        _resid_norm_stream_kernel,
        out_shape=(jax.ShapeDtypeStruct((LP, d), F32), jax.ShapeDtypeStruct((LP, d), BF16)),
        grid=(LP // blk,),
        in_specs=_stream_specs() + [row, vec, vec],
        out_specs=(row, row),
        compiler_params=_params(("arbitrary",), 2 * blk * d * 14 + 4 * blk * d * 4),
        name="resid_norm_stream",
    )(x2d, meta, y, w_post.reshape(1, d), w_next.reshape(1, d))


def _resid_norm_kernel(h_ref, y_ref, wp_ref, wn_ref, ho_ref, xo_ref):
    h = h_ref[...] + _rms(y_ref[...], wp_ref[...])
    ho_ref[...] = h
    xo_ref[...] = _rms(h, wn_ref[...]).astype(xo_ref.dtype)


def resid_norm(h, y, w_post, w_next):
    n, d = h.shape
    tr = TR_NORM
    row = pl.BlockSpec((tr, d), lambda i: (i, 0))
    vec = pl.BlockSpec((1, d), lambda i: (0, 0))
    return pl.pallas_call(
        _resid_norm_kernel,
        out_shape=(jax.ShapeDtypeStruct((n, d), F32), jax.ShapeDtypeStruct((n, d), BF16)),
        grid=(n // tr,),
        in_specs=[row, row, vec, vec],
        out_specs=(row, row),
        compiler_params=_params(("arbitrary",), 2 * tr * d * 14),
        name="resid_norm",
    )(h, y, w_post.reshape(1, d), w_next.reshape(1, d))


def _resid_final_kernel(*refs, groups):
    h_refs, y_refs, wp_ref, o_ref = refs[:groups], refs[groups:2 * groups], refs[2 * groups], refs[2 * groups + 1]
    blk = ATTN_BLOCK
    for q in range(groups):
        o_ref[q * blk:(q + 1) * blk, :] = h_refs[q][...] + _rms(y_refs[q][...], wp_ref[...])


def resid_final(h, y, w_post):
    n, d = h.shape
    blk, groups = ATTN_BLOCK, FINAL_GROUPS
    rows = blk * groups
    off = ROW0 // blk
    src = [pl.BlockSpec((blk, d), lambda i, q=q: (groups * i + q + off, 0)) for q in range(groups)]
    return pl.pallas_call(
        functools.partial(_resid_final_kernel, groups=groups),
        out_shape=jax.ShapeDtypeStruct((n - ROW0, d), F32),
        grid=((n - ROW0) // rows,),
        in_specs=src + src + [pl.BlockSpec((1, d), lambda i: (0, 0))],
        out_specs=pl.BlockSpec((rows, d), lambda i: (i, 0)),
        compiler_params=_params(("arbitrary",), 2 * rows * d * 10),
        name="resid_final",
    )(*([h] * groups), *([y] * groups), w_post.reshape(1, d))


def _slab_specs(w, layer, k, tn, n_slabs, col_of):
    rows = k // n_slabs
    assert rows * n_slabs == k

    def spec(s):
        if w.ndim == 3:
            return pl.BlockSpec((None, rows, tn), lambda *g: (layer, s, col_of(*g, s)))
        return pl.BlockSpec((rows, tn), lambda *g: (s, col_of(*g, s)))

    return [spec(s) for s in range(n_slabs)]


def _cast_slabs(slab_refs, dst_ref):
    rows = slab_refs[0].shape[0]
    for s, ref in enumerate(slab_refs):
        dst_ref[s * rows:(s + 1) * rows, :] = ref[...].astype(BF16)


def _mm_kernel(*refs, n_lhs, n_slabs, cast):
    x_refs, w_refs, o_ref = refs[:n_lhs], refs[n_lhs:n_lhs + n_slabs], refs[n_lhs + n_slabs]
    if cast:
        w_ref = refs[n_lhs + n_slabs + 1]

        @pl.when(pl.program_id(1) == 0)
        def _():
            _cast_slabs(w_refs, w_ref)
    else:
        (w_ref,) = w_refs
    acc, k0 = None, 0
    for x_ref in x_refs:
        kk = x_ref.shape[1]
        part = _dot(x_ref[...], w_ref[k0:k0 + kk, :])
        acc = part if acc is None else acc + part
        k0 += kk
    o_ref[...] = acc.astype(o_ref.dtype)


def matmul(xs, w, *, layer=0, tm, tn, out_dtype=F32, name="matmul"):
    m = xs[0].shape[0]
    k, n = w.shape[-2:]
    assert sum(x.shape[1] for x in xs) == k
    cast = w.dtype != BF16
    ni, nj = m // tm, n // tn
    wbytes = k * tn * (4 if cast else 2)
    nbytes = 2 * tm * k * 2 + 2 * wbytes + (k * tn * 2 if cast else 0) + 2 * tm * tn * 4
    if cast and ni <= MAX_WEIGHT_SLABS:
        n_slabs = ni
        w_specs = _slab_specs(w, layer, k, tn, n_slabs,
                              lambda j, i, s: jnp.minimum(j + (i + ni - 1 - s) // ni, nj - 1))
    else:
        n_slabs = 1
        w_specs = _slab_specs(w, layer, k, tn, 1, lambda j, i, s: j)
    return pl.pallas_call(
        functools.partial(_mm_kernel, n_lhs=len(xs), n_slabs=n_slabs, cast=cast),
        out_shape=jax.ShapeDtypeStruct((m, n), out_dtype),
        grid=(nj, ni),
        in_specs=[pl.BlockSpec((tm, x.shape[1]), lambda j, i: (i, 0)) for x in xs] + w_specs,
        out_specs=pl.BlockSpec((tm, tn), lambda j, i: (i, j)),
        scratch_shapes=[pltpu.VMEM((k, tn), BF16)] if cast else [],
        compiler_params=_params(("arbitrary", "arbitrary"), nbytes),
        name=name,
    )(*xs, *([w] * n_slabs))


def _tile_maps(ni, nj):
    nt = ni * nj

    def cur_row(t):
        return lax.rem(jnp.minimum(t, nt - 1), ni)

    def prev_row(t):
        return lax.rem(jnp.maximum(t - 1, 0), ni)

    def prev_col(t):
        return lax.div(jnp.maximum(t - 1, 0), ni)

    return nt, cur_row, prev_row, prev_col


def _interleaved_projections(x_ref, w_refs, dst_stores, epilogue_chunk, tm):
    nk = x_ref.shape[1] // K_CHUNK
    chunk_rows = list(range(0, tm, EPI_ROWS))
    slots, slot, done = len(w_refs) * nk, 0, 0
    for w_ref, store in zip(w_refs, dst_stores):
        acc = None
        for kk in range(nk):
            ks = slice(kk * K_CHUNK, (kk + 1) * K_CHUNK)
            part = _dot(x_ref[:, ks], w_ref[ks, :])
            acc = part if acc is None else acc + part
            slot += 1
            while done < len(chunk_rows) * slot // slots:
                epilogue_chunk(chunk_rows[done])
                done += 1
        store(acc)


def _causal_conv(buf_ref, cw_ref, cb_ref, width, r0, rows):
    win = buf_ref[r0:r0 + CARRY_ROWS + rows, :]
    conv = cb_ref[...]
    for tap in range(width):
        back = width - 1 - tap
        shifted = pltpu.roll(win, back, 0) if back else win
        conv = conv + shifted[CARRY_ROWS:, :] * cw_ref[tap:tap + 1, :]
    return conv


def _ffn_up_kernel(*refs, tm, ni, nt):
    x_ref, wg_refs, wu_refs = refs[0], refs[1:1 + ni], refs[1 + ni:1 + 2 * ni]
    (cw_ref, cb_ref, wd_ref, o_ref, wdb_ref,
     wgb_ref, wub_ref, g0_ref, g1_ref, u0_ref, u1_ref) = refs[1 + 2 * ni:]
    t = pl.program_id(0)
    i = lax.rem(t, ni)

    @pl.when(t == 0)
    def _():
        g1_ref[...] = jnp.zeros_like(g1_ref)
        u1_ref[...] = jnp.zeros_like(u1_ref)

    @pl.when(jnp.logical_and(i == 0, t < nt))
    def _():
        _cast_slabs(wg_refs, wgb_ref)
        _cast_slabs(wu_refs, wub_ref)

    wdb_ref[...] = wd_ref[...].astype(BF16)

    def step(g_cur, u_cur, g_prev, u_prev):
        g_cur[0:CARRY_ROWS, :] = jnp.where(i == 0, 0.0, g_prev[tm:tm + CARRY_ROWS, :])

        def epilogue_chunk(r0):
            conv = _causal_conv(g_prev, cw_ref, cb_ref, FFN_CONV, r0, EPI_ROWS)
            o_ref[r0:r0 + EPI_ROWS, :] = (jax.nn.gelu(conv, approximate=True)
                                          * u_prev[r0:r0 + EPI_ROWS, :]).astype(o_ref.dtype)

        def store_gate(acc):
            g_cur[CARRY_ROWS:CARRY_ROWS + tm, :] = acc

        def store_up(acc):
            u_cur[...] = acc

        _interleaved_projections(x_ref, (wgb_ref, wub_ref), (store_gate, store_up), epilogue_chunk, tm)

    @pl.when(lax.rem(t, 2) == 0)
    def _():
        step(g0_ref, u0_ref, g1_ref, u1_ref)

    @pl.when(lax.rem(t, 2) == 1)
    def _():
        step(g1_ref, u1_ref, g0_ref, u0_ref)


def ffn_up(xn, w_gu, conv_w, conv_b, w_down, layer):
    m, k = xn.shape
    tm, tf = TM_MM, TF_FFN
    ni, nj = m // tm, D_FF // tf
    nt, cur_row, prev_row, prev_col = _tile_maps(ni, nj)

    def slab_col(t, s):
        return jnp.minimum(lax.div(jnp.minimum(t, nt - 1) + ni - 1 - s, ni), nj - 1)

    wd_rows = D_FF // nt
    assert wd_rows * nt == D_FF and wd_rows % (2 * V7X_SUBLANES) == 0
    nbytes = (2 * tm * k * 2 + 4 * k * tf * 4 + 2 * k * tf * 2 + 2 * tm * tf * 2
              + 4 * (tm + CARRY_ROWS) * tf * 4 + 2 * wd_rows * D_MODEL * 6)
    return pl.pallas_call(
        functools.partial(_ffn_up_kernel, tm=tm, ni=ni, nt=nt),
        out_shape=(jax.ShapeDtypeStruct((m, D_FF), BF16), jax.ShapeDtypeStruct((D_FF, D_MODEL), BF16)),
        grid=(nt + 1,),
        in_specs=[pl.BlockSpec((tm, k), lambda t: (cur_row(t), 0))]
        + _slab_specs(w_gu, layer, k, tf, ni, slab_col)
        + _slab_specs(w_gu, layer, k, tf, ni, lambda t, s: slab_col(t, s) + nj)
        + [pl.BlockSpec((None, FFN_CONV, tf), lambda t: (layer, 0, prev_col(t))),
                  pl.BlockSpec((None, 1, tf), lambda t: (layer, 0, prev_col(t))),
                  pl.BlockSpec((None, wd_rows, D_MODEL), lambda t: (layer, jnp.minimum(t, nt - 1), 0))],
        out_specs=(pl.BlockSpec((tm, tf), lambda t: (prev_row(t), prev_col(t))),
                   pl.BlockSpec((wd_rows, D_MODEL), lambda t: (jnp.minimum(t, nt - 1), 0))),
        scratch_shapes=[pltpu.VMEM((k, tf), BF16), pltpu.VMEM((k, tf), BF16),
                        pltpu.VMEM((tm + CARRY_ROWS, tf), F32), pltpu.VMEM((tm + CARRY_ROWS, tf), F32),
                        pltpu.VMEM((tm, tf), F32), pltpu.VMEM((tm, tf), F32)],
        compiler_params=_params(("arbitrary",), nbytes),
        name="ffn_up",
    )(xn, *([w_gu] * (2 * ni)), conv_w, conv_b.reshape(conv_b.shape[0], 1, D_FF), w_down)


def _in_odd_kernel(*refs, tm, ni, nt):
    x_ref, wy_refs, wx_refs = refs[0], refs[1:1 + ni], refs[1 + ni:1 + 2 * ni]
    (cw_ref, cb_ref, y_ref, xc_ref,
     wyb_ref, wxb_ref, b0_ref, b1_ref, r0_ref, r1_ref) = refs[1 + 2 * ni:]
    t = pl.program_id(0)
    i = lax.rem(t, ni)

    @pl.when(t == 0)
    def _():
        b1_ref[...] = jnp.zeros_like(b1_ref)
        r1_ref[...] = jnp.zeros_like(r1_ref)

    @pl.when(jnp.logical_and(i == 0, t < nt))
    def _():
        _cast_slabs(wy_refs, wyb_ref)
        _cast_slabs(wx_refs, wxb_ref)

    def step(b_cur, r_cur, b_prev, r_prev):
        b_cur[0:CARRY_ROWS, :] = jnp.where(i == 0, 0.0, b_prev[tm:tm + CARRY_ROWS, :])

        def epilogue_chunk(r0):
            y_ref[r0:r0 + EPI_ROWS, :] = jax.nn.gelu(r_prev[r0:r0 + EPI_ROWS, :],
                                                     approximate=True).astype(y_ref.dtype)
            xc_ref[r0:r0 + EPI_ROWS, :] = _causal_conv(b_prev, cw_ref, cb_ref, LRU_CONV, r0, EPI_ROWS)

        def store_y(acc):
            r_cur[...] = acc

        def store_x(acc):
            b_cur[CARRY_ROWS:CARRY_ROWS + tm, :] = acc

        _interleaved_projections(x_ref, (wyb_ref, wxb_ref), (store_y, store_x), epilogue_chunk, tm)

    @pl.when(lax.rem(t, 2) == 0)
    def _():
        step(b0_ref, r0_ref, b1_ref, r1_ref)

    @pl.when(lax.rem(t, 2) == 1)
    def _():
        step(b1_ref, r1_ref, b0_ref, r0_ref)


def in_odd(xn, w_in, conv_w, conv_b, layer):
    m, k = xn.shape
    tm, tn = TM_MM, TN_ODD
    ni, nj = m // tm, LRU_WIDTH // tn
    nt, cur_row, prev_row, prev_col = _tile_maps(ni, nj)
    nbytes = (2 * tm * k * 2 + 4 * k * tn * 4 + 2 * k * tn * 2 + 4 * tm * tn * 4
              + 4 * (tm + CARRY_ROWS) * tn * 4)

    def slab_col(t, s):
        return jnp.minimum(lax.div(jnp.minimum(t, nt - 1) + ni - 1 - s, ni), nj - 1)

    out = pl.BlockSpec((tm, tn), lambda t: (prev_row(t), prev_col(t)))
    return pl.pallas_call(
        functools.partial(_in_odd_kernel, tm=tm, ni=ni, nt=nt),
        out_shape=(jax.ShapeDtypeStruct((m, LRU_WIDTH), BF16), jax.ShapeDtypeStruct((m, LRU_WIDTH), F32)),
        grid=(nt + 1,),
        in_specs=[pl.BlockSpec((tm, k), lambda t: (cur_row(t), 0))]
        + _slab_specs(w_in, layer, k, tn, ni, slab_col)
        + _slab_specs(w_in, layer, k, tn, ni, lambda t, s: slab_col(t, s) + nj)
        + [pl.BlockSpec((None, LRU_CONV, tn), lambda t: (layer, 0, prev_col(t))),
                  pl.BlockSpec((None, 1, tn), lambda t: (layer, 0, prev_col(t)))],
        out_specs=(out, out),
        scratch_shapes=[pltpu.VMEM((k, tn), BF16), pltpu.VMEM((k, tn), BF16),
                        pltpu.VMEM((tm + CARRY_ROWS, tn), F32), pltpu.VMEM((tm + CARRY_ROWS, tn), F32),
                        pltpu.VMEM((tm, tn), F32), pltpu.VMEM((tm, tn), F32)],
        compiler_params=_params(("arbitrary",), nbytes),
        name="in_odd",
    )(xn, *([w_in] * (2 * ni)), conv_w, conv_b.reshape(conv_b.shape[0], 1, LRU_WIDTH))


def _lru_kernel(xc_ref, y_ref, wa_ref, wx_ref, ba_ref, bx_ref, ap_ref, o_ref, a_ref, b_ref, hc_ref, *, tr):
    step = pl.program_id(0)

    @pl.when(step == 0)
    def _():
        hc_ref[...] = jnp.zeros_like(hc_ref)

    valid = (step * tr + _iota((tr, 1), 0)) >= PAD
    for blk in range(LRU_BLOCKS):
        sl = slice(blk * LRU_BDIM, (blk + 1) * LRU_BDIM)
        x = xc_ref[:, sl]
        xb = x.astype(BF16)
        t_r = jnp.tanh(_dot(xb, wa_ref[blk]) + ba_ref[:, sl])
        t_i = jnp.tanh(_dot(xb, wx_ref[blk]) + bx_ref[:, sl])
        c = (LRU_C / 4.0) * jax.nn.log_sigmoid(ap_ref[:, sl])
        th = jnp.tanh(t_r * c + c)
        em = (th + th) / (1.0 - th)
        a_ref[:, sl] = 1.0 + em
        xh = 0.5 * x
        inp = jnp.sqrt(em * (-2.0 - em)) * (t_i * xh + xh)
        b_ref[:, sl] = jnp.where(valid, inp, 0.0)

    half = SCAN_ROWS // 2
    ridx = _iota((half, CW_LRU), 0)

    def local_scan(a, b):
        shift = 1
        while shift < half:
            ok = ridx >= shift
            b = jnp.where(ok, a * pltpu.roll(b, shift, 0) + b, b)
            a = jnp.where(ok, a * pltpu.roll(a, shift, 0), a)
            shift *= 2
        return a, b

    for c in range(LRU_WIDTH // CW_LRU):
        cs = slice(c * CW_LRU, (c + 1) * CW_LRU)

        def body(g, carry, cs=cs):
            r0 = pl.multiple_of(g * SCAN_ROWS, SCAN_ROWS)
            a = a_ref[pl.ds(r0, SCAN_ROWS), cs]
            b = b_ref[pl.ds(r0, SCAN_ROWS), cs]
            a_top, b_top = local_scan(a[:half], b[:half])
            a_bot, b_bot = local_scan(a[half:], b[half:])
            h_top = a_top * carry + b_top
            h_bot = a_bot * h_top[half - 1:half, :] + b_bot
            h = jnp.concatenate([h_top, h_bot], axis=0)
            o_ref[pl.ds(r0, SCAN_ROWS), cs] = (
                h * y_ref[pl.ds(r0, SCAN_ROWS), cs].astype(F32)).astype(o_ref.dtype)
            return h_bot[half - 1:half, :]

        hc_ref[0:1, cs] = lax.fori_loop(0, tr // SCAN_ROWS, body, hc_ref[0:1, cs])


def rglru(xc, y, w_a, b_a, w_x, b_x, a_param):
    m, d = xc.shape
    tr = TR_LRU
    row = pl.BlockSpec((tr, d), lambda i: (i, 0))
    vec = pl.BlockSpec((1, d), lambda i: (0, 0))
    wspec = pl.BlockSpec((LRU_BLOCKS, LRU_BDIM, LRU_BDIM), lambda i: (0, 0, 0))
    nbytes = 4 * tr * d * 4 + 2 * tr * d * 2 + 2 * tr * d * 4 + 4 * LRU_BLOCKS * LRU_BDIM * LRU_BDIM * 2
    return pl.pallas_call(
        functools.partial(_lru_kernel, tr=tr),
        out_shape=jax.ShapeDtypeStruct((m, d), BF16),
        grid=(m // tr,),
        in_specs=[row, row, wspec, wspec, vec, vec, vec],
        out_specs=row,
        scratch_shapes=[pltpu.VMEM((tr, d), F32), pltpu.VMEM((tr, d), F32),
                        pltpu.VMEM((V7X_SUBLANES, d), F32)],
        compiler_params=_params(("arbitrary",), nbytes),
        name="rglru",
    )(xc, y, (0.5 * w_a).astype(BF16), (0.5 * w_x).astype(BF16), 0.5 * b_a.reshape(1, d),
      0.5 * b_x.reshape(1, d), a_param.reshape(1, d))


def _hgrn_kernel(q_ref, f_ref, i_ref, g_ref, lbl_ref, gnw_ref, o_ref, st_ref, *, tr, layer_j):
    step = pl.program_id(1)

    @pl.when(step == 0)
    def _():
        st_ref[...] = jnp.zeros_like(st_ref)

    logits = lbl_ref[...]
    e = jnp.exp(logits - jnp.max(logits, axis=0, keepdims=True))
    lb = jnp.sum(e[0:layer_j + 1], axis=0, keepdims=True) / jnp.sum(e, axis=0, keepdims=True)

    width = HGRN_HEADS_PER_STEP * A_KDIM
    valid = (step * tr + _iota((tr, 1), 0)) >= PAD
    q = jax.nn.silu(q_ref[...].astype(F32))
    forget = lb + (1.0 - lb) * jax.nn.sigmoid(f_ref[...].astype(F32))
    k = jnp.where(valid, 1.0 - forget, 0.0)
    g = jnp.where(valid, jnp.log(forget), 0.0)
    v = i_ref[...]

    pos = _iota((tr, width), 0) & (HGRN_CHUNK - 1)
    b = g
    shift = 1
    while shift < HGRN_CHUNK:
        b = b + jnp.where(pos >= shift, pltpu.roll(b, shift, 0), 0.0)
        shift *= 2

    n_sub = HGRN_CHUNK // HGRN_SUB
    cpos = _iota((HGRN_CHUNK, A_KDIM), 0)
    causal = _iota((HGRN_CHUNK, HGRN_CHUNK), 0) >= _iota((HGRN_CHUNK, HGRN_CHUNK), 1)
    gnw = gnw_ref[...]
    heads = range(HGRN_HEADS_PER_STEP)
    chunks = range(tr // HGRN_CHUNK)
    pairs = [(hh, c) for hh in heads for c in chunks]

    def part(x, p):
        hh, c = p
        return x[c * HGRN_CHUNK:(c + 1) * HGRN_CHUNK, hh * A_KDIM:(hh + 1) * A_KDIM]

    bc = {p: part(b, p) for p in pairs}
    qc = {p: part(q, p) for p in pairs}
    kc = {p: part(k, p) for p in pairs}
    vcb = {p: part(v, p).astype(BF16) for p in pairs}
    b_last = {p: bc[p][HGRN_CHUNK - 1:HGRN_CHUNK] for p in pairs}
    u_t = {p: _dot_tn(vcb[p], (kc[p] * jnp.exp(b_last[p] - bc[p])).astype(BF16)) for p in pairs}
    att = {}
    for p in pairs:
        rows = []
        for i in range(n_sub):
            ss = slice(i * HGRN_SUB, (i + 1) * HGRN_SUB)
            ref = jnp.zeros((1, A_KDIM), F32) if i == 0 else bc[p][i * HGRN_SUB - 1:i * HGRN_SUB]
            q_sc = (qc[p][ss] * jnp.exp(bc[p][ss] - ref)).astype(BF16)
            expo = jnp.where(cpos < (i + 1) * HGRN_SUB, ref - bc[p], 0.0)
            k_sc = (kc[p] * jnp.exp(expo)).astype(BF16)
            rows.append(_dot_nt(q_sc, k_sc))
        att[p] = jnp.where(causal, jnp.concatenate(rows, axis=0), 0.0).astype(BF16)
    o_intra = {p: _dot(att[p], vcb[p]) for p in pairs}
    states = {}
    for hh in heads:
        st = st_ref[hh]
        for c in chunks:
            states[(hh, c)] = st.astype(BF16)
            st = st * jnp.exp(b_last[(hh, c)]) + u_t[(hh, c)]
        st_ref[hh] = st
    o_inter = {p: _dot_nt((qc[p] * jnp.exp(bc[p])).astype(BF16), states[p]) for p in pairs}
    for hh, c in pairs:
        rs = slice(c * HGRN_CHUNK, (c + 1) * HGRN_CHUNK)
        cs = slice(hh * A_VDIM, (hh + 1) * A_VDIM)
        gate = jax.nn.silu(g_ref[rs, cs].astype(F32))
        o_ref[rs, cs] = (_rms(o_inter[(hh, c)] + o_intra[(hh, c)], gnw) * gate).astype(o_ref.dtype)


def hgrn2(hproj, lb_logits, gn_w, layer_j):
    m = hproj.shape[0]
    tr = TR_HGRN
    nrow = lb_logits.shape[0]
    hps = HGRN_HEADS_PER_STEP
    groups = A_HEADS // hps

    def col(off):
        return pl.BlockSpec((tr, hps * A_KDIM), lambda h, t, off=off: (t, h + off * groups))

    nbytes = 2 * 4 * tr * hps * A_KDIM * 2 + 2 * tr * hps * A_VDIM * 2 + 16 * tr * hps * A_KDIM * 4
    return pl.pallas_call(
        functools.partial(_hgrn_kernel, tr=tr, layer_j=layer_j),
        out_shape=jax.ShapeDtypeStruct((m, A_WIDTH), BF16),
        grid=(groups, m // tr),
        in_specs=[col(0), col(1), col(2), col(3),
                  pl.BlockSpec((nrow, hps * A_KDIM), lambda h, t: (0, h)),
                  pl.BlockSpec((1, A_VDIM), lambda h, t: (0, 0))],
        out_specs=pl.BlockSpec((tr, hps * A_VDIM), lambda h, t: (t, h)),
        scratch_shapes=[pltpu.VMEM((hps, A_VDIM, A_KDIM), F32)],
        compiler_params=_params(("arbitrary", "arbitrary"), nbytes),
        name="hgrn2",
    )(hproj, hproj, hproj, hproj, lb_logits, gn_w.reshape(1, A_VDIM))


PAIRS = B_GROUP // 2
QROWS = PAIRS * ATTN_BLOCK


def _swa_kernel(sink_ref, q_ref, kc_ref, kp_ref, km_ref, vc_ref, vp_ref, vm_ref, o_ref):
    n = pl.program_id(0)
    scale = B_HDIM ** -0.5
    lane_lo = _iota((1, V7X_LANES), 1) < B_HDIM
    tq = _iota((QROWS, ATTN_BLOCK), 0) & (ATTN_BLOCK - 1)
    sk = _iota((QROWS, ATTN_BLOCK), 1)
    masks = (sk > tq + jnp.where(n >= 2, 0, 2 * ATTN_BLOCK),
             sk <= tq + jnp.where(n >= 1, 0, -2 * ATTN_BLOCK),
             jnp.logical_and(sk >= PAD, sk <= tq + jnp.where(n >= 1, ATTN_BLOCK, 0)))

    def split(x, natural_lo, fill):
        rolled = pltpu.roll(x, B_HDIM, 1)
        lo_src, hi_src = (x, rolled) if natural_lo else (rolled, x)
        return (jnp.where(lane_lo, lo_src, fill).astype(BF16), jnp.where(lane_lo, fill, hi_src).astype(BF16))

    chains = [(h, par) for h in range(B_KVHEADS) for par in range(2)]
    keys, vals, qs = {}, {}, {}
    for h in range(B_KVHEADS):
        tile = slice((h // 2) * V7X_LANES, (h // 2 + 1) * V7X_LANES)
        nat = h % 2 == 0
        keys[h] = [split(r[:, tile].astype(F32), nat, 0.0) for r in (kp_ref, kc_ref, km_ref)]
        vals[h] = [split(r[:, tile].astype(F32), nat, 1.0) for r in (vp_ref, vc_ref, vm_ref)]
        qs[h] = (jnp.concatenate(
            [q_ref[:, (h * PAIRS + p) * V7X_LANES:(h * PAIRS + p + 1) * V7X_LANES] for p in range(PAIRS)],
            axis=0).astype(F32) * scale).astype(BF16)
    logits = {c: [jnp.where(m, _dot_nt(qs[c[0]], kk[c[1]]), NEG_INF) for m, kk in zip(masks, keys[c[0]])]
              for c in chains}
    sinks = {(h, par): jnp.concatenate(
        [jnp.full((ATTN_BLOCK, V7X_LANES), sink_ref[h * B_GROUP + 2 * p + par], F32) for p in range(PAIRS)],
        axis=0) for h, par in chains}
    mx = {c: jnp.maximum(jnp.broadcast_to(
        jnp.max(jnp.maximum(jnp.maximum(logits[c][0], logits[c][1]), logits[c][2]), axis=-1, keepdims=True),
        (QROWS, V7X_LANES)), sinks[c]) for c in chains}
    pv = {}
    for c in chains:
        for lg, vv in zip(logits[c], vals[c[0]]):
            part = _dot(jnp.exp(lg - mx[c]).astype(BF16), vv[c[1]])
            pv[c] = part if c not in pv else pv[c] + part
    out = {}
    for c in chains:
        den = pltpu.roll(pv[c], B_HDIM, 1) + jnp.exp(sinks[c] - mx[c])
        own_half = lane_lo if c[1] == 0 else jnp.logical_not(lane_lo)
        part = jnp.where(own_half, pv[c] / den, 0.0)
        out[c[0]] = part if c[0] not in out else out[c[0]] + part
    for h in range(B_KVHEADS):
        for p in range(PAIRS):
            o_ref[:, (h * PAIRS + p) * V7X_LANES:(h * PAIRS + p + 1) * V7X_LANES] = (
                out[h][p * ATTN_BLOCK:(p + 1) * ATTN_BLOCK].astype(o_ref.dtype))


def swa(hproj, sinks):
    m = hproj.shape[0]
    blk = ATTN_BLOCK
    q_col = (2 * A_FDIM + 2 * A_WIDTH) // B_WIDTH
    k_col = (2 * A_FDIM + 2 * A_WIDTH + B_WIDTH) // B_KVWIDTH
    v_col = k_col + 1
    qspec = pl.BlockSpec((blk, B_WIDTH), lambda n: (n, q_col))

    def kv(col, which):
        if which == "cur":
            return pl.BlockSpec((blk, B_KVWIDTH), lambda n: (n, col))
        if which == "prev":
            return pl.BlockSpec((blk, B_KVWIDTH), lambda n: (jnp.maximum(n - 1, 0), col))
        return pl.BlockSpec((blk, B_KVWIDTH), lambda n: (0, col))

    nbytes = 2 * blk * B_WIDTH * 4 + 12 * blk * B_KVWIDTH * 4 + 2 * blk * B_WIDTH * 2 + 40 * QROWS * 128 * 4
    return pl.pallas_call(
        _swa_kernel,
        out_shape=jax.ShapeDtypeStruct((m, B_WIDTH), BF16),
        grid=(m // blk,),
        in_specs=[pl.BlockSpec(memory_space=pltpu.SMEM), qspec,
                  kv(k_col, "cur"), kv(k_col, "prev"), kv(k_col, "meta"),
                  kv(v_col, "cur"), kv(v_col, "prev"), kv(v_col, "meta")],
        out_specs=pl.BlockSpec((blk, B_WIDTH), lambda n: (n, 0)),
        compiler_params=_params(("arbitrary",), nbytes),
        name="swa",
    )(sinks, hproj, hproj, hproj, hproj, hproj, hproj, hproj)


def kernel(x, meta_tokens, norm_w, w_in_even, lb_logits, hgrn_gn_w, attn_sinks, w_out_even,
           w_in_odd, lru_conv_w, lru_conv_b, lru_wa, lru_ba, lru_wx, lru_bx, lru_a_param, w_out_odd,
           ffn_w_gu, ffn_conv_w, ffn_conv_b, ffn_w_down):
    assert x.shape == (1, SEQ, D_MODEL) and norm_w.shape[0] == DEPTH and w_in_even.shape[-1] == EVEN_IN
    x2d = x[0]
    xn = rms_cast_stream(x2d, meta_tokens, norm_w[0, 0])
    h, out = None, None
    for layer in range(DEPTH):
        j = layer // 2
        if layer % 2 == 0:
            hproj = matmul([xn], w_in_even, layer=j, tm=TM_IN_EVEN, tn=TN_IN_EVEN, out_dtype=BF16,
                           name="in_even")
            o_a = hgrn2(hproj, lb_logits, hgrn_gn_w[j], j)
            o_b = swa(hproj, attn_sinks[j])
            mix = matmul([o_a, o_b], w_out_even, layer=j, tm=TM_OUT, tn=TN_OUT, out_dtype=BF16,
                         name="out_even")
        else:
            y_br, x_br = in_odd(xn, w_in_odd, lru_conv_w, lru_conv_b, j)
            rec = rglru(x_br, y_br, lru_wa[j], lru_ba[j], lru_wx[j], lru_bx[j], lru_a_param[j])
            mix = matmul([rec], w_out_odd, layer=j, tm=TM_OUT, tn=TN_OUT, out_dtype=BF16, name="out_odd")
        if h is None:
            h, xn = resid_norm_stream(x2d, meta_tokens, mix, norm_w[layer, 1], norm_w[layer, 2])
        else:
            h, xn = resid_norm(h, mix, norm_w[layer, 1], norm_w[layer, 2])
        act, w_down_bf16 = ffn_up(xn, ffn_w_gu, ffn_conv_w, ffn_conv_b, ffn_w_down, layer)
        ff = matmul([act], w_down_bf16, tm=TM_DOWN, tn=TN_DOWN, out_dtype=BF16, name="ffn_down")
        if layer + 1 < DEPTH:
            h, xn = resid_norm(h, ff, norm_w[layer, 3], norm_w[layer + 1, 0])
        else:
            out = resid_final(h, ff, norm_w[layer, 3])
    return out[None]
```

```python
import functools

import jax
import jax.numpy as jnp
from jax import lax
from jax.experimental import pallas as pl
from jax.experimental.pallas import tpu as pltpu

F32 = jnp.float32
BF16 = jnp.bfloat16

D_MODEL = 4096
SEQ = 8192
DEPTH = 2
N_META = 16
A_HEADS = 16
A_KDIM = 128
A_VDIM = D_MODEL // 2 // A_HEADS
A_FDIM = A_HEADS * A_KDIM
A_WIDTH = A_HEADS * A_VDIM
HGRN_CHUNK = 64
HGRN_SUB = 16
B_HDIM = 64
B_QHEADS = D_MODEL // 2 // B_HDIM
B_KVHEADS = B_QHEADS // 8
B_GROUP = B_QHEADS // B_KVHEADS
B_WIDTH = B_QHEADS * B_HDIM
B_KVWIDTH = B_KVHEADS * B_HDIM
WINDOW = 128
ATTN_BLOCK = 128
EVEN_IN = 2 * A_FDIM + 2 * A_WIDTH + B_WIDTH + 2 * B_KVWIDTH
LRU_WIDTH = D_MODEL
LRU_BLOCKS = 16
LRU_BDIM = LRU_WIDTH // LRU_BLOCKS
LRU_CONV = 4
LRU_C = 8.0
D_FF = 256 * ((8 * D_MODEL // 3 + 255) // 256)
FFN_CONV = 3
NORM_EPS = 1e-6
NEG_INF = -1e30

PAD = ATTN_BLOCK - N_META
ROW0 = PAD + N_META
LP = ROW0 + SEQ
assert PAD % HGRN_CHUNK == HGRN_CHUNK - N_META and ROW0 == ATTN_BLOCK
assert WINDOW == ATTN_BLOCK

V7X_LANES = 128
V7X_SUBLANES = 8
V7X_VMEM_LIMIT_CAP = 60 * 1024 * 1024
CARRY_ROWS = V7X_SUBLANES

TM_MM = 1040
TM_IN_EVEN = 1040
TN_IN_EVEN = 512
TM_OUT = 520
TN_OUT = 1024
TM_DOWN = 520
TN_DOWN = 512
TF_FFN = 256
TN_ODD = 256
EPI_ROWS = 80
K_CHUNK = 256
MAX_WEIGHT_SLABS = 8
TR_NORM = 320
STREAM_GROUPS = 5
FINAL_GROUPS = 4
TR_LRU = 208
CW_LRU = 1024
TR_HGRN = 640
HGRN_HEADS_PER_STEP = 4
SCAN_ROWS = 16


def _vmem_limit(nbytes):
    return int(min(V7X_VMEM_LIMIT_CAP, nbytes * 1.15 + (4 << 20)))


def _params(sem, nbytes):
    return pltpu.CompilerParams(dimension_semantics=sem, vmem_limit_bytes=_vmem_limit(nbytes))


def _rms(x, w):
    x = x.astype(F32)
    return x * lax.rsqrt(jnp.mean(x * x, axis=-1, keepdims=True) + NORM_EPS) * w


def _iota(shape, dim):
    return lax.broadcasted_iota(jnp.int32, shape, dim)


def _dot(a, b):
    return jnp.dot(a, b, preferred_element_type=F32)


def _dot_nt(a, b):
    return lax.dot_general(a, b, (((1,), (1,)), ((), ())), preferred_element_type=F32)


def _dot_tn(a, b):
    return lax.dot_general(a, b, (((0,), (0,)), ((), ())), preferred_element_type=F32)


def _stream_rows(n, x_ref, meta_ref):
    first = jnp.concatenate([jnp.zeros((PAD, D_MODEL), F32), meta_ref[...].astype(F32)], axis=0)
    return jnp.where(n == 0, first, x_ref[...].astype(F32))


def _stream_specs():
    blk = ATTN_BLOCK
    return [pl.BlockSpec((blk, D_MODEL), lambda n: (jnp.maximum(n - 1, 0), 0)),
            pl.BlockSpec((N_META, D_MODEL), lambda n: (0, 0))]


def _rms_cast_stream_kernel(*refs, groups):
    x_refs, meta_ref, w_ref, o_ref = refs[:groups], refs[groups], refs[groups + 1], refs[groups + 2]
    blk = ATTN_BLOCK
    for q, x_ref in enumerate(x_refs):
        rows = _stream_rows(pl.program_id(0), x_ref, meta_ref) if q == 0 else x_ref[...].astype(F32)
        o_ref[q * blk:(q + 1) * blk, :] = _rms(rows, w_ref[...]).astype(o_ref.dtype)


def rms_cast_stream(x2d, meta, w):
    blk, d, groups = ATTN_BLOCK, D_MODEL, STREAM_GROUPS
    rows = blk * groups
    x_specs = [pl.BlockSpec((blk, d), lambda n, q=q: (jnp.maximum(groups * n + q - 1, 0), 0))
               for q in range(groups)]
    return pl.pallas_call(
        functools.partial(_rms_cast_stream_kernel, groups=groups),
        out_shape=jax.ShapeDtypeStruct((LP, d), BF16),
        grid=(LP // rows,),
        in_specs=x_specs + [pl.BlockSpec((N_META, d), lambda n: (0, 0)), pl.BlockSpec((1, d), lambda n: (0, 0))],
        out_specs=pl.BlockSpec((rows, d), lambda n: (n, 0)),
        compiler_params=_params(("arbitrary",), 2 * rows * d * 6),
        name="rms_cast_stream",
    )(*([x2d] * groups), meta, w.reshape(1, d))


def _resid_norm_stream_kernel(x_ref, meta_ref, y_ref, wp_ref, wn_ref, ho_ref, xo_ref):
    h = _stream_rows(pl.program_id(0), x_ref, meta_ref) + _rms(y_ref[...], wp_ref[...])
    ho_ref[...] = h
    xo_ref[...] = _rms(h, wn_ref[...]).astype(xo_ref.dtype)


def resid_norm_stream(x2d, meta, y, w_post, w_next):
    blk, d = ATTN_BLOCK, D_MODEL
    row = pl.BlockSpec((blk, d), lambda n: (n, 0))
    vec = pl.BlockSpec((1, d), lambda n: (0, 0))
    return pl.pallas_call(
        _resid_norm_stream_kernel,
        out_shape=(jax.ShapeDtypeStruct((LP, d), F32), jax.ShapeDtypeStruct((LP, d), BF16)),
        grid=(LP // blk,),
        in_specs=_stream_specs() + [row, vec, vec],
        out_specs=(row, row),
        compiler_params=_params(("arbitrary",), 2 * blk * d * 14 + 4 * blk * d * 4),
        name="resid_norm_stream",
    )(x2d, meta, y, w_post.reshape(1, d), w_next.reshape(1, d))


def _resid_norm_kernel(h_ref, y_ref, wp_ref, wn_ref, ho_ref, xo_ref):
    h = h_ref[...] + _rms(y_ref[...], wp_ref[...])
    ho_ref[...] = h
    xo_ref[...] = _rms(h, wn_ref[...]).astype(xo_ref.dtype)


def resid_norm(h, y, w_post, w_next):
    n, d = h.shape
    tr = TR_NORM
    row = pl.BlockSpec((tr, d), lambda i: (i, 0))
    vec = pl.BlockSpec((1, d), lambda i: (0, 0))
    return pl.pallas_call(
        _resid_norm_kernel,
        out_shape=(jax.ShapeDtypeStruct((n, d), F32), jax.ShapeDtypeStruct((n, d), BF16)),
        grid=(n // tr,),
        in_specs=[row, row, vec, vec],
        out_specs=(row, row),
        compiler_params=_params(("arbitrary",), 2 * tr * d * 14),
        name="resid_norm",
    )(h, y, w_post.reshape(1, d), w_next.reshape(1, d))


def _resid_final_kernel(*refs, groups):
    h_refs, y_refs, wp_ref, o_ref = refs[:groups], refs[groups:2 * groups], refs[2 * groups], refs[2 * groups + 1]
    blk = ATTN_BLOCK
    for q in range(groups):
        o_ref[q * blk:(q + 1) * blk, :] = h_refs[q][...] + _rms(y_refs[q][...], wp_ref[...])


def resid_final(h, y, w_post):
    n, d = h.shape
    blk, groups = ATTN_BLOCK, FINAL_GROUPS
    rows = blk * groups
    off = ROW0 // blk
    src = [pl.BlockSpec((blk, d), lambda i, q=q: (groups * i + q + off, 0)) for q in range(groups)]
    return pl.pallas_call(
        functools.partial(_resid_final_kernel, groups=groups),
        out_shape=jax.ShapeDtypeStruct((n - ROW0, d), F32),
        grid=((n - ROW0) // rows,),
        in_specs=src + src + [pl.BlockSpec((1, d), lambda i: (0, 0))],
        out_specs=pl.BlockSpec((rows, d), lambda i: (i, 0)),
        compiler_params=_params(("arbitrary",), 2 * rows * d * 10),
        name="resid_final",
    )(*([h] * groups), *([y] * groups), w_post.reshape(1, d))


def _slab_specs(w, layer, k, tn, n_slabs, col_of):
    rows = k // n_slabs
    assert rows * n_slabs == k

    def spec(s):
        if w.ndim == 3:
            return pl.BlockSpec((None, rows, tn), lambda *g: (layer, s, col_of(*g, s)))
        return pl.BlockSpec((rows, tn), lambda *g: (s, col_of(*g, s)))

    return [spec(s) for s in range(n_slabs)]


def _cast_slabs(slab_refs, dst_ref):
    rows = slab_refs[0].shape[0]
    for s, ref in enumerate(slab_refs):
        dst_ref[s * rows:(s + 1) * rows, :] = ref[...].astype(BF16)


def _mm_kernel(*refs, n_lhs, n_slabs, cast):
    x_refs, w_refs, o_ref = refs[:n_lhs], refs[n_lhs:n_lhs + n_slabs], refs[n_lhs + n_slabs]
    if cast:
        w_ref = refs[n_lhs + n_slabs + 1]

        @pl.when(pl.program_id(1) == 0)
        def _():
            _cast_slabs(w_refs, w_ref)
    else:
        (w_ref,) = w_refs
    acc, k0 = None, 0
    for x_ref in x_refs:
        kk = x_ref.shape[1]
        part = _dot(x_ref[...], w_ref[k0:k0 + kk, :])
        acc = part if acc is None else acc + part
        k0 += kk
    o_ref[...] = acc.astype(o_ref.dtype)


def matmul(xs, w, *, layer=0, tm, tn, out_dtype=F32, name="matmul"):
    m = xs[0].shape[0]
    k, n = w.shape[-2:]
    assert sum(x.shape[1] for x in xs) == k
    cast = w.dtype != BF16
    ni, nj = m // tm, n // tn
    wbytes = k * tn * (4 if cast else 2)
    nbytes = 2 * tm * k * 2 + 2 * wbytes + (k * tn * 2 if cast else 0) + 2 * tm * tn * 4
    if cast and ni <= MAX_WEIGHT_SLABS:
        n_slabs = ni
        w_specs = _slab_specs(w, layer, k, tn, n_slabs,
                              lambda j, i, s: jnp.minimum(j + (i + ni - 1 - s) // ni, nj - 1))
    else:
        n_slabs = 1
        w_specs = _slab_specs(w, layer, k, tn, 1, lambda j, i, s: j)
    return pl.pallas_call(
        functools.partial(_mm_kernel, n_lhs=len(xs), n_slabs=n_slabs, cast=cast),
        out_shape=jax.ShapeDtypeStruct((m, n), out_dtype),
        grid=(nj, ni),
        in_specs=[pl.BlockSpec((tm, x.shape[1]), lambda j, i: (i, 0)) for x in xs] + w_specs,
        out_specs=pl.BlockSpec((tm, tn), lambda j, i: (i, j)),
        scratch_shapes=[pltpu.VMEM((k, tn), BF16)] if cast else [],
        compiler_params=_params(("arbitrary", "arbitrary"), nbytes),
        name=name,
    )(*xs, *([w] * n_slabs))


def _tile_maps(ni, nj):
    nt = ni * nj

    def cur_row(t):
        return lax.rem(jnp.minimum(t, nt - 1), ni)

    def prev_row(t):
        return lax.rem(jnp.maximum(t - 1, 0), ni)

    def prev_col(t):
        return lax.div(jnp.maximum(t - 1, 0), ni)

    return nt, cur_row, prev_row, prev_col


def _interleaved_projections(x_ref, w_refs, dst_stores, epilogue_chunk, tm):
    nk = x_ref.shape[1] // K_CHUNK
    chunk_rows = list(range(0, tm, EPI_ROWS))
    slots, slot, done = len(w_refs) * nk, 0, 0
    for w_ref, store in zip(w_refs, dst_stores):
        acc = None
        for kk in range(nk):
            ks = slice(kk * K_CHUNK, (kk + 1) * K_CHUNK)
            part = _dot(x_ref[:, ks], w_ref[ks, :])
            acc = part if acc is None else acc + part
            slot += 1
            while done < len(chunk_rows) * slot // slots:
                epilogue_chunk(chunk_rows[done])
                done += 1
        store(acc)


def _causal_conv(buf_ref, cw_ref, cb_ref, width, r0, rows):
    win = buf_ref[r0:r0 + CARRY_ROWS + rows, :]
    conv = cb_ref[...]
    for tap in range(width):
        back = width - 1 - tap
        shifted = pltpu.roll(win, back, 0) if back else win
        conv = conv + shifted[CARRY_ROWS:, :] * cw_ref[tap:tap + 1, :]
    return conv


def _ffn_up_kernel(*refs, tm, ni, nt):
    x_ref, wg_refs, wu_refs = refs[0], refs[1:1 + ni], refs[1 + ni:1 + 2 * ni]
    (cw_ref, cb_ref, wd_ref, o_ref, wdb_ref,
     wgb_ref, wub_ref, g0_ref, g1_ref, u0_ref, u1_ref) = refs[1 + 2 * ni:]
    t = pl.program_id(0)
    i = lax.rem(t, ni)

    @pl.when(t == 0)
    def _():
        g1_ref[...] = jnp.zeros_like(g1_ref)
        u1_ref[...] = jnp.zeros_like(u1_ref)

    @pl.when(jnp.logical_and(i == 0, t < nt))
    def _():
        _cast_slabs(wg_refs, wgb_ref)
        _cast_slabs(wu_refs, wub_ref)

    wdb_ref[...] = wd_ref[...].astype(BF16)

    def step(g_cur, u_cur, g_prev, u_prev):
        g_cur[0:CARRY_ROWS, :] = jnp.where(i == 0, 0.0, g_prev[tm:tm + CARRY_ROWS, :])

        def epilogue_chunk(r0):
            conv = _causal_conv(g_prev, cw_ref, cb_ref, FFN_CONV, r0, EPI_ROWS)
            o_ref[r0:r0 + EPI_ROWS, :] = (jax.nn.gelu(conv, approximate=True)
                                          * u_prev[r0:r0 + EPI_ROWS, :]).astype(o_ref.dtype)

        def store_gate(acc):
            g_cur[CARRY_ROWS:CARRY_ROWS + tm, :] = acc

        def store_up(acc):
            u_cur[...] = acc

        _interleaved_projections(x_ref, (wgb_ref, wub_ref), (store_gate, store_up), epilogue_chunk, tm)

    @pl.when(lax.rem(t, 2) == 0)
    def _():
        step(g0_ref, u0_ref, g1_ref, u1_ref)

    @pl.when(lax.rem(t, 2) == 1)
    def _():
        step(g1_ref, u1_ref, g0_ref, u0_ref)


def ffn_up(xn, w_gu, conv_w, conv_b, w_down, layer):
    m, k = xn.shape
    tm, tf = TM_MM, TF_FFN
    ni, nj = m // tm, D_FF // tf
    nt, cur_row, prev_row, prev_col = _tile_maps(ni, nj)

    def slab_col(t, s):
        return jnp.minimum(lax.div(jnp.minimum(t, nt - 1) + ni - 1 - s, ni), nj - 1)

    wd_rows = D_FF // nt
    assert wd_rows * nt == D_FF and wd_rows % (2 * V7X_SUBLANES) == 0
    nbytes = (2 * tm * k * 2 + 4 * k * tf * 4 + 2 * k * tf * 2 + 2 * tm * tf * 2
              + 4 * (tm + CARRY_ROWS) * tf * 4 + 2 * wd_rows * D_MODEL * 6)
    return pl.pallas_call(
        functools.partial(_ffn_up_kernel, tm=tm, ni=ni, nt=nt),
        out_shape=(jax.ShapeDtypeStruct((m, D_FF), BF16), jax.ShapeDtypeStruct((D_FF, D_MODEL), BF16)),
        grid=(nt + 1,),
        in_specs=[pl.BlockSpec((tm, k), lambda t: (cur_row(t), 0))]
        + _slab_specs(w_gu, layer, k, tf, ni, slab_col)
        + _slab_specs(w_gu, layer, k, tf, ni, lambda t, s: slab_col(t, s) + nj)
        + [pl.BlockSpec((None, FFN_CONV, tf), lambda t: (layer, 0, prev_col(t))),
                  pl.BlockSpec((None, 1, tf), lambda t: (layer, 0, prev_col(t))),
                  pl.BlockSpec((None, wd_rows, D_MODEL), lambda t: (layer, jnp.minimum(t, nt - 1), 0))],
        out_specs=(pl.BlockSpec((tm, tf), lambda t: (prev_row(t), prev_col(t))),
                   pl.BlockSpec((wd_rows, D_MODEL), lambda t: (jnp.minimum(t, nt - 1), 0))),
        scratch_shapes=[pltpu.VMEM((k, tf), BF16), pltpu.VMEM((k, tf), BF16),
                        pltpu.VMEM((tm + CARRY_ROWS, tf), F32), pltpu.VMEM((tm + CARRY_ROWS, tf), F32),
                        pltpu.VMEM((tm, tf), F32), pltpu.VMEM((tm, tf), F32)],
        compiler_params=_params(("arbitrary",), nbytes),
        name="ffn_up",
    )(xn, *([w_gu] * (2 * ni)), conv_w, conv_b.reshape(conv_b.shape[0], 1, D_FF), w_down)


def _in_odd_kernel(*refs, tm, ni, nt):
    x_ref, wy_refs, wx_refs = refs[0], refs[1:1 + ni], refs[1 + ni:1 + 2 * ni]
    (cw_ref, cb_ref, y_ref, xc_ref,
     wyb_ref, wxb_ref, b0_ref, b1_ref, r0_ref, r1_ref) = refs[1 + 2 * ni:]
    t = pl.program_id(0)
    i = lax.rem(t, ni)

    @pl.when(t == 0)
    def _():
        b1_ref[...] = jnp.zeros_like(b1_ref)
        r1_ref[...] = jnp.zeros_like(r1_ref)

    @pl.when(jnp.logical_and(i == 0, t < nt))
    def _():
        _cast_slabs(wy_refs, wyb_ref)
        _cast_slabs(wx_refs, wxb_ref)

    def step(b_cur, r_cur, b_prev, r_prev):
        b_cur[0:CARRY_ROWS, :] = jnp.where(i == 0, 0.0, b_prev[tm:tm + CARRY_ROWS, :])

        def epilogue_chunk(r0):
            y_ref[r0:r0 + EPI_ROWS, :] = jax.nn.gelu(r_prev[r0:r0 + EPI_ROWS, :],
                                                     approximate=True).astype(y_ref.dtype)
            xc_ref[r0:r0 + EPI_ROWS, :] = _causal_conv(b_prev, cw_ref, cb_ref, LRU_CONV, r0, EPI_ROWS)

        def store_y(acc):
            r_cur[...] = acc

        def store_x(acc):
            b_cur[CARRY_ROWS:CARRY_ROWS + tm, :] = acc

        _interleaved_projections(x_ref, (wyb_ref, wxb_ref), (store_y, store_x), epilogue_chunk, tm)

    @pl.when(lax.rem(t, 2) == 0)
    def _():
        step(b0_ref, r0_ref, b1_ref, r1_ref)

    @pl.when(lax.rem(t, 2) == 1)
    def _():
        step(b1_ref, r1_ref, b0_ref, r0_ref)


def in_odd(xn, w_in, conv_w, conv_b, layer):
    m, k = xn.shape
    tm, tn = TM_MM, TN_ODD
    ni, nj = m // tm, LRU_WIDTH // tn
    nt, cur_row, prev_row, prev_col = _tile_maps(ni, nj)
    nbytes = (2 * tm * k * 2 + 4 * k * tn * 4 + 2 * k * tn * 2 + 4 * tm * tn * 4
              + 4 * (tm + CARRY_ROWS) * tn * 4)

    def slab_col(t, s):
        return jnp.minimum(lax.div(jnp.minimum(t, nt - 1) + ni - 1 - s, ni), nj - 1)

    out = pl.BlockSpec((tm, tn), lambda t: (prev_row(t), prev_col(t)))
    return pl.pallas_call(
        functools.partial(_in_odd_kernel, tm=tm, ni=ni, nt=nt),
        out_shape=(jax.ShapeDtypeStruct((m, LRU_WIDTH), BF16), jax.ShapeDtypeStruct((m, LRU_WIDTH), F32)),
        grid=(nt + 1,),
        in_specs=[pl.BlockSpec((tm, k), lambda t: (cur_row(t), 0))]
        + _slab_specs(w_in, layer, k, tn, ni, slab_col)
        + _slab_specs(w_in, layer, k, tn, ni, lambda t, s: slab_col(t, s) + nj)
        + [pl.BlockSpec((None, LRU_CONV, tn), lambda t: (layer, 0, prev_col(t))),
                  pl.BlockSpec((None, 1, tn), lambda t: (layer, 0, prev_col(t)))],
        out_specs=(out, out),
        scratch_shapes=[pltpu.VMEM((k, tn), BF16), pltpu.VMEM((k, tn), BF16),
                        pltpu.VMEM((tm + CARRY_ROWS, tn), F32), pltpu.VMEM((tm + CARRY_ROWS, tn), F32),
                        pltpu.VMEM((tm, tn), F32), pltpu.VMEM((tm, tn), F32)],
        compiler_params=_params(("arbitrary",), nbytes),
        name="in_odd",
    )(xn, *([w_in] * (2 * ni)), conv_w, conv_b.reshape(conv_b.shape[0], 1, LRU_WIDTH))


def _lru_kernel(xc_ref, y_ref, wa_ref, wx_ref, ba_ref, bx_ref, ap_ref, o_ref, a_ref, b_ref, hc_ref, *, tr):
    step = pl.program_id(0)

    @pl.when(step == 0)
    def _():
        hc_ref[...] = jnp.zeros_like(hc_ref)

    for blk in range(LRU_BLOCKS):
        sl = slice(blk * LRU_BDIM, (blk + 1) * LRU_BDIM)
        x = xc_ref[:, sl]
        xb = x.astype(BF16)
        t_r = jnp.tanh(_dot(xb, wa_ref[blk]) + ba_ref[:, sl])
        t_i = jnp.tanh(_dot(xb, wx_ref[blk]) + bx_ref[:, sl])
        c = (LRU_C / 4.0) * jax.nn.log_sigmoid(ap_ref[:, sl])
        th = jnp.tanh(t_r * c + c)
        em = (th + th) / (1.0 - th)
        a_ref[:, sl] = 1.0 + em
        xh = 0.5 * x
        inp = jnp.sqrt(em * (-2.0 - em)) * (t_i * xh + xh)
        b_ref[:, sl] = inp

    @pl.when(step == 0)
    def _():
        b_ref[0:PAD, :] = jnp.zeros((PAD, b_ref.shape[1]), F32)

    half = SCAN_ROWS // 2
    ridx = _iota((half, CW_LRU), 0)

    def local_scan(a, b):
        shift = 1
        while shift < half:
            ok = ridx >= shift
            b = jnp.where(ok, a * pltpu.roll(b, shift, 0) + b, b)
            a = jnp.where(ok, a * pltpu.roll(a, shift, 0), a)
            shift *= 2
        return a, b

    for c in range(LRU_WIDTH // CW_LRU):
        cs = slice(c * CW_LRU, (c + 1) * CW_LRU)

        def body(g, carry, cs=cs):
            r0 = pl.multiple_of(g * SCAN_ROWS, SCAN_ROWS)
            a = a_ref[pl.ds(r0, SCAN_ROWS), cs]
            b = b_ref[pl.ds(r0, SCAN_ROWS), cs]
            a_top, b_top = local_scan(a[:half], b[:half])
            a_bot, b_bot = local_scan(a[half:], b[half:])
            h_top = a_top * carry + b_top
            h_bot = a_bot * h_top[half - 1:half, :] + b_bot
            h = jnp.concatenate([h_top, h_bot], axis=0)
            o_ref[pl.ds(r0, SCAN_ROWS), cs] = (
                h * y_ref[pl.ds(r0, SCAN_ROWS), cs].astype(F32)).astype(o_ref.dtype)
            return h_bot[half - 1:half, :]

        hc_ref[0:1, cs] = lax.fori_loop(0, tr // SCAN_ROWS, body, hc_ref[0:1, cs])


def rglru(xc, y, w_a, b_a, w_x, b_x, a_param):
    m, d = xc.shape
    tr = TR_LRU
    row = pl.BlockSpec((tr, d), lambda i: (i, 0))
    vec = pl.BlockSpec((1, d), lambda i: (0, 0))
    wspec = pl.BlockSpec((LRU_BLOCKS, LRU_BDIM, LRU_BDIM), lambda i: (0, 0, 0))
    nbytes = 4 * tr * d * 4 + 2 * tr * d * 2 + 2 * tr * d * 4 + 4 * LRU_BLOCKS * LRU_BDIM * LRU_BDIM * 2
    return pl.pallas_call(
        functools.partial(_lru_kernel, tr=tr),
        out_shape=jax.ShapeDtypeStruct((m, d), BF16),
        grid=(m // tr,),
        in_specs=[row, row, wspec, wspec, vec, vec, vec],
        out_specs=row,
        scratch_shapes=[pltpu.VMEM((tr, d), F32), pltpu.VMEM((tr, d), F32),
                        pltpu.VMEM((V7X_SUBLANES, d), F32)],
        compiler_params=_params(("arbitrary",), nbytes),
        name="rglru",
    )(xc, y, (0.5 * w_a).astype(BF16), (0.5 * w_x).astype(BF16), 0.5 * b_a.reshape(1, d),
      0.5 * b_x.reshape(1, d), a_param.reshape(1, d))


def _hgrn_kernel(q_ref, f_ref, i_ref, g_ref, lbl_ref, gnw_ref, o_ref, st_ref, *, tr, layer_j):
    step = pl.program_id(1)

    @pl.when(step == 0)
    def _():
        st_ref[...] = jnp.zeros_like(st_ref)

    logits = lbl_ref[...]
    e = jnp.exp(logits - jnp.max(logits, axis=0, keepdims=True))
    lb = jnp.sum(e[0:layer_j + 1], axis=0, keepdims=True) / jnp.sum(e, axis=0, keepdims=True)

    width = HGRN_HEADS_PER_STEP * A_KDIM
    valid = (step * tr + _iota((tr, 1), 0)) >= PAD
    q = jax.nn.silu(q_ref[...].astype(F32))
    forget = lb + (1.0 - lb) * jax.nn.sigmoid(f_ref[...].astype(F32))
    k = jnp.where(valid, 1.0 - forget, 0.0)
    g = jnp.where(valid, jnp.log(forget), 0.0)
    v = i_ref[...]

    pos = _iota((tr, width), 0) & (HGRN_CHUNK - 1)
    b = g
    shift = 1
    while shift < HGRN_CHUNK:
        b = b + jnp.where(pos >= shift, pltpu.roll(b, shift, 0), 0.0)
        shift *= 2

    n_sub = HGRN_CHUNK // HGRN_SUB
    cpos = _iota((HGRN_CHUNK, A_KDIM), 0)
    causal = _iota((HGRN_CHUNK, HGRN_CHUNK), 0) >= _iota((HGRN_CHUNK, HGRN_CHUNK), 1)
    gnw = gnw_ref[...]
    heads = range(HGRN_HEADS_PER_STEP)
    chunks = range(tr // HGRN_CHUNK)
    pairs = [(hh, c) for hh in heads for c in chunks]

    def part(x, p):
        hh, c = p
        return x[c * HGRN_CHUNK:(c + 1) * HGRN_CHUNK, hh * A_KDIM:(hh + 1) * A_KDIM]

    bc = {p: part(b, p) for p in pairs}
    qc = {p: part(q, p) for p in pairs}
    kc = {p: part(k, p) for p in pairs}
    vcb = {p: part(v, p).astype(BF16) for p in pairs}
    b_last = {p: bc[p][HGRN_CHUNK - 1:HGRN_CHUNK] for p in pairs}
    u_t = {p: _dot_tn(vcb[p], (kc[p] * jnp.exp(b_last[p] - bc[p])).astype(BF16)) for p in pairs}
    att = {}
    for p in pairs:
        rows = []
        for i in range(n_sub):
            ss = slice(i * HGRN_SUB, (i + 1) * HGRN_SUB)
            ref = jnp.zeros((1, A_KDIM), F32) if i == 0 else bc[p][i * HGRN_SUB - 1:i * HGRN_SUB]
            q_sc = (qc[p][ss] * jnp.exp(bc[p][ss] - ref)).astype(BF16)
            expo = jnp.where(cpos < (i + 1) * HGRN_SUB, ref - bc[p], 0.0)
            k_sc = (kc[p] * jnp.exp(expo)).astype(BF16)
            rows.append(_dot_nt(q_sc, k_sc))
        att[p] = jnp.where(causal, jnp.concatenate(rows, axis=0), 0.0).astype(BF16)
    o_intra = {p: _dot(att[p], vcb[p]) for p in pairs}
    states = {}
    for hh in heads:
        st = st_ref[hh]
        for c in chunks:
            states[(hh, c)] = st.astype(BF16)
            st = st * jnp.exp(b_last[(hh, c)]) + u_t[(hh, c)]
        st_ref[hh] = st
    o_inter = {p: _dot_nt((qc[p] * jnp.exp(bc[p])).astype(BF16), states[p]) for p in pairs}
    for hh, c in pairs:
        rs = slice(c * HGRN_CHUNK, (c + 1) * HGRN_CHUNK)
        cs = slice(hh * A_VDIM, (hh + 1) * A_VDIM)
        gate = jax.nn.silu(g_ref[rs, cs].astype(F32))
        o_ref[rs, cs] = (_rms(o_inter[(hh, c)] + o_intra[(hh, c)], gnw) * gate).astype(o_ref.dtype)


def hgrn2(hproj, lb_logits, gn_w, layer_j):
    m = hproj.shape[0]
    tr = TR_HGRN
    nrow = lb_logits.shape[0]
    hps = HGRN_HEADS_PER_STEP
    groups = A_HEADS // hps

    def col(off):
        return pl.BlockSpec((tr, hps * A_KDIM), lambda h, t, off=off: (t, h + off * groups))

    nbytes = 2 * 4 * tr * hps * A_KDIM * 2 + 2 * tr * hps * A_VDIM * 2 + 16 * tr * hps * A_KDIM * 4
    return pl.pallas_call(
        functools.partial(_hgrn_kernel, tr=tr, layer_j=layer_j),
        out_shape=jax.ShapeDtypeStruct((m, A_WIDTH), BF16),
        grid=(groups, m // tr),
        in_specs=[col(0), col(1), col(2), col(3),
                  pl.BlockSpec((nrow, hps * A_KDIM), lambda h, t: (0, h)),
                  pl.BlockSpec((1, A_VDIM), lambda h, t: (0, 0))],
        out_specs=pl.BlockSpec((tr, hps * A_VDIM), lambda h, t: (t, h)),
        scratch_shapes=[pltpu.VMEM((hps, A_VDIM, A_KDIM), F32)],
        compiler_params=_params(("arbitrary", "arbitrary"), nbytes),
        name="hgrn2",
    )(hproj, hproj, hproj, hproj, lb_logits, gn_w.reshape(1, A_VDIM))


PAIRS = B_GROUP // 2
QROWS = PAIRS * ATTN_BLOCK


def _swa_kernel(sink_ref, q_ref, kc_ref, kp_ref, km_ref, vc_ref, vp_ref, vm_ref, o_ref):
    n = pl.program_id(0)
    scale = B_HDIM ** -0.5
    lane_lo = _iota((1, V7X_LANES), 1) < B_HDIM
    tq = _iota((QROWS, ATTN_BLOCK), 0) & (ATTN_BLOCK - 1)
    sk = _iota((QROWS, ATTN_BLOCK), 1)
    masks = (sk > tq + jnp.where(n >= 2, 0, 2 * ATTN_BLOCK),
             sk <= tq + jnp.where(n >= 1, 0, -2 * ATTN_BLOCK),
             jnp.logical_and(sk >= PAD, sk <= tq + jnp.where(n >= 1, ATTN_BLOCK, 0)))

    def split(x, natural_lo, fill):
        rolled = pltpu.roll(x, B_HDIM, 1)
        lo_src, hi_src = (x, rolled) if natural_lo else (rolled, x)
        return (jnp.where(lane_lo, lo_src, fill).astype(BF16), jnp.where(lane_lo, fill, hi_src).astype(BF16))

    chains = [(h, par) for h in range(B_KVHEADS) for par in range(2)]
    keys, vals, qs = {}, {}, {}
    for h in range(B_KVHEADS):
        tile = slice((h // 2) * V7X_LANES, (h // 2 + 1) * V7X_LANES)
        nat = h % 2 == 0
        keys[h] = [split(r[:, tile].astype(F32), nat, 0.0) for r in (kp_ref, kc_ref, km_ref)]
        vals[h] = [split(r[:, tile].astype(F32), nat, 1.0) for r in (vp_ref, vc_ref, vm_ref)]
        qs[h] = (jnp.concatenate(
            [q_ref[:, (h * PAIRS + p) * V7X_LANES:(h * PAIRS + p + 1) * V7X_LANES] for p in range(PAIRS)],
            axis=0).astype(F32) * scale).astype(BF16)
    logits = {c: [jnp.where(m, _dot_nt(qs[c[0]], kk[c[1]]), NEG_INF) for m, kk in zip(masks, keys[c[0]])]
              for c in chains}
    sinks = {(h, par): jnp.concatenate(
        [jnp.full((ATTN_BLOCK, V7X_LANES), sink_ref[h * B_GROUP + 2 * p + par], F32) for p in range(PAIRS)],
        axis=0) for h, par in chains}
    mx = {c: jnp.maximum(jnp.broadcast_to(
        jnp.max(jnp.maximum(jnp.maximum(logits[c][0], logits[c][1]), logits[c][2]), axis=-1, keepdims=True),
        (QROWS, V7X_LANES)), sinks[c]) for c in chains}
    pv = {}
    for c in chains:
        for lg, vv in zip(logits[c], vals[c[0]]):
            part = _dot(jnp.exp(lg - mx[c]).astype(BF16), vv[c[1]])
            pv[c] = part if c not in pv else pv[c] + part
    out = {}
    for c in chains:
        den = pltpu.roll(pv[c], B_HDIM, 1) + jnp.exp(sinks[c] - mx[c])
        own_half = lane_lo if c[1] == 0 else jnp.logical_not(lane_lo)
        part = jnp.where(own_half, pv[c] / den, 0.0)
        out[c[0]] = part if c[0] not in out else out[c[0]] + part
    for h in range(B_KVHEADS):
        for p in range(PAIRS):
            o_ref[:, (h * PAIRS + p) * V7X_LANES:(h * PAIRS + p + 1) * V7X_LANES] = (
                out[h][p * ATTN_BLOCK:(p + 1) * ATTN_BLOCK].astype(o_ref.dtype))


def swa(hproj, sinks):
    m = hproj.shape[0]
    blk = ATTN_BLOCK
    q_col = (2 * A_FDIM + 2 * A_WIDTH) // B_WIDTH
    k_col = (2 * A_FDIM + 2 * A_WIDTH + B_WIDTH) // B_KVWIDTH
    v_col = k_col + 1
    qspec = pl.BlockSpec((blk, B_WIDTH), lambda n: (n, q_col))

    def kv(col, which):
        if which == "cur":
            return pl.BlockSpec((blk, B_KVWIDTH), lambda n: (n, col))
        if which == "prev":
            return pl.BlockSpec((blk, B_KVWIDTH), lambda n: (jnp.maximum(n - 1, 0), col))
        return pl.BlockSpec((blk, B_KVWIDTH), lambda n: (0, col))

    nbytes = 2 * blk * B_WIDTH * 4 + 12 * blk * B_KVWIDTH * 4 + 2 * blk * B_WIDTH * 2 + 40 * QROWS * 128 * 4
    return pl.pallas_call(
        _swa_kernel,
        out_shape=jax.ShapeDtypeStruct((m, B_WIDTH), BF16),
        grid=(m // blk,),
        in_specs=[pl.BlockSpec(memory_space=pltpu.SMEM), qspec,
                  kv(k_col, "cur"), kv(k_col, "prev"), kv(k_col, "meta"),
                  kv(v_col, "cur"), kv(v_col, "prev"), kv(v_col, "meta")],
        out_specs=pl.BlockSpec((blk, B_WIDTH), lambda n: (n, 0)),
        compiler_params=_params(("arbitrary",), nbytes),
        name="swa",
    )(sinks, hproj, hproj, hproj, hproj, hproj, hproj, hproj)


def kernel(x, meta_tokens, norm_w, w_in_even, lb_logits, hgrn_gn_w, attn_sinks, w_out_even,
           w_in_odd, lru_conv_w, lru_conv_b, lru_wa, lru_ba, lru_wx, lru_bx, lru_a_param, w_out_odd,
           ffn_w_gu, ffn_conv_w, ffn_conv_b, ffn_w_down):
    assert x.shape == (1, SEQ, D_MODEL) and norm_w.shape[0] == DEPTH and w_in_even.shape[-1] == EVEN_IN
    x2d = x[0]
    xn = rms_cast_stream(x2d, meta_tokens, norm_w[0, 0])
    h, out = None, None
    for layer in range(DEPTH):
        j = layer // 2
        if layer % 2 == 0:
            hproj = matmul([xn], w_in_even, layer=j, tm=TM_IN_EVEN, tn=TN_IN_EVEN, out_dtype=BF16,
                           name="in_even")
            o_a = hgrn2(hproj, lb_logits, hgrn_gn_w[j], j)
            o_b = swa(hproj, attn_sinks[j])
            mix = matmul([o_a, o_b], w_out_even, layer=j, tm=TM_OUT, tn=TN_OUT, out_dtype=BF16,
                         name="out_even")
        else:
            y_br, x_br = in_odd(xn, w_in_odd, lru_conv_w, lru_conv_b, j)
            rec = rglru(x_br, y_br, lru_wa[j], lru_ba[j], lru_wx[j], lru_bx[j], lru_a_param[j])
            mix = matmul([rec], w_out_odd, layer=j, tm=TM_OUT, tn=TN_OUT, out_dtype=BF16, name="out_odd")
        if h is None:
            h, xn = resid_norm_stream(x2d, meta_tokens, mix, norm_w[layer, 1], norm_w[layer, 2])
        else:
            h, xn = resid_norm(h, mix, norm_w[layer, 1], norm_w[layer, 2])
        act, w_down_bf16 = ffn_up(xn, ffn_w_gu, ffn_conv_w, ffn_conv_b, ffn_w_down, layer)
        ff = matmul([act], w_down_bf16, tm=TM_DOWN, tn=TN_DOWN, out_dtype=BF16, name="ffn_down")
        if layer + 1 < DEPTH:
            h, xn = resid_norm(h, ff, norm_w[layer, 3], norm_w[layer + 1, 0])
        else:
            out = resid_final(h, ff, norm_w[layer, 3])
    return out[None]
```

```python
import functools

import jax
import jax.numpy as jnp
from jax import lax
from jax.experimental import pallas as pl
from jax.experimental.pallas import tpu as pltpu

F32 = jnp.float32
BF16 = jnp.bfloat16

D_MODEL = 4096
SEQ = 8192
DEPTH = 2
N_META = 16
A_HEADS = 16
A_KDIM = 128
A_VDIM = D_MODEL // 2 // A_HEADS
A_FDIM = A_HEADS * A_KDIM
A_WIDTH = A_HEADS * A_VDIM
HGRN_CHUNK = 64
HGRN_SUB = 16
B_HDIM = 64
B_QHEADS = D_MODEL // 2 // B_HDIM
B_KVHEADS = B_QHEADS // 8
B_GROUP = B_QHEADS // B_KVHEADS
B_WIDTH = B_QHEADS * B_HDIM
B_KVWIDTH = B_KVHEADS * B_HDIM
WINDOW = 128
ATTN_BLOCK = 128
EVEN_IN = 2 * A_FDIM + 2 * A_WIDTH + B_WIDTH + 2 * B_KVWIDTH
LRU_WIDTH = D_MODEL
LRU_BLOCKS = 16
LRU_BDIM = LRU_WIDTH // LRU_BLOCKS
LRU_CONV = 4
LRU_C = 8.0
D_FF = 256 * ((8 * D_MODEL // 3 + 255) // 256)
FFN_CONV = 3
NORM_EPS = 1e-6
NEG_INF = -1e30

PAD = ATTN_BLOCK - N_META
ROW0 = PAD + N_META
LP = ROW0 + SEQ
assert PAD % HGRN_CHUNK == HGRN_CHUNK - N_META and ROW0 == ATTN_BLOCK
assert WINDOW == ATTN_BLOCK

V7X_LANES = 128
V7X_SUBLANES = 8
V7X_VMEM_LIMIT_CAP = 60 * 1024 * 1024
CARRY_ROWS = V7X_SUBLANES

TM_MM = 1040
TM_IN_EVEN = 1040
TN_IN_EVEN = 512
TM_OUT = 520
TN_OUT = 1024
TM_DOWN = 520
TN_DOWN = 512
TF_FFN = 256
TN_ODD = 256
EPI_ROWS = 80
K_CHUNK = 256
MAX_WEIGHT_SLABS = 8
TR_NORM = 320
STREAM_GROUPS = 5
FINAL_GROUPS = 4
TR_LRU = 208
CW_LRU = 1024
TR_HGRN = 640
HGRN_HEADS_PER_STEP = 4
SCAN_ROWS = 16


def _vmem_limit(nbytes):
    return int(min(V7X_VMEM_LIMIT_CAP, nbytes * 1.15 + (4 << 20)))


def _params(sem, nbytes):
    return pltpu.CompilerParams(dimension_semantics=sem, vmem_limit_bytes=_vmem_limit(nbytes))


def _rms(x, w):
    x = x.astype(F32)
    return x * lax.rsqrt(jnp.mean(x * x, axis=-1, keepdims=True) + NORM_EPS) * w


def _iota(shape, dim):
    return lax.broadcasted_iota(jnp.int32, shape, dim)


def _dot(a, b):
    return jnp.dot(a, b, preferred_element_type=F32)


def _dot_nt(a, b):
    return lax.dot_general(a, b, (((1,), (1,)), ((), ())), preferred_element_type=F32)


def _dot_tn(a, b):
    return lax.dot_general(a, b, (((0,), (0,)), ((), ())), preferred_element_type=F32)


def _stream_rows(n, x_ref, meta_ref):
    first = jnp.concatenate([jnp.zeros((PAD, D_MODEL), F32), meta_ref[...].astype(F32)], axis=0)
    return jnp.where(n == 0, first, x_ref[...].astype(F32))


def _stream_specs():
    blk = ATTN_BLOCK
    return [pl.BlockSpec((blk, D_MODEL), lambda n: (jnp.maximum(n - 1, 0), 0)),
            pl.BlockSpec((N_META, D_MODEL), lambda n: (0, 0))]


def _rms_cast_stream_kernel(*refs, groups):
    x_refs, meta_ref, w_ref, o_ref = refs[:groups], refs[groups], refs[groups + 1], refs[groups + 2]
    blk = ATTN_BLOCK
    for q, x_ref in enumerate(x_refs):
        rows = _stream_rows(pl.program_id(0), x_ref, meta_ref) if q == 0 else x_ref[...].astype(F32)
        o_ref[q * blk:(q + 1) * blk, :] = _rms(rows, w_ref[...]).astype(o_ref.dtype)


def rms_cast_stream(x2d, meta, w):
    blk, d, groups = ATTN_BLOCK, D_MODEL, STREAM_GROUPS
    rows = blk * groups
    x_specs = [pl.BlockSpec((blk, d), lambda n, q=q: (jnp.maximum(groups * n + q - 1, 0), 0))
               for q in range(groups)]
    return pl.pallas_call(
        functools.partial(_rms_cast_stream_kernel, groups=groups),
        out_shape=jax.ShapeDtypeStruct((LP, d), BF16),
        grid=(LP // rows,),
        in_specs=x_specs + [pl.BlockSpec((N_META, d), lambda n: (0, 0)), pl.BlockSpec((1, d), lambda n: (0, 0))],
        out_specs=pl.BlockSpec((rows, d), lambda n: (n, 0)),
        compiler_params=_params(("arbitrary",), 2 * rows * d * 6),
        name="rms_cast_stream",
    )(*([x2d] * groups), meta, w.reshape(1, d))


def _resid_norm_stream_kernel(x_ref, meta_ref, y_ref, wp_ref, wn_ref, ho_ref, xo_ref):
    h = _stream_rows(pl.program_id(0), x_ref, meta_ref) + _rms(y_ref[...], wp_ref[...])
    ho_ref[...] = h
    xo_ref[...] = _rms(h, wn_ref[...]).astype(xo_ref.dtype)


def resid_norm_stream(x2d, meta, y, w_post, w_next):
    blk, d = ATTN_BLOCK, D_MODEL
    row = pl.BlockSpec((blk, d), lambda n: (n, 0))
    vec = pl.BlockSpec((1, d), lambda n: (0, 0))
    return pl.pallas_call(
        _resid_norm_stream_kernel,
        out_shape=(jax.ShapeDtypeStruct((LP, d), F32), jax.ShapeDtypeStruct((LP, d), BF16)),
        grid=(LP // blk,),
        in_specs=_stream_specs() + [row, vec, vec],
        out_specs=(row, row),
        compiler_params=_params(("arbitrary",), 2 * blk * d * 14 + 4 * blk * d * 4),
        name="resid_norm_stream",
    )(x2d, meta, y, w_post.reshape(1, d), w_next.reshape(1, d))


def _resid_norm_kernel(h_ref, y_ref, wp_ref, wn_ref, ho_ref, xo_ref):
    h = h_ref[...] + _rms(y_ref[...], wp_ref[...])
    ho_ref[...] = h
    xo_ref[...] = _rms(h, wn_ref[...]).astype(xo_ref.dtype)


def resid_norm(h, y, w_post, w_next):
    n, d = h.shape
    tr = TR_NORM
    row = pl.BlockSpec((tr, d), lambda i: (i, 0))
    vec = pl.BlockSpec((1, d), lambda i: (0, 0))
    return pl.pallas_call(
        _resid_norm_kernel,
        out_shape=(jax.ShapeDtypeStruct((n, d), F32), jax.ShapeDtypeStruct((n, d), BF16)),
        grid=(n // tr,),
        in_specs=[row, row, vec, vec],
        out_specs=(row, row),
        compiler_params=_params(("arbitrary",), 2 * tr * d * 14),
        name="resid_norm",
    )(h, y, w_post.reshape(1, d), w_next.reshape(1, d))


def _resid_final_kernel(*refs, groups):
    h_refs, y_refs, wp_ref, o_ref = refs[:groups], refs[groups:2 * groups], refs[2 * groups], refs[2 * groups + 1]
    blk = ATTN_BLOCK
    for q in range(groups):
        o_ref[q * blk:(q + 1) * blk, :] = h_refs[q][...] + _rms(y_refs[q][...], wp_ref[...])


def resid_final(h, y, w_post):
    n, d = h.shape
    blk, groups = ATTN_BLOCK, FINAL_GROUPS
    rows = blk * groups
    off = ROW0 // blk
    src = [pl.BlockSpec((blk, d), lambda i, q=q: (groups * i + q + off, 0)) for q in range(groups)]
    return pl.pallas_call(
        functools.partial(_resid_final_kernel, groups=groups),
        out_shape=jax.ShapeDtypeStruct((n - ROW0, d), F32),
        grid=((n - ROW0) // rows,),
        in_specs=src + src + [pl.BlockSpec((1, d), lambda i: (0, 0))],
        out_specs=pl.BlockSpec((rows, d), lambda i: (i, 0)),
        compiler_params=_params(("arbitrary",), 2 * rows * d * 10),
        name="resid_final",
    )(*([h] * groups), *([y] * groups), w_post.reshape(1, d))


def _slab_specs(w, layer, k, tn, n_slabs, col_of):
    rows = k // n_slabs
    assert rows * n_slabs == k

    def spec(s):
        if w.ndim == 3:
            return pl.BlockSpec((None, rows, tn), lambda *g: (layer, s, col_of(*g, s)))
        return pl.BlockSpec((rows, tn), lambda *g: (s, col_of(*g, s)))

    return [spec(s) for s in range(n_slabs)]


def _cast_slabs(slab_refs, dst_ref):
    rows = slab_refs[0].shape[0]
    for s, ref in enumerate(slab_refs):
        dst_ref[s * rows:(s + 1) * rows, :] = ref[...].astype(BF16)


def _mm_kernel(*refs, n_lhs, n_slabs, cast):
    x_refs, w_refs, o_ref = refs[:n_lhs], refs[n_lhs:n_lhs + n_slabs], refs[n_lhs + n_slabs]
    if cast:
        w_ref = refs[n_lhs + n_slabs + 1]

        @pl.when(pl.program_id(1) == 0)
        def _():
            _cast_slabs(w_refs, w_ref)
    else:
        (w_ref,) = w_refs
    acc, k0 = None, 0
    for x_ref in x_refs:
        kk = x_ref.shape[1]
        part = _dot(x_ref[...], w_ref[k0:k0 + kk, :])
        acc = part if acc is None else acc + part
        k0 += kk
    o_ref[...] = acc.astype(o_ref.dtype)


def matmul(xs, w, *, layer=0, tm, tn, out_dtype=F32, name="matmul"):
    m = xs[0].shape[0]
    k, n = w.shape[-2:]
    assert sum(x.shape[1] for x in xs) == k
    cast = w.dtype != BF16
    ni, nj = m // tm, n // tn
    wbytes = k * tn * (4 if cast else 2)
    nbytes = 2 * tm * k * 2 + 2 * wbytes + (k * tn * 2 if cast else 0) + 2 * tm * tn * 4
    if cast and ni <= MAX_WEIGHT_SLABS:
        n_slabs = ni
        w_specs = _slab_specs(w, layer, k, tn, n_slabs,
                              lambda j, i, s: jnp.minimum(j + (i + ni - 1 - s) // ni, nj - 1))
    else:
        n_slabs = 1
        w_specs = _slab_specs(w, layer, k, tn, 1, lambda j, i, s: j)
    return pl.pallas_call(
        functools.partial(_mm_kernel, n_lhs=len(xs), n_slabs=n_slabs, cast=cast),
        out_shape=jax.ShapeDtypeStruct((m, n), out_dtype),
        grid=(nj, ni),
        in_specs=[pl.BlockSpec((tm, x.shape[1]), lambda j, i: (i, 0)) for x in xs] + w_specs,
        out_specs=pl.BlockSpec((tm, tn), lambda j, i: (i, j)),
        scratch_shapes=[pltpu.VMEM((k, tn), BF16)] if cast else [],
        compiler_params=_params(("arbitrary", "arbitrary"), nbytes),
        name=name,
    )(*xs, *([w] * n_slabs))


def _tile_maps(ni, nj):
    nt = ni * nj

    def cur_row(t):
        return lax.rem(jnp.minimum(t, nt - 1), ni)

    def prev_row(t):
        return lax.rem(jnp.maximum(t - 1, 0), ni)

    def prev_col(t):
        return lax.div(jnp.maximum(t - 1, 0), ni)

    return nt, cur_row, prev_row, prev_col


def _interleaved_projections(x_ref, w_refs, dst_stores, epilogue_chunk, tm):
    nk = x_ref.shape[1] // K_CHUNK
    chunk_rows = list(range(0, tm, EPI_ROWS))
    slots, slot, done = len(w_refs) * nk, 0, 0
    for w_ref, store in zip(w_refs, dst_stores):
        acc = None
        for kk in range(nk):
            ks = slice(kk * K_CHUNK, (kk + 1) * K_CHUNK)
            part = _dot(x_ref[:, ks], w_ref[ks, :])
            acc = part if acc is None else acc + part
            slot += 1
            while done < len(chunk_rows) * slot // slots:
                epilogue_chunk(chunk_rows[done])
                done += 1
        store(acc)


def _causal_conv(buf_ref, cw_ref, cb_ref, width, r0, rows):
    win = buf_ref[r0:r0 + CARRY_ROWS + rows, :]
    conv = cb_ref[...]
    for tap in range(width):
        back = width - 1 - tap
        shifted = pltpu.roll(win, back, 0) if back else win
        conv = conv + shifted[CARRY_ROWS:, :] * cw_ref[tap:tap + 1, :]
    return conv


def _ffn_up_kernel(*refs, tm, ni, nt):
    x_ref, wg_refs, wu_refs = refs[0], refs[1:1 + ni], refs[1 + ni:1 + 2 * ni]
    (cw_ref, cb_ref, wd_ref, o_ref, wdb_ref,
     wgb_ref, wub_ref, g0_ref, g1_ref, u0_ref, u1_ref) = refs[1 + 2 * ni:]
    t = pl.program_id(0)
    i = lax.rem(t, ni)

    @pl.when(t == 0)
    def _():
        g1_ref[...] = jnp.zeros_like(g1_ref)
        u1_ref[...] = jnp.zeros_like(u1_ref)

    @pl.when(jnp.logical_and(i == 0, t < nt))
    def _():
        _cast_slabs(wg_refs, wgb_ref)
        _cast_slabs(wu_refs, wub_ref)

    wdb_ref[...] = wd_ref[...].astype(BF16)

    def step(g_cur, u_cur, g_prev, u_prev):
        g_cur[0:CARRY_ROWS, :] = jnp.where(i == 0, 0.0, g_prev[tm:tm + CARRY_ROWS, :])

        def epilogue_chunk(r0):
            conv = _causal_conv(g_prev, cw_ref, cb_ref, FFN_CONV, r0, EPI_ROWS)
            o_ref[r0:r0 + EPI_ROWS, :] = (jax.nn.gelu(conv, approximate=True)
                                          * u_prev[r0:r0 + EPI_ROWS, :]).astype(o_ref.dtype)

        def store_gate(acc):
            g_cur[CARRY_ROWS:CARRY_ROWS + tm, :] = acc

        def store_up(acc):
            u_cur[...] = acc

        _interleaved_projections(x_ref, (wgb_ref, wub_ref), (store_gate, store_up), epilogue_chunk, tm)

    @pl.when(lax.rem(t, 2) == 0)
    def _():
        step(g0_ref, u0_ref, g1_ref, u1_ref)

    @pl.when(lax.rem(t, 2) == 1)
    def _():
        step(g1_ref, u1_ref, g0_ref, u0_ref)


def ffn_up(xn, w_gu, conv_w, conv_b, w_down, layer):
    m, k = xn.shape
    tm, tf = TM_MM, TF_FFN
    ni, nj = m // tm, D_FF // tf
    nt, cur_row, prev_row, prev_col = _tile_maps(ni, nj)

    def slab_col(t, s):
        return jnp.minimum(lax.div(jnp.minimum(t, nt - 1) + ni - 1 - s, ni), nj - 1)

    wd_rows = D_FF // nt
    assert wd_rows * nt == D_FF and wd_rows % (2 * V7X_SUBLANES) == 0
    nbytes = (2 * tm * k * 2 + 4 * k * tf * 4 + 2 * k * tf * 2 + 2 * tm * tf * 2
              + 4 * (tm + CARRY_ROWS) * tf * 4 + 2 * wd_rows * D_MODEL * 6)
    return pl.pallas_call(
        functools.partial(_ffn_up_kernel, tm=tm, ni=ni, nt=nt),
        out_shape=(jax.ShapeDtypeStruct((m, D_FF), BF16), jax.ShapeDtypeStruct((D_FF, D_MODEL), BF16)),
        grid=(nt + 1,),
        in_specs=[pl.BlockSpec((tm, k), lambda t: (cur_row(t), 0))]
        + _slab_specs(w_gu, layer, k, tf, ni, slab_col)
        + _slab_specs(w_gu, layer, k, tf, ni, lambda t, s: slab_col(t, s) + nj)
        + [pl.BlockSpec((None, FFN_CONV, tf), lambda t: (layer, 0, prev_col(t))),
                  pl.BlockSpec((None, 1, tf), lambda t: (layer, 0, prev_col(t))),
                  pl.BlockSpec((None, wd_rows, D_MODEL), lambda t: (layer, jnp.minimum(t, nt - 1), 0))],
        out_specs=(pl.BlockSpec((tm, tf), lambda t: (prev_row(t), prev_col(t))),
                   pl.BlockSpec((wd_rows, D_MODEL), lambda t: (jnp.minimum(t, nt - 1), 0))),
        scratch_shapes=[pltpu.VMEM((k, tf), BF16), pltpu.VMEM((k, tf), BF16),
                        pltpu.VMEM((tm + CARRY_ROWS, tf), F32), pltpu.VMEM((tm + CARRY_ROWS, tf), F32),
                        pltpu.VMEM((tm, tf), F32), pltpu.VMEM((tm, tf), F32)],
        compiler_params=_params(("arbitrary",), nbytes),
        name="ffn_up",
    )(xn, *([w_gu] * (2 * ni)), conv_w, conv_b.reshape(conv_b.shape[0], 1, D_FF), w_down)


def _in_odd_kernel(*refs, tm, ni, nt):
    x_ref, wy_refs, wx_refs = refs[0], refs[1:1 + ni], refs[1 + ni:1 + 2 * ni]
    (cw_ref, cb_ref, y_ref, xc_ref,
     wyb_ref, wxb_ref, b0_ref, b1_ref, r0_ref, r1_ref) = refs[1 + 2 * ni:]
    t = pl.program_id(0)
    i = lax.rem(t, ni)

    @pl.when(t == 0)
    def _():
        b1_ref[...] = jnp.zeros_like(b1_ref)
        r1_ref[...] = jnp.zeros_like(r1_ref)

    @pl.when(jnp.logical_and(i == 0, t < nt))
    def _():
        _cast_slabs(wy_refs, wyb_ref)
        _cast_slabs(wx_refs, wxb_ref)

    def step(b_cur, r_cur, b_prev, r_prev):
        b_cur[0:CARRY_ROWS, :] = jnp.where(i == 0, 0.0, b_prev[tm:tm + CARRY_ROWS, :])

        def epilogue_chunk(r0):
            y_ref[r0:r0 + EPI_ROWS, :] = jax.nn.gelu(r_prev[r0:r0 + EPI_ROWS, :],
                                                     approximate=True).astype(y_ref.dtype)
            xc_ref[r0:r0 + EPI_ROWS, :] = _causal_conv(b_prev, cw_ref, cb_ref, LRU_CONV, r0, EPI_ROWS)

        def store_y(acc):
            r_cur[...] = acc

        def store_x(acc):
            b_cur[CARRY_ROWS:CARRY_ROWS + tm, :] = acc

        _interleaved_projections(x_ref, (wyb_ref, wxb_ref), (store_y, store_x), epilogue_chunk, tm)

    @pl.when(lax.rem(t, 2) == 0)
    def _():
        step(b0_ref, r0_ref, b1_ref, r1_ref)

    @pl.when(lax.rem(t, 2) == 1)
    def _():
        step(b1_ref, r1_ref, b0_ref, r0_ref)


def in_odd(xn, w_in, conv_w, conv_b, layer):
    m, k = xn.shape
    tm, tn = TM_MM, TN_ODD
    ni, nj = m // tm, LRU_WIDTH // tn
    nt, cur_row, prev_row, prev_col = _tile_maps(ni, nj)
    nbytes = (2 * tm * k * 2 + 4 * k * tn * 4 + 2 * k * tn * 2 + 4 * tm * tn * 4
              + 4 * (tm + CARRY_ROWS) * tn * 4)

    def slab_col(t, s):
        return jnp.minimum(lax.div(jnp.minimum(t, nt - 1) + ni - 1 - s, ni), nj - 1)

    out = pl.BlockSpec((tm, tn), lambda t: (prev_row(t), prev_col(t)))
    return pl.pallas_call(
        functools.partial(_in_odd_kernel, tm=tm, ni=ni, nt=nt),
        out_shape=(jax.ShapeDtypeStruct((m, LRU_WIDTH), BF16), jax.ShapeDtypeStruct((m, LRU_WIDTH), F32)),
        grid=(nt + 1,),
        in_specs=[pl.BlockSpec((tm, k), lambda t: (cur_row(t), 0))]
        + _slab_specs(w_in, layer, k, tn, ni, slab_col)
        + _slab_specs(w_in, layer, k, tn, ni, lambda t, s: slab_col(t, s) + nj)
        + [pl.BlockSpec((None, LRU_CONV, tn), lambda t: (layer, 0, prev_col(t))),
                  pl.BlockSpec((None, 1, tn), lambda t: (layer, 0, prev_col(t)))],
        out_specs=(out, out),
        scratch_shapes=[pltpu.VMEM((k, tn), BF16), pltpu.VMEM((k, tn), BF16),
                        pltpu.VMEM((tm + CARRY_ROWS, tn), F32), pltpu.VMEM((tm + CARRY_ROWS, tn), F32),
                        pltpu.VMEM((tm, tn), F32), pltpu.VMEM((tm, tn), F32)],
        compiler_params=_params(("arbitrary",), nbytes),
        name="in_odd",
    )(xn, *([w_in] * (2 * ni)), conv_w, conv_b.reshape(conv_b.shape[0], 1, LRU_WIDTH))


def _lru_kernel(xc_ref, y_ref, wa_ref, wx_ref, ba_ref, bx_ref, ap_ref, o_ref, a_ref, b_ref, hc_ref, *, tr):
    step = pl.program_id(0)

    @pl.when(step == 0)
    def _():
        hc_ref[...] = jnp.zeros_like(hc_ref)

    for blk in range(LRU_BLOCKS):
        sl = slice(blk * LRU_BDIM, (blk + 1) * LRU_BDIM)
        x = xc_ref[:, sl]
        xb = x.astype(BF16)
        t_r = jnp.tanh(_dot(xb, wa_ref[blk]) + ba_ref[:, sl])
        t_i = jnp.tanh(_dot(xb, wx_ref[blk]) + bx_ref[:, sl])
        c = (LRU_C / 4.0) * jax.nn.log_sigmoid(ap_ref[:, sl])
        th = jnp.tanh(t_r * c + c)
        em = (th + th) / (1.0 - th)
        a_ref[:, sl] = 1.0 + em
        xh = 0.5 * x
        inp = jnp.sqrt(em * (-2.0 - em)) * (t_i * xh + xh)
        b_ref[:, sl] = inp

    @pl.when(step == 0)
    def _():
        b_ref[0:PAD, :] = jnp.zeros((PAD, b_ref.shape[1]), F32)

    half = SCAN_ROWS // 2
    ridx = _iota((half, CW_LRU), 0)

    def local_scan(a, b):
        shift = 1
        while shift < half:
            ok = ridx >= shift
            b = jnp.where(ok, a * pltpu.roll(b, shift, 0) + b, b)
            a = jnp.where(ok, a * pltpu.roll(a, shift, 0), a)
            shift *= 2
        return a, b

    for c in range(LRU_WIDTH // CW_LRU):
        cs = slice(c * CW_LRU, (c + 1) * CW_LRU)

        def body(g, carry, cs=cs):
            r0 = pl.multiple_of(g * SCAN_ROWS, SCAN_ROWS)
            a = a_ref[pl.ds(r0, SCAN_ROWS), cs]
            b = b_ref[pl.ds(r0, SCAN_ROWS), cs]
            a_top, b_top = local_scan(a[:half], b[:half])
            a_bot, b_bot = local_scan(a[half:], b[half:])
            h_top = a_top * carry + b_top
            h_bot = a_bot * h_top[half - 1:half, :] + b_bot
            h = jnp.concatenate([h_top, h_bot], axis=0)
            o_ref[pl.ds(r0, SCAN_ROWS), cs] = (
                h * y_ref[pl.ds(r0, SCAN_ROWS), cs].astype(F32)).astype(o_ref.dtype)
            return h_bot[half - 1:half, :]

        hc_ref[0:1, cs] = lax.fori_loop(0, tr // SCAN_ROWS, body, hc_ref[0:1, cs])


def rglru(xc, y, w_a, b_a, w_x, b_x, a_param):
    m, d = xc.shape
    tr = TR_LRU
    row = pl.BlockSpec((tr, d), lambda i: (i, 0))
    vec = pl.BlockSpec((1, d), lambda i: (0, 0))
    wspec = pl.BlockSpec((LRU_BLOCKS, LRU_BDIM, LRU_BDIM), lambda i: (0, 0, 0))
    nbytes = 4 * tr * d * 4 + 2 * tr * d * 2 + 2 * tr * d * 4 + 4 * LRU_BLOCKS * LRU_BDIM * LRU_BDIM * 2
    return pl.pallas_call(
        functools.partial(_lru_kernel, tr=tr),
        out_shape=jax.ShapeDtypeStruct((m, d), BF16),
        grid=(m // tr,),
        in_specs=[row, row, wspec, wspec, vec, vec, vec],
        out_specs=row,
        scratch_shapes=[pltpu.VMEM((tr, d), F32), pltpu.VMEM((tr, d), F32),
                        pltpu.VMEM((V7X_SUBLANES, d), F32)],
        compiler_params=_params(("arbitrary",), nbytes),
        name="rglru",
    )(xc, y, (0.5 * w_a).astype(BF16), (0.5 * w_x).astype(BF16), 0.5 * b_a.reshape(1, d),
      0.5 * b_x.reshape(1, d), a_param.reshape(1, d))


def _hgrn_kernel(q_ref, f_ref, i_ref, g_ref, lbl_ref, gnw_ref, o_ref, st_ref, *, tr, layer_j):
    step = pl.program_id(1)

    @pl.when(step == 0)
    def _():
        st_ref[...] = jnp.zeros_like(st_ref)

    logits = lbl_ref[...]
    e = jnp.exp(logits - jnp.max(logits, axis=0, keepdims=True))
    lb = jnp.sum(e[0:layer_j + 1], axis=0, keepdims=True) / jnp.sum(e, axis=0, keepdims=True)

    width = HGRN_HEADS_PER_STEP * A_KDIM
    valid = (step * tr + _iota((tr, 1), 0)) >= PAD
    q = jax.nn.silu(q_ref[...].astype(F32))
    forget = lb + (1.0 - lb) * jax.nn.sigmoid(f_ref[...].astype(F32))
    k = jnp.where(valid, 1.0 - forget, 0.0)
    g = jnp.where(valid, jnp.log(forget), 0.0)
    v = i_ref[...]

    pos = _iota((tr, width), 0) & (HGRN_CHUNK - 1)
    b = g
    shift = 1
    while shift < HGRN_CHUNK:
        b = b + jnp.where(pos >= shift, pltpu.roll(b, shift, 0), 0.0)
        shift *= 2

    n_sub = HGRN_CHUNK // HGRN_SUB
    cpos = _iota((HGRN_CHUNK, A_KDIM), 0)
    causal = _iota((HGRN_CHUNK, HGRN_CHUNK), 0) >= _iota((HGRN_CHUNK, HGRN_CHUNK), 1)
    gnw = gnw_ref[...]
    heads = range(HGRN_HEADS_PER_STEP)
    chunks = range(tr // HGRN_CHUNK)
    pairs = [(hh, c) for hh in heads for c in chunks]

    def part(x, p):
        hh, c = p
        return x[c * HGRN_CHUNK:(c + 1) * HGRN_CHUNK, hh * A_KDIM:(hh + 1) * A_KDIM]

    bc = {p: part(b, p) for p in pairs}
    qc = {p: part(q, p) for p in pairs}
    kc = {p: part(k, p) for p in pairs}
    vcb = {p: part(v, p).astype(BF16) for p in pairs}
    b_last = {p: bc[p][HGRN_CHUNK - 1:HGRN_CHUNK] for p in pairs}
    u_t = {p: _dot_tn(vcb[p], (kc[p] * jnp.exp(b_last[p] - bc[p])).astype(BF16)) for p in pairs}
    att = {}
    for p in pairs:
        rows = []
        for i in range(n_sub):
            ss = slice(i * HGRN_SUB, (i + 1) * HGRN_SUB)
            ref = jnp.zeros((1, A_KDIM), F32) if i == 0 else bc[p][i * HGRN_SUB - 1:i * HGRN_SUB]
            q_sc = (qc[p][ss] * jnp.exp(bc[p][ss] - ref)).astype(BF16)
            expo = jnp.where(cpos < (i + 1) * HGRN_SUB, ref - bc[p], 0.0)
            k_sc = (kc[p] * jnp.exp(expo)).astype(BF16)
            rows.append(_dot_nt(q_sc, k_sc))
        att[p] = jnp.where(causal, jnp.concatenate(rows, axis=0), 0.0).astype(BF16)
    o_intra = {p: _dot(att[p], vcb[p]) for p in pairs}
    states = {}
    for hh in heads:
        st = st_ref[hh]
        for c in chunks:
            states[(hh, c)] = st.astype(BF16)
            st = st * jnp.exp(b_last[(hh, c)]) + u_t[(hh, c)]
        st_ref[hh] = st
    o_inter = {p: _dot_nt((qc[p] * jnp.exp(bc[p])).astype(BF16), states[p]) for p in pairs}
    for hh, c in pairs:
        rs = slice(c * HGRN_CHUNK, (c + 1) * HGRN_CHUNK)
        cs = slice(hh * A_VDIM, (hh + 1) * A_VDIM)
        gate = jax.nn.silu(g_ref[rs, cs].astype(F32))
        o_ref[rs, cs] = (_rms(o_inter[(hh, c)] + o_intra[(hh, c)], gnw) * gate).astype(o_ref.dtype)


def hgrn2(hproj, lb_logits, gn_w, layer_j):
    m = hproj.shape[0]
    tr = TR_HGRN
    nrow = lb_logits.shape[0]
    hps = HGRN_HEADS_PER_STEP
    groups = A_HEADS // hps

    def col(off):
        return pl.BlockSpec((tr, hps * A_KDIM), lambda h, t, off=off: (t, h + off * groups))

    nbytes = 2 * 4 * tr * hps * A_KDIM * 2 + 2 * tr * hps * A_VDIM * 2 + 16 * tr * hps * A_KDIM * 4
    return pl.pallas_call(
        functools.partial(_hgrn_kernel, tr=tr, layer_j=layer_j),
        out_shape=jax.ShapeDtypeStruct((m, A_WIDTH), BF16),
        grid=(groups, m // tr),
        in_specs=[col(0), col(1), col(2), col(3),
                  pl.BlockSpec((nrow, hps * A_KDIM), lambda h, t: (0, h)),
                  pl.BlockSpec((1, A_VDIM), lambda h, t: (0, 0))],
        out_specs=pl.BlockSpec((tr, hps * A_VDIM), lambda h, t: (t, h)),
        scratch_shapes=[pltpu.VMEM((hps, A_VDIM, A_KDIM), F32)],
        compiler_params=_params(("arbitrary", "arbitrary"), nbytes),
        name="hgrn2",
    )(hproj, hproj, hproj, hproj, lb_logits, gn_w.reshape(1, A_VDIM))


PAIRS = B_GROUP // 2
QROWS = PAIRS * ATTN_BLOCK
SWA_ROW_GROUPS = 2


def _swa_kernel(sink_ref, q_ref, kc_ref, kp_ref, km_ref, vc_ref, vp_ref, vm_ref, o_ref):
    n = pl.program_id(0)
    scale = B_HDIM ** -0.5
    lane_lo = _iota((1, V7X_LANES), 1) < B_HDIM
    tq = _iota((QROWS, ATTN_BLOCK), 0) & (ATTN_BLOCK - 1)
    sk = _iota((QROWS, ATTN_BLOCK), 1)
    masks = (sk > tq + jnp.where(n >= 2, 0, 2 * ATTN_BLOCK),
             sk <= tq + jnp.where(n >= 1, 0, -2 * ATTN_BLOCK),
             jnp.logical_and(sk >= PAD, sk <= tq + jnp.where(n >= 1, ATTN_BLOCK, 0)))

    def split(x, natural_lo, fill):
        rolled = pltpu.roll(x, B_HDIM, 1)
        lo_src, hi_src = (x, rolled) if natural_lo else (rolled, x)
        return (jnp.where(lane_lo, lo_src, fill).astype(BF16), jnp.where(lane_lo, fill, hi_src).astype(BF16))

    chains = [(h, par) for h in range(B_KVHEADS) for par in range(2)]
    keys, vals = {}, {}
    for h in range(B_KVHEADS):
        tile = slice((h // 2) * V7X_LANES, (h // 2 + 1) * V7X_LANES)
        nat = h % 2 == 0
        keys[h] = [split(r[:, tile].astype(F32), nat, 0.0) for r in (kp_ref, kc_ref, km_ref)]
        vals[h] = [split(r[:, tile].astype(F32), nat, 1.0) for r in (vp_ref, vc_ref, vm_ref)]
    per = PAIRS // SWA_ROW_GROUPS
    rows = per * ATTN_BLOCK
    gmasks = [m[:rows] for m in masks]
    for grp in range(SWA_ROW_GROUPS):
        pairs = range(grp * per, (grp + 1) * per)
        qs = {h: (jnp.concatenate(
            [q_ref[:, (h * PAIRS + p) * V7X_LANES:(h * PAIRS + p + 1) * V7X_LANES] for p in pairs],
            axis=0).astype(F32) * scale).astype(BF16) for h in range(B_KVHEADS)}
        logits = {c: [jnp.where(m, _dot_nt(qs[c[0]], kk[c[1]]), NEG_INF) for m, kk in zip(gmasks, keys[c[0]])]
                  for c in chains}
        sinks = {(h, par): jnp.concatenate(
            [jnp.full((ATTN_BLOCK, V7X_LANES), sink_ref[h * B_GROUP + 2 * p + par], F32) for p in pairs],
            axis=0) for h, par in chains}
        mx = {c: jnp.maximum(jnp.broadcast_to(
            jnp.max(jnp.maximum(jnp.maximum(logits[c][0], logits[c][1]), logits[c][2]), axis=-1, keepdims=True),
            (rows, V7X_LANES)), sinks[c]) for c in chains}
        pv = {}
        for c in chains:
            for lg, vv in zip(logits[c], vals[c[0]]):
                part = _dot(jnp.exp(lg - mx[c]).astype(BF16), vv[c[1]])
                pv[c] = part if c not in pv else pv[c] + part
        out = {}
        for c in chains:
            den = pltpu.roll(pv[c], B_HDIM, 1) + jnp.exp(sinks[c] - mx[c])
            own_half = lane_lo if c[1] == 0 else jnp.logical_not(lane_lo)
            part = jnp.where(own_half, pv[c] / den, 0.0)
            out[c[0]] = part if c[0] not in out else out[c[0]] + part
        for h in range(B_KVHEADS):
            for i, p in enumerate(pairs):
                o_ref[:, (h * PAIRS + p) * V7X_LANES:(h * PAIRS + p + 1) * V7X_LANES] = (
                    out[h][i * ATTN_BLOCK:(i + 1) * ATTN_BLOCK].astype(o_ref.dtype))


def swa(hproj, sinks):
    m = hproj.shape[0]
    blk = ATTN_BLOCK
    q_col = (2 * A_FDIM + 2 * A_WIDTH) // B_WIDTH
    k_col = (2 * A_FDIM + 2 * A_WIDTH + B_WIDTH) // B_KVWIDTH
    v_col = k_col + 1
    qspec = pl.BlockSpec((blk, B_WIDTH), lambda n: (n, q_col))

    def kv(col, which):
        if which == "cur":
            return pl.BlockSpec((blk, B_KVWIDTH), lambda n: (n, col))
        if which == "prev":
            return pl.BlockSpec((blk, B_KVWIDTH), lambda n: (jnp.maximum(n - 1, 0), col))
        return pl.BlockSpec((blk, B_KVWIDTH), lambda n: (0, col))

    nbytes = 2 * blk * B_WIDTH * 4 + 12 * blk * B_KVWIDTH * 4 + 2 * blk * B_WIDTH * 2 + 40 * QROWS * 128 * 4
    return pl.pallas_call(
        _swa_kernel,
        out_shape=jax.ShapeDtypeStruct((m, B_WIDTH), BF16),
        grid=(m // blk,),
        in_specs=[pl.BlockSpec(memory_space=pltpu.SMEM), qspec,
                  kv(k_col, "cur"), kv(k_col, "prev"), kv(k_col, "meta"),
                  kv(v_col, "cur"), kv(v_col, "prev"), kv(v_col, "meta")],
        out_specs=pl.BlockSpec((blk, B_WIDTH), lambda n: (n, 0)),
        compiler_params=_params(("arbitrary",), nbytes),
        name="swa",
    )(sinks, hproj, hproj, hproj, hproj, hproj, hproj, hproj)


def kernel(x, meta_tokens, norm_w, w_in_even, lb_logits, hgrn_gn_w, attn_sinks, w_out_even,
           w_in_odd, lru_conv_w, lru_conv_b, lru_wa, lru_ba, lru_wx, lru_bx, lru_a_param, w_out_odd,
           ffn_w_gu, ffn_conv_w, ffn_conv_b, ffn_w_down):
    assert x.shape == (1, SEQ, D_MODEL) and norm_w.shape[0] == DEPTH and w_in_even.shape[-1] == EVEN_IN
    x2d = x[0]
    xn = rms_cast_stream(x2d, meta_tokens, norm_w[0, 0])
    h, out = None, None
    for layer in range(DEPTH):
        j = layer // 2
        if layer % 2 == 0:
            hproj = matmul([xn], w_in_even, layer=j, tm=TM_IN_EVEN, tn=TN_IN_EVEN, out_dtype=BF16,
                           name="in_even")
            o_a = hgrn2(hproj, lb_logits, hgrn_gn_w[j], j)
            o_b = swa(hproj, attn_sinks[j])
            mix = matmul([o_a, o_b], w_out_even, layer=j, tm=TM_OUT, tn=TN_OUT, out_dtype=BF16,
                         name="out_even")
        else:
            y_br, x_br = in_odd(xn, w_in_odd, lru_conv_w, lru_conv_b, j)
            rec = rglru(x_br, y_br, lru_wa[j], lru_ba[j], lru_wx[j], lru_bx[j], lru_a_param[j])
            mix = matmul([rec], w_out_odd, layer=j, tm=TM_OUT, tn=TN_OUT, out_dtype=BF16, name="out_odd")
        if h is None:
            h, xn = resid_norm_stream(x2d, meta_tokens, mix, norm_w[layer, 1], norm_w[layer, 2])
        else:
            h, xn = resid_norm(h, mix, norm_w[layer, 1], norm_w[layer, 2])
        act, w_down_bf16 = ffn_up(xn, ffn_w_gu, ffn_conv_w, ffn_conv_b, ffn_w_down, layer)
        ff = matmul([act], w_down_bf16, tm=TM_DOWN, tn=TN_DOWN, out_dtype=BF16, name="ffn_down")
        if layer + 1 < DEPTH:
            h, xn = resid_norm(h, ff, norm_w[layer, 3], norm_w[layer + 1, 0])
        else:
            out = resid_final(h, ff, norm_w[layer, 3])
    return out[None]
```
